```python
import math
import jax, jax.numpy as jnp
from jax import lax
import numpy as np

D_MODEL = 4096
BATCH = 2
SEQ = 4096
DEPTH = 2

GRID_W = 64
CTX_LEN = 256
EPS = 1e-6
ROPE_THETA = 10000.0
Q_BLOCK = 128
TOK_BLOCK = 128

A_HEADS = 16
A_NOPE = 128
A_ROPE = 64
A_VDIM = 128
A_QLORA = 768
A_KVLORA = 256

B_HEADS = 8
B_DH = 128
B_WIDTH = B_HEADS * B_DH
WIN_H = 8
WIN_W = 16

C_WIDTH = 1024
POOL_SIZES = (2, 4, 8, 16)
C_GROUPS = 4
C_GW = C_WIDTH // C_GROUPS

N_BRANCH = 3
IN_WIDTHS = (A_QLORA, A_KVLORA, A_ROPE, B_WIDTH, B_WIDTH, B_WIDTH, C_WIDTH, N_BRANCH * D_MODEL)
IN_W = sum(IN_WIDTHS)
IN_SPLITS = tuple(int(v) for v in np.cumsum(IN_WIDTHS)[:-1])

P_HEADS = 8
P_NKEYS = 128
P_EXPERTS = P_NKEYS * P_NKEYS
P_DK = 256
P_TOPK = 16

kernel_name = 'hybrid_mla_natten_pool_peer_dit'


def rmsnorm(x, g):
    xf = x.astype(jnp.float32)
    y = xf * lax.rsqrt(jnp.mean(xf * xf, axis=-1, keepdims=True) + EPS)
    return (y * g.astype(jnp.float32)).astype(x.dtype)


def axial_rope(n_tok):
    t = jnp.arange(n_tok)
    row = (t // GRID_W).astype(jnp.float32)
    col = (t % GRID_W).astype(jnp.float32)
    n_freq = A_ROPE // 4
    freqs = ROPE_THETA ** (-jnp.arange(n_freq, dtype=jnp.float32) / n_freq)
    ang = jnp.concatenate([row[:, None] * freqs, col[:, None] * freqs], axis=-1)
    return jnp.cos(ang), jnp.sin(ang)


def apply_rope(x, cos, sin):
    x1, x2 = jnp.split(x.astype(jnp.float32), 2, axis=-1)
    return jnp.concatenate([x1 * cos - x2 * sin, x1 * sin + x2 * cos], axis=-1).astype(x.dtype)


def mla_project(cq, ckv, q_norm_g, kv_norm_g, w_uq, w_ukv):
    B, T = cq.shape[:2]
    q = (rmsnorm(cq, q_norm_g) @ w_uq).reshape(B, T, A_HEADS, A_NOPE + A_ROPE)
    kv = (rmsnorm(ckv, kv_norm_g) @ w_ukv).reshape(B, T, A_HEADS, A_NOPE + A_VDIM)
    return q[..., :A_NOPE], q[..., A_NOPE:], kv[..., :A_NOPE], kv[..., A_NOPE:]


def mla_attend(qn, qr, kn, kr, v):
    s = jnp.einsum('bqhd,bkhd->bhqk', qn, kn) + jnp.einsum('bqhr,bkr->bhqk', qr, kr)
    p = jax.nn.softmax(s.astype(jnp.float32) * (A_NOPE + A_ROPE) ** -0.5, axis=-1).astype(v.dtype)
    return jnp.einsum('bhqk,bkhd->bqhd', p, v)


def mla_latent(qn, qr, kn, kr, v):
    B, S = qn.shape[:2]
    nb = S // Q_BLOCK

    def blocks(t):
        return jnp.moveaxis(t.reshape((B, nb, Q_BLOCK) + t.shape[2:]), 1, 0)

    o = lax.map(lambda a: mla_attend(a[0], a[1], kn, kr, v), (blocks(qn), blocks(qr)))
    return jnp.moveaxis(o, 0, 1).reshape(B, S, A_HEADS * A_VDIM)


def dense_attend(q, k, v):
    B, T, H, d = q.shape
    s = jnp.einsum('bqhd,bkhd->bhqk', q, k).astype(jnp.float32) * d ** -0.5
    p = jax.nn.softmax(s, axis=-1).astype(v.dtype)
    return jnp.einsum('bhqk,bkhd->bqhd', p, v).reshape(B, T, H * d)


def na_latent(q, k, v, kc, vc, rel_bias):
    B, S = q.shape[:2]
    rows = S // GRID_W
    kh = min(WIN_H, rows)
    n_ctx = kc.shape[1]
    scale = B_DH ** -0.5
    cols = jnp.arange(GRID_W)
    c0 = jnp.clip(cols - WIN_W // 2, 0, GRID_W - WIN_W)
    key_cols = c0[:, None] + jnp.arange(WIN_W)[None, :]
    bj = key_cols - cols[:, None] + (WIN_W - 1)
    q_rows = jnp.moveaxis(q.reshape(B, rows, GRID_W, B_HEADS, B_DH), 1, 0)

    def row_block(args):
        r, q_r = args
        r0 = jnp.clip(r - kh // 2, 0, rows - kh)
        key_rows = r0 + jnp.arange(kh)
        idx = (key_rows[None, :, None] * GRID_W + key_cols[:, None, :]).reshape(GRID_W, kh * WIN_W)
        k_w = k[:, idx]
        v_w = v[:, idx]
        bi = key_rows - r + (WIN_H - 1)
        rel = rel_bias[:, bi[None, :, None], bj[:, None, :]].reshape(B_HEADS, GRID_W, kh * WIN_W)
        s_ctx = jnp.einsum('bqhd,bkhd->bhqk', q_r, kc).astype(jnp.float32) * scale
        s_loc = jnp.einsum('bqhd,bqkhd->bhqk', q_r, k_w).astype(jnp.float32) * scale + rel.astype(jnp.float32)[None]
        p = jax.nn.softmax(jnp.concatenate([s_ctx, s_loc], axis=-1), axis=-1).astype(v.dtype)
        return (jnp.einsum('bhqk,bkhd->bqhd', p[..., :n_ctx], vc)
                + jnp.einsum('bhqk,bqkhd->bqhd', p[..., n_ctx:], v_w))

    o = lax.map(row_block, (jnp.arange(rows), q_rows))
    return jnp.moveaxis(o, 0, 1).reshape(B, S, B_WIDTH)


def pool_mix(x, pool_w, pool_scale):
    B, T, _ = x.shape
    xf = x.astype(jnp.float32)
    cs = jnp.concatenate([jnp.zeros((B, 1, C_WIDTH), jnp.float32), jnp.cumsum(xf, axis=1)], axis=1)
    t = jnp.arange(T)
    outs = []
    for gi, w in enumerate(POOL_SIZES):
        lo = jnp.clip(t - w // 2, 0, T)
        hi = jnp.clip(t + (w - w // 2), 0, T)
        sl = slice(gi * C_GW, (gi + 1) * C_GW)
        mean = (cs[:, hi, sl] - cs[:, lo, sl]) / (hi - lo).astype(jnp.float32)[None, :, None]
        outs.append(mean - xf[:, :, sl])
    d = jnp.stack(outs, axis=2).astype(x.dtype)
    y = jnp.einsum('btgi,gio->btgo', d, pool_w).reshape(B, T, C_WIDTH)
    return y * pool_scale


def merge_branches(gates, a, b, cc, w_br_a, w_br_b, w_br_c, w_out):
    g_a, g_b, g_c = jnp.split(jax.nn.sigmoid(gates.astype(jnp.float32)).astype(gates.dtype), N_BRANCH, axis=-1)
    m = g_a * (a @ w_br_a) + g_b * (b @ w_br_b) + g_c * (cc @ w_br_c)
    return m @ w_out


def token_mixer(h_lat, h_ctx, with_ctx_out, rope_cos, rope_sin, w_in, q_norm_g, kv_norm_g, w_uq, w_ukv,
                na_rel_bias, pool_w, pool_scale, w_br_a, w_br_b, w_br_c, w_out):
    B, S = h_lat.shape[:2]
    n_ctx = h_ctx.shape[1]
    cq_l, ckv_l, kr_l, nq_l, nk_l, nv_l, pin_l, gate_l = jnp.split(h_lat @ w_in, IN_SPLITS, axis=-1)
    cq_c, ckv_c, kr_c, nq_c, nk_c, nv_c, pin_c, gate_c = jnp.split(h_ctx @ w_in, IN_SPLITS, axis=-1)

    qn_l, qr_l, kn_l, v_l = mla_project(cq_l, ckv_l, q_norm_g, kv_norm_g, w_uq, w_ukv)
    qr_l = apply_rope(qr_l, rope_cos[:, None, :], rope_sin[:, None, :])
    kr_l = apply_rope(kr_l, rope_cos, rope_sin)
    qn_c, qr_c, kn_c, v_c = mla_project(cq_c, ckv_c, q_norm_g, kv_norm_g, w_uq, w_ukv)
    kn_all = jnp.concatenate([kn_c, kn_l], axis=1)
    kr_all = jnp.concatenate([kr_c, kr_l], axis=1)
    v_all = jnp.concatenate([v_c, v_l], axis=1)
    a_lat = mla_latent(qn_l, qr_l, kn_all, kr_all, v_all)

    def heads(t):
        return t.reshape(t.shape[0], t.shape[1], B_HEADS, B_DH)

    nk_c, nv_c = heads(nk_c), heads(nv_c)
    b_lat = na_latent(heads(nq_l), heads(nk_l), heads(nv_l), nk_c, nv_c, na_rel_bias)

    c_lat = pool_mix(pin_l, pool_w, pool_scale)
    out_lat = merge_branches(gate_l, a_lat, b_lat, c_lat, w_br_a, w_br_b, w_br_c, w_out)
    if not with_ctx_out:
        return out_lat, None

    a_ctx = mla_attend(qn_c, qr_c, kn_c, kr_c, v_c).reshape(B, n_ctx, A_HEADS * A_VDIM)
    b_ctx = dense_attend(heads(nq_c), nk_c, nv_c)
    c_ctx_out = pool_mix(pin_c, pool_w, pool_scale)
    out_ctx = merge_branches(gate_c, a_ctx, b_ctx, c_ctx_out, w_br_a, w_br_b, w_br_c, w_out)
    return out_lat, out_ctx


def peer_ffn(h, wq, k1, k2, u, v):
    B, T, D = h.shape
    n = B * T
    hf = h.reshape(n, D)
    q = (hf @ wq).reshape(n, P_HEADS, 2, P_DK // 2)
    s1 = jnp.einsum('nhd,hkd->nhk', q[:, :, 0], k1).astype(jnp.float32)
    s2 = jnp.einsum('nhd,hkd->nhk', q[:, :, 1], k2).astype(jnp.float32)
    v1, i1 = lax.top_k(s1, P_TOPK)
    v2, i2 = lax.top_k(s2, P_TOPK)
    cand_s = (v1[..., :, None] + v2[..., None, :]).reshape(n, P_HEADS, P_TOPK * P_TOPK)
    cand_i = (i1[..., :, None] * P_NKEYS + i2[..., None, :]).reshape(n, P_HEADS, P_TOPK * P_TOPK)
    top_s, pos = lax.top_k(cand_s, P_TOPK)
    experts = jnp.take_along_axis(cand_i, pos, axis=-1)
    gates = jax.nn.softmax(top_s, axis=-1).astype(h.dtype)
    nb = n // TOK_BLOCK
    hk = P_HEADS * P_TOPK

    def block(args):
        x_b, e_b, g_b = args
        u_b = u[e_b]
        v_b = v[e_b]
        act = jax.nn.gelu(jnp.einsum('td,tkd->tk', x_b, u_b), approximate=False)
        return jnp.einsum('tk,tkd->td', g_b * act, v_b)

    out = lax.map(block, (hf.reshape(nb, TOK_BLOCK, D), experts.reshape(nb, TOK_BLOCK, hk),
                          gates.reshape(nb, TOK_BLOCK, hk)))
    return out.reshape(B, T, D)


def setup_inputs(seed: int = 0) -> dict:
    key = jax.random.key(seed)
    ks = jax.random.split(key, 26)
    L, D = DEPTH, D_MODEL

    def nrm(k, shape, scale):
        return jax.random.normal(k, shape, jnp.float32) * scale

    return {
        'x': nrm(ks[0], (BATCH, SEQ, D), 1.0),
        'c': nrm(ks[1], (BATCH, D), 1.0),
        'ctx': nrm(ks[2], (BATCH, CTX_LEN, D), 1.0),
        'c_ctx': nrm(ks[3], (D,), 1.0),
        'w_ada': nrm(ks[4], (L, D, 6 * D), 0.5 * D ** -0.5),
        'b_ada': nrm(ks[5], (L, 6 * D), 0.02),
        'norm1_g': 1.0 + nrm(ks[6], (L, D), 0.02),
        'norm2_g': 1.0 + nrm(ks[7], (L, D), 0.02),
        'w_in': nrm(ks[8], (L, D, IN_W), D ** -0.5),
        'q_norm_g': 1.0 + nrm(ks[9], (L, A_QLORA), 0.02),
        'kv_norm_g': 1.0 + nrm(ks[10], (L, A_KVLORA), 0.02),
        'w_uq': nrm(ks[11], (L, A_QLORA, A_HEADS * (A_NOPE + A_ROPE)), A_QLORA ** -0.5),
        'w_ukv': nrm(ks[12], (L, A_KVLORA, A_HEADS * (A_NOPE + A_VDIM)), A_KVLORA ** -0.5),
        'na_rel_bias': nrm(ks[13], (L, B_HEADS, 2 * WIN_H - 1, 2 * WIN_W - 1), 0.1),
        'pool_w': nrm(ks[14], (L, C_GROUPS, C_GW, C_GW), C_GW ** -0.5),
        'pool_scale': 1.0 + nrm(ks[15], (L, C_WIDTH), 0.02),
        'w_br_a': nrm(ks[16], (L, A_HEADS * A_VDIM, D), (A_HEADS * A_VDIM) ** -0.5),
        'w_br_b': nrm(ks[17], (L, B_WIDTH, D), B_WIDTH ** -0.5),
        'w_br_c': nrm(ks[18], (L, C_WIDTH, D), C_WIDTH ** -0.5),
        'w_out': nrm(ks[19], (L, D, D), D ** -0.5),
        'peer_wq': nrm(ks[20], (L, D, P_HEADS * P_DK), D ** -0.5),
        'peer_k1': nrm(ks[21], (L, P_HEADS, P_NKEYS, P_DK // 2), (P_DK // 2) ** -0.5),
        'peer_k2': nrm(ks[22], (L, P_HEADS, P_NKEYS, P_DK // 2), (P_DK // 2) ** -0.5),
        'peer_u': nrm(ks[23], (L, P_EXPERTS, D), D ** -0.5),
        'peer_v': nrm(ks[24], (L, P_EXPERTS, D), (P_HEADS * P_TOPK) ** -0.5),
        'final_g': 1.0 + nrm(ks[25], (D,), 0.02),
    }


def reference(x, c, ctx, c_ctx, w_ada, b_ada, norm1_g, norm2_g, w_in, q_norm_g, kv_norm_g, w_uq, w_ukv,
              na_rel_bias, pool_w, pool_scale, w_br_a, w_br_b, w_br_c, w_out, peer_wq, peer_k1, peer_k2,
              peer_u, peer_v, final_g):
    S = x.shape[1]
    rope_cos, rope_sin = axial_rope(S)
    x_lat, x_ctx = x, ctx
    for l in range(DEPTH):
        ctx_out = l < DEPTH - 1
        sh1, sc1, g1, sh2, sc2, g2 = jnp.split((jax.nn.silu(c) @ w_ada[l] + b_ada[l])[:, None, :], 6, axis=-1)
        sh1c, sc1c, g1c, sh2c, sc2c, g2c = jnp.split(jax.nn.silu(c_ctx) @ w_ada[l] + b_ada[l], 6, axis=-1)
        h_lat = rmsnorm(x_lat, norm1_g[l]) * (1 + sc1) + sh1
        h_ctx = rmsnorm(x_ctx, norm1_g[l]) * (1 + sc1c) + sh1c
        mix_lat, mix_ctx = token_mixer(h_lat, h_ctx, ctx_out, rope_cos, rope_sin, w_in[l], q_norm_g[l],
                                       kv_norm_g[l], w_uq[l], w_ukv[l], na_rel_bias[l], pool_w[l],
                                       pool_scale[l], w_br_a[l], w_br_b[l], w_br_c[l], w_out[l])
        x_lat = x_lat + g1 * mix_lat
        h2 = rmsnorm(x_lat, norm2_g[l]) * (1 + sc2) + sh2
        x_lat = x_lat + g2 * peer_ffn(h2, peer_wq[l], peer_k1[l], peer_k2[l], peer_u[l], peer_v[l])
        if ctx_out:
            x_ctx = x_ctx + g1c * mix_ctx
            h2c = rmsnorm(x_ctx, norm2_g[l]) * (1 + sc2c) + sh2c
            x_ctx = x_ctx + g2c * peer_ffn(h2c, peer_wq[l], peer_k1[l], peer_k2[l], peer_u[l], peer_v[l])
    return rmsnorm(x_lat, final_g)
```

```python
import functools

import numpy as np
import jax
import jax.numpy as jnp
from jax import lax
from jax.experimental import pallas as pl
from jax.experimental.pallas import tpu as pltpu

F32 = jnp.float32
BF16 = jnp.bfloat16

GRID_W = 64
EPS = 1e-6
ROPE_THETA = 10000.0

A_HEADS = 16
A_NOPE = 128
A_ROPE = 64
A_VDIM = 128
A_QLORA = 768
A_KVLORA = 256
A_HEAD_PAD = 256

B_HEADS = 8
B_DH = 128
B_WIDTH = B_HEADS * B_DH
WIN_H = 8
WIN_W = 16

C_WIDTH = 1024
POOL_SIZES = (2, 4, 8, 16)
C_GW = C_WIDTH // len(POOL_SIZES)
POOL_HALO = 8

P_HEADS = 8
P_NKEYS = 128
P_DK = 256
P_TOPK = 16

LANES = 128
SMALL_W = A_QLORA + A_KVLORA + LANES

NEG_BIG = -1e30
MIB = 1024 * 1024


def _cparams(n_axes, vmem_mib):
    return pltpu.CompilerParams(dimension_semantics=("arbitrary",) * n_axes,
                                vmem_limit_bytes=vmem_mib * MIB)


def _segment(row_tile, tile_rows, seq, n_batch):
    return jnp.minimum((row_tile * tile_rows) // seq, n_batch)


def _ada_kernel(c_ref, w_ref, b_ref, o_ref):
    c = c_ref[...]
    a = (c * jax.nn.sigmoid(c)).astype(BF16)
    o_ref[0] = jnp.dot(a, w_ref[0].astype(BF16), preferred_element_type=F32) + b_ref[0]


def ada_modulation(cvec, w_ada, b_ada):
    n_layers, d, n_out = w_ada.shape
    rows = cvec.shape[0]
    tn = 512
    return pl.pallas_call(
        _ada_kernel,
        grid=(n_layers, n_out // tn),
        in_specs=[pl.BlockSpec((rows, d), lambda l, j: (0, 0)),
                  pl.BlockSpec((1, d, tn), lambda l, j: (l, 0, j)),
                  pl.BlockSpec((1, 1, tn), lambda l, j: (l, 0, j))],
        out_specs=pl.BlockSpec((1, rows, tn), lambda l, j: (l, 0, j)),
        out_shape=jax.ShapeDtypeStruct((n_layers, rows, n_out), F32),
        compiler_params=_cparams(2, 40),
        name="ada_modulation",
    )(cvec, w_ada, b_ada.reshape(n_layers, 1, n_out))


def _rms(x, g):
    return x * lax.rsqrt(jnp.mean(x * x, axis=-1, keepdims=True) + EPS) * g


def _ln_mod_kernel(x_ref, g_ref, sc_ref, sh_ref, o_ref, *, transpose):
    h = _rms(x_ref[...], g_ref[...]) * (1.0 + sc_ref[0]) + sh_ref[0]
    if transpose:
        o_ref[...] = h.T.astype(o_ref.dtype)
    else:
        o_ref[...] = h.astype(o_ref.dtype)


def ln_modulate(x, g, mod3, shift_chunk, n_rows, seq, n_batch, transpose=False):
    d = x.shape[1]
    tm = 256
    seg = functools.partial(_segment, tile_rows=tm, seq=seq, n_batch=n_batch)
    if transpose:
        out_spec = pl.BlockSpec((d, tm), lambda i: (0, i))
        out_shape = jax.ShapeDtypeStruct((d, n_rows), BF16)
    else:
        out_spec = pl.BlockSpec((tm, d), lambda i: (i, 0))
        out_shape = jax.ShapeDtypeStruct((n_rows, d), BF16)
    return pl.pallas_call(
        functools.partial(_ln_mod_kernel, transpose=transpose),
        grid=(n_rows // tm,),
        in_specs=[pl.BlockSpec((tm, d), lambda i: (i, 0)),
                  pl.BlockSpec((1, d), lambda i: (0, 0)),
                  pl.BlockSpec((1, 1, d), lambda i: (seg(i), 0, shift_chunk + 1)),
                  pl.BlockSpec((1, 1, d), lambda i: (seg(i), 0, shift_chunk))],
        out_specs=out_spec,
        out_shape=out_shape,
        compiler_params=_cparams(1, 48),
        name="ln_modulate_t" if transpose else "ln_modulate",
    )(x, g.reshape(1, d), mod3, mod3)


def _final_norm_kernel(x_ref, g_ref, o_ref):
    o_ref[...] = _rms(x_ref[...], g_ref[...])


def final_norm(x, g, n_rows):
    d = x.shape[1]
    tm = 256
    return pl.pallas_call(
        _final_norm_kernel,
        grid=(n_rows // tm,),
        in_specs=[pl.BlockSpec((tm, d), lambda i: (i, 0)),
                  pl.BlockSpec((1, d), lambda i: (0, 0))],
        out_specs=pl.BlockSpec((tm, d), lambda i: (i, 0)),
        out_shape=jax.ShapeDtypeStruct((n_rows, d), F32),
        compiler_params=_cparams(1, 48),
        name="final_norm",
    )(x, g.reshape(1, d))


def _mm_kernel(a_ref, b_ref, o_ref, *, act):
    acc = jnp.dot(a_ref[...], b_ref[...], preferred_element_type=F32)
    if act == "sigmoid":
        acc = jax.nn.sigmoid(acc)
    o_ref[...] = acc.astype(o_ref.dtype)


def matmul(a, b, out_dtype, act=None, tn=1024, tm=512):
    m, k = a.shape
    n = b.shape[1]
    if n % tn:
        tn = n
    return pl.pallas_call(
        functools.partial(_mm_kernel, act=act),
        grid=(n // tn, m // tm),
        in_specs=[pl.BlockSpec((tm, k), lambda j, i: (i, 0)),
                  pl.BlockSpec((k, tn), lambda j, i: (0, j))],
        out_specs=pl.BlockSpec((tm, tn), lambda j, i: (i, j)),
        out_shape=jax.ShapeDtypeStruct((m, n), out_dtype),
        compiler_params=_cparams(2, 52),
        name="matmul_" + (act or "plain"),
    )(a, b)


def _mm_residual_kernel(a_ref, b_ref, x_ref, g_ref, o_ref):
    acc = jnp.dot(a_ref[...], b_ref[...], preferred_element_type=F32)
    o_ref[...] = x_ref[...] + g_ref[0] * acc


def matmul_gated_residual(a, b, x, mod3, gate_chunk, seq, n_batch, tn=1024, tm=512):
    m, k = a.shape
    n = b.shape[1]
    seg = functools.partial(_segment, tile_rows=tm, seq=seq, n_batch=n_batch)
    nj = n // tn
    return pl.pallas_call(
        _mm_residual_kernel,
        grid=(nj, m // tm),
        in_specs=[pl.BlockSpec((tm, k), lambda j, i: (i, 0)),
                  pl.BlockSpec((k, tn), lambda j, i: (0, j)),
                  pl.BlockSpec((tm, tn), lambda j, i: (i, j)),
                  pl.BlockSpec((1, 1, tn), lambda j, i: (seg(i), 0, gate_chunk * nj + j))],
        out_specs=pl.BlockSpec((tm, tn), lambda j, i: (i, j)),
        out_shape=jax.ShapeDtypeStruct((m, n), F32),
        compiler_params=_cparams(2, 52),
        name="matmul_gated_residual",
    )(a, b, x, mod3)


def _merge_kernel(a_ref, b_ref, c_ref, wa_ref, wb_ref, wc_ref, ga_ref, gb_ref, gc_ref, o_ref):
    m = ga_ref[...] * jnp.dot(a_ref[...], wa_ref[...], preferred_element_type=F32)
    m = m + gb_ref[...] * jnp.dot(b_ref[...], wb_ref[...], preferred_element_type=F32)
    m = m + gc_ref[...] * jnp.dot(c_ref[...], wc_ref[...], preferred_element_type=F32)
    o_ref[...] = m.astype(o_ref.dtype)


def merge_branches(a, b, c, wa, wb, wc, gates, tn=1024, tm=512):
    m = a.shape[0]
    n = wa.shape[1]
    nj = n // tn
    row = lambda j, i: (i, 0)
    col = lambda j, i: (0, j)
    return pl.pallas_call(
        _merge_kernel,
        grid=(nj, m // tm),
        in_specs=[pl.BlockSpec((tm, a.shape[1]), row),
                  pl.BlockSpec((tm, b.shape[1]), row),
                  pl.BlockSpec((tm, c.shape[1]), row),
                  pl.BlockSpec((wa.shape[0], tn), col),
                  pl.BlockSpec((wb.shape[0], tn), col),
                  pl.BlockSpec((wc.shape[0], tn), col),
                  pl.BlockSpec((tm, tn), lambda j, i: (i, j)),
                  pl.BlockSpec((tm, tn), lambda j, i: (i, nj + j)),
                  pl.BlockSpec((tm, tn), lambda j, i: (i, 2 * nj + j))],
        out_specs=pl.BlockSpec((tm, tn), lambda j, i: (i, j)),
        out_shape=jax.ShapeDtypeStruct((m, n), BF16),
        compiler_params=_cparams(2, 52),
        name="merge_branches",
    )(a, b, c, wa, wb, wc, gates, gates, gates)


def _rotate(v, c_ref, s1_ref, s2_ref):
    return (v * c_ref[...] + pltpu.roll(v, LANES - A_ROPE // 2, 1) * s1_ref[...]
            + pltpu.roll(v, A_ROPE // 2, 1) * s2_ref[...])


def _mla_q_kernel(cq_ref, g_ref, w_ref, c_ref, s1_ref, s2_ref, o_ref):
    y = _rms(cq_ref[...], g_ref[...]).astype(BF16)
    acc = jnp.dot(y, w_ref[...], preferred_element_type=F32)
    for h in range(A_HEADS):
        lo = h * A_HEAD_PAD
        o_ref[:, lo:lo + A_NOPE] = acc[:, lo:lo + A_NOPE].astype(o_ref.dtype)
        rot = _rotate(acc[:, lo + A_NOPE:lo + A_HEAD_PAD], c_ref, s1_ref, s2_ref)
        o_ref[:, lo + A_NOPE:lo + A_HEAD_PAD] = rot.astype(o_ref.dtype)


def mla_q(p_small, g, w_uq_pad, rope_c, rope_s1, rope_s2):
    nt = p_small.shape[0]
    tm = 256
    width = A_HEADS * A_HEAD_PAD
    tab = pl.BlockSpec((tm, LANES), lambda i: (i, 0))
    return pl.pallas_call(
        _mla_q_kernel,
        grid=(nt // tm,),
        in_specs=[pl.BlockSpec((tm, A_QLORA), lambda i: (i, 0)),
                  pl.BlockSpec((1, A_QLORA), lambda i: (0, 0)),
                  pl.BlockSpec((A_QLORA, width), lambda i: (0, 0)),
                  tab, tab, tab],
        out_specs=pl.BlockSpec((tm, width), lambda i: (i, 0)),
        out_shape=jax.ShapeDtypeStruct((nt, width), BF16),
        compiler_params=_cparams(1, 48),
        name="mla_q",
    )(p_small, g.reshape(1, A_QLORA), w_uq_pad, rope_c, rope_s1, rope_s2)


def _mla_kv_kernel(ckv_ref, kr_ref, g_ref, w_ref, c_ref, s1_ref, s2_ref, k_ref, v_ref):
    y = _rms(ckv_ref[...], g_ref[...]).astype(BF16)
    acc = jnp.dot(y, w_ref[...], preferred_element_type=F32)
    kr = _rotate(kr_ref[...], c_ref, s1_ref, s2_ref).astype(k_ref.dtype)
    for h in range(A_HEADS):
        lo = h * A_HEAD_PAD
        k_ref[:, lo:lo + A_NOPE] = acc[:, h * A_NOPE:(h + 1) * A_NOPE].astype(k_ref.dtype)
        k_ref[:, lo + A_NOPE:lo + A_HEAD_PAD] = kr
    v_ref[...] = acc[:, A_HEADS * A_NOPE:].astype(v_ref.dtype)


def mla_kv(p_small, g, w_ukv_perm, rope_c, rope_s1, rope_s2):
    nt = p_small.shape[0]
    tm = 256
    kw = A_HEADS * A_HEAD_PAD
    vw = A_HEADS * A_VDIM
    tab = pl.BlockSpec((tm, LANES), lambda i: (i, 0))
    return pl.pallas_call(
        _mla_kv_kernel,
        grid=(nt // tm,),
        in_specs=[pl.BlockSpec((tm, A_KVLORA), lambda i: (i, A_QLORA // A_KVLORA)),
                  pl.BlockSpec((tm, LANES), lambda i: (i, (A_QLORA + A_KVLORA) // LANES)),
                  pl.BlockSpec((1, A_KVLORA), lambda i: (0, 0)),
                  pl.BlockSpec((A_KVLORA, A_HEADS * (A_NOPE + A_VDIM)), lambda i: (0, 0)),
                  tab, tab, tab],
        out_specs=[pl.BlockSpec((tm, kw), lambda i: (i, 0)),
                   pl.BlockSpec((tm, vw), lambda i: (i, 0))],
        out_shape=[jax.ShapeDtypeStruct((nt, kw), BF16),
                   jax.ShapeDtypeStruct((nt, vw), BF16)],
        compiler_params=_cparams(1, 48),
        name="mla_kv",
    )(p_small, p_small, g.reshape(1, A_KVLORA), w_ukv_perm, rope_c, rope_s1, rope_s2)


def _qk(q, k):
    return lax.dot_general(q, k, (((1,), (1,)), ((), ())), preferred_element_type=F32)


def _attn_two_sets_kernel(q_ref, kl_ref, vl_ref, kc_ref, vc_ref, o_ref, *, scale):
    q = q_ref[...]
    s_l = _qk(q, kl_ref[...]) * scale
    s_c = _qk(q, kc_ref[...]) * scale
    m = jnp.maximum(jnp.max(s_l, axis=-1, keepdims=True), jnp.max(s_c, axis=-1, keepdims=True))
    p_l = jnp.exp(s_l - m)
    p_c = jnp.exp(s_c - m)
    den = jnp.sum(p_l, axis=-1, keepdims=True) + jnp.sum(p_c, axis=-1, keepdims=True)
    o = jnp.dot(p_l.astype(BF16), vl_ref[...], preferred_element_type=F32)
    o = o + jnp.dot(p_c.astype(BF16), vc_ref[...], preferred_element_type=F32)
    o_ref[...] = (o / den).astype(o_ref.dtype)


def mla_latent_attention(q, k, v, n_batch, seq, ctx_len, tq=256):
    ctx_blk0 = (n_batch * seq) // ctx_len
    nq = seq // tq
    scale = float((A_NOPE + A_ROPE) ** -0.5)
    return pl.pallas_call(
        functools.partial(_attn_two_sets_kernel, scale=scale),
        grid=(n_batch, A_HEADS, nq),
        in_specs=[pl.BlockSpec((tq, A_HEAD_PAD), lambda b, h, i: (b * nq + i, h)),
                  pl.BlockSpec((seq, A_HEAD_PAD), lambda b, h, i: (b, h)),
                  pl.BlockSpec((seq, A_VDIM), lambda b, h, i: (b, h)),
                  pl.BlockSpec((ctx_len, A_HEAD_PAD), lambda b, h, i: (ctx_blk0 + b, h)),
                  pl.BlockSpec((ctx_len, A_VDIM), lambda b, h, i: (ctx_blk0 + b, h))],
        out_specs=pl.BlockSpec((tq, A_VDIM), lambda b, h, i: (b * nq + i, h)),
        out_shape=jax.ShapeDtypeStruct((n_batch * seq, A_HEADS * A_VDIM), BF16),
        compiler_params=_cparams(3, 48),
        name="mla_latent_attention",
    )(q, k, v, k, v)


def _attn_one_set_kernel(q_ref, k_ref, v_ref, o_ref, *, scale):
    s = _qk(q_ref[...], k_ref[...]) * scale
    p = jnp.exp(s - jnp.max(s, axis=-1, keepdims=True))
    den = jnp.sum(p, axis=-1, keepdims=True)
    o = jnp.dot(p.astype(BF16), v_ref[...], preferred_element_type=F32)
    o_ref[...] = (o / den).astype(o_ref.dtype)


def context_attention(q, k, v, n_batch, seq, ctx_len, n_heads, dqk, dv, q_col0, k_col0, v_col0, scale):
    ctx_blk0 = (n_batch * seq) // ctx_len
    return pl.pallas_call(
        functools.partial(_attn_one_set_kernel, scale=scale),
        grid=(n_batch, n_heads),
        in_specs=[pl.BlockSpec((ctx_len, dqk), lambda b, h: (ctx_blk0 + b, q_col0 + h)),
                  pl.BlockSpec((ctx_len, dqk), lambda b, h: (ctx_blk0 + b, k_col0 + h)),
                  pl.BlockSpec((ctx_len, dv), lambda b, h: (ctx_blk0 + b, v_col0 + h))],
        out_specs=pl.BlockSpec((ctx_len, dv), lambda b, h: (b, h)),
        out_shape=jax.ShapeDtypeStruct((n_batch * ctx_len, n_heads * dv), BF16),
        compiler_params=_cparams(2, 32),
        name="context_attention",
    )(q, k, v)


def _na_kernel(q_ref, kl_ref, vl_ref, kc_ref, vc_ref, bias_ref, o_ref, *, rows, scale):
    r = pl.program_id(2)
    r0 = jnp.clip(r - WIN_H // 2, 0, rows - WIN_H)
    start = pl.multiple_of(r0 * GRID_W, GRID_W)
    win = WIN_H * GRID_W
    q = q_ref[...]
    s_l = _qk(q, kl_ref[pl.ds(start, win), :]) * scale + bias_ref[0, 0]
    s_c = _qk(q, kc_ref[...]) * scale
    m = jnp.maximum(jnp.max(s_l, axis=-1, keepdims=True), jnp.max(s_c, axis=-1, keepdims=True))
    p_l = jnp.exp(s_l - m)
    p_c = jnp.exp(s_c - m)
    den = jnp.sum(p_l, axis=-1, keepdims=True) + jnp.sum(p_c, axis=-1, keepdims=True)
    o = jnp.dot(p_l.astype(BF16), vl_ref[pl.ds(start, win), :], preferred_element_type=F32)
    o = o + jnp.dot(p_c.astype(BF16), vc_ref[...], preferred_element_type=F32)
    o_ref[...] = (o / den).astype(o_ref.dtype)


def na_latent_attention(qkv, bias, n_batch, seq, ctx_len):
    rows = seq // GRID_W
    ctx_blk0 = (n_batch * seq) // ctx_len
    win = WIN_H * GRID_W

    def bias_idx(b, h, r):
        return (r - jnp.clip(r - WIN_H // 2, 0, rows - WIN_H), h, 0, 0)

    return pl.pallas_call(
        functools.partial(_na_kernel, rows=rows, scale=float(B_DH ** -0.5)),
        grid=(n_batch, B_HEADS, rows),
        in_specs=[pl.BlockSpec((GRID_W, B_DH), lambda b, h, r: (b * rows + r, h)),
                  pl.BlockSpec((seq, B_DH), lambda b, h, r: (b, B_HEADS + h)),
                  pl.BlockSpec((seq, B_DH), lambda b, h, r: (b, 2 * B_HEADS + h)),
                  pl.BlockSpec((ctx_len, B_DH), lambda b, h, r: (ctx_blk0 + b, B_HEADS + h)),
                  pl.BlockSpec((ctx_len, B_DH), lambda b, h, r: (ctx_blk0 + b, 2 * B_HEADS + h)),
                  pl.BlockSpec((1, 1, GRID_W, win), bias_idx)],
        out_specs=pl.BlockSpec((GRID_W, B_DH), lambda b, h, r: (b * rows + r, h)),
        out_shape=jax.ShapeDtypeStruct((n_batch * seq, B_WIDTH), BF16),
        compiler_params=_cparams(3, 32),
        name="na_latent_attention",
    )(qkv, qkv, qkv, qkv, qkv, bias)


def na_bias_table(rel_bias):
    d = jnp.arange(WIN_H)[:, None, None, None]
    c = jnp.arange(GRID_W)[None, :, None, None]
    kr = jnp.arange(WIN_H)[None, None, :, None]
    kc = jnp.arange(GRID_W)[None, None, None, :]
    c0 = jnp.clip(c - WIN_W // 2, 0, GRID_W - WIN_W)
    inside = (kc >= c0) & (kc < c0 + WIN_W)
    bi = jnp.broadcast_to(kr - d + (WIN_H - 1), (WIN_H, GRID_W, WIN_H, GRID_W))
    bj = jnp.clip(jnp.broadcast_to(kc - c + (WIN_W - 1), bi.shape), 0, 2 * WIN_W - 2)
    tab = rel_bias.astype(F32)[:, bi, bj]
    tab = jnp.where(inside[None], tab, NEG_BIG)
    return tab.transpose(1, 0, 2, 3, 4).reshape(WIN_H, B_HEADS, GRID_W, WIN_H * GRID_W)


def _pool_kernel(x_ref, w_ref, sc_ref, o_ref, pad_ref, *, seq_len, chunk):
    g = pl.program_id(1)
    zeros = jnp.zeros((POOL_HALO, C_GW), F32)
    pad_ref[0:POOL_HALO, :] = zeros
    pad_ref[POOL_HALO + seq_len:2 * POOL_HALO + seq_len, :] = zeros
    pad_ref[POOL_HALO:POOL_HALO + seq_len, :] = x_ref[...]
    for gi, w in enumerate(POOL_SIZES):
        @pl.when(g == gi)
        def _(w=w):
            for c0 in range(0, seq_len, chunk):
                acc = None
                for k in range(-(w // 2), w - w // 2):
                    piece = pad_ref[POOL_HALO + c0 + k:POOL_HALO + c0 + k + chunk, :]
                    acc = piece if acc is None else acc + piece
                t = c0 + lax.broadcasted_iota(jnp.int32, (chunk, C_GW), 0)
                cnt = jnp.minimum(t + (w - w // 2), seq_len) - jnp.maximum(t - w // 2, 0)
                d = (acc / cnt.astype(F32) - x_ref[c0:c0 + chunk, :]).astype(BF16)
                y = jnp.dot(d, w_ref[0], preferred_element_type=F32) * sc_ref[...]
                o_ref[c0:c0 + chunk, :] = y.astype(o_ref.dtype)


def pool_mix(p_pool, pool_w, pool_scale, n_seq, seq_len, first_block):
    chunk = min(seq_len, 512)
    return pl.pallas_call(
        functools.partial(_pool_kernel, seq_len=seq_len, chunk=chunk),
        grid=(n_seq, len(POOL_SIZES)),
        in_specs=[pl.BlockSpec((seq_len, C_GW), lambda b, g: (first_block + b, g)),
                  pl.BlockSpec((1, C_GW, C_GW), lambda b, g: (g, 0, 0)),
                  pl.BlockSpec((1, C_GW), lambda b, g: (0, g))],
        out_specs=pl.BlockSpec((seq_len, C_GW), lambda b, g: (b, g)),
        out_shape=jax.ShapeDtypeStruct((n_seq * seq_len, C_WIDTH), BF16),
        scratch_shapes=[pltpu.VMEM((seq_len + 2 * POOL_HALO, C_GW), F32)],
        compiler_params=_cparams(2, 48),
        name="pool_mix",
    )(p_pool, pool_w, pool_scale.reshape(1, C_WIDTH))


_REGION = [(r1, r2) for r1 in range(P_TOPK) for r2 in range(P_TOPK) if (r1 + 1) * (r2 + 1) <= P_TOPK]


def _dominates(a, b):
    return a != b and a[0] <= b[0] and a[1] <= b[1]


def _router_kernel(h_ref, wq_ref, k1_ref, k2_ref, cnt_ref, a_ref, rk_ref, b_ref,
                   s_scr, v1_scr, i1_scr, v2_scr, i2_scr, cr_scr, e1_scr, e2_scr):
    tm = h_ref.shape[1]
    half = P_HEADS * P_DK // 2
    q_t = jnp.dot(wq_ref[...], h_ref[...], preferred_element_type=F32)
    key = lax.broadcasted_iota(jnp.int32, (P_NKEYS, P_HEADS, tm), 0)

    def top_sorted(k_ref, q_half, v_scr, i_scr):
        s_scr[...] = jnp.dot(k_ref[...], q_half.astype(BF16),
                             preferred_element_type=F32).reshape(P_NKEYS, P_HEADS, tm)

        def body(r, carry):
            s = s_scr[...]
            m = jnp.max(s, axis=0)
            idx = jnp.min(jnp.where(s == m[None], key, P_NKEYS), axis=0)
            s_scr[...] = jnp.where(key == idx[None], -jnp.inf, s)
            v_scr[r] = m
            i_scr[r] = idx
            return carry

        lax.fori_loop(0, P_TOPK, body, 0)

    top_sorted(k1_ref, q_t[:half], v1_scr, i1_scr)
    top_sorted(k2_ref, q_t[half:], v2_scr, i2_scr)

    v1 = [v1_scr[r] for r in range(P_TOPK)]
    v2 = [v2_scr[r] for r in range(P_TOPK)]
    sums = {c: v1[c[0]] + v2[c[1]] for c in _REGION}
    beaten = {c: float(sum(_dominates(o, c) for o in _REGION)) for c in _REGION}
    for ci, c in enumerate(_REGION):
        for o in _REGION[:ci]:
            if _dominates(o, c):
                continue
            o_first = (sums[o] >= sums[c]).astype(F32)
            beaten[c] = beaten[c] + o_first
            beaten[o] = beaten[o] + (1.0 - o_first)
    e1 = [jnp.exp(v1[r] - v1[0]) for r in range(P_TOPK)]
    e2 = [jnp.exp(v2[r] - v2[0]) for r in range(P_TOPK)]
    z = jnp.zeros_like(v1[0])
    counts = [jnp.zeros_like(v1[0]) for _ in range(P_TOPK)]
    for c in _REGION:
        chosen = beaten[c] < float(P_TOPK)
        z = z + jnp.where(chosen, e1[c[0]] * e2[c[1]], 0.0)
        counts[c[0]] = counts[c[0]] + jnp.where(chosen, 1.0, 0.0)
    inv_z = 1.0 / z
    for r in range(P_TOPK):
        cr_scr[r] = counts[r]
        e1_scr[r] = e1[r]
        e2_scr[r] = e2[r] * inv_z

    key2 = lax.broadcasted_iota(jnp.int32, (P_NKEYS, tm), 0)
    for h in range(P_HEADS):
        def expand(r, carry):
            cnt, a, rk, b = carry
            hit1 = key2 == i1_scr[r, h:h + 1, :]
            hit2 = key2 == i2_scr[r, h:h + 1, :]
            cnt = jnp.where(hit1, cr_scr[r, h:h + 1, :], cnt)
            a = jnp.where(hit1, e1_scr[r, h:h + 1, :], a)
            rk = jnp.where(hit2, r.astype(F32), rk)
            b = jnp.where(hit2, e2_scr[r, h:h + 1, :], b)
            return cnt, a, rk, b

        zero = jnp.zeros((P_NKEYS, tm), F32)
        cnt, a, rk, b = lax.fori_loop(0, P_TOPK, expand,
                                      (zero, zero, jnp.full((P_NKEYS, tm), float(P_TOPK), F32), zero))
        cnt_ref[h] = cnt
        a_ref[h] = a
        rk_ref[h] = rk
        b_ref[h] = b


def peer_router(h_t, wq_t, k1_packed, k2_packed, tm=256):
    d, n = h_t.shape
    qw = wq_t.shape[0]
    kp = k1_packed.shape[0]
    tab = jax.ShapeDtypeStruct((P_HEADS, P_NKEYS, n), F32)
    tab_spec = pl.BlockSpec((P_HEADS, P_NKEYS, tm), lambda i: (0, 0, i))
    small = lambda dt: pltpu.VMEM((P_TOPK, P_HEADS, tm), dt)
    return pl.pallas_call(
        _router_kernel,
        grid=(n // tm,),
        in_specs=[pl.BlockSpec((d, tm), lambda i: (0, i)),
                  pl.BlockSpec((qw, d), lambda i: (0, 0), pipeline_mode=pl.Buffered(1)),
                  pl.BlockSpec((kp, kp), lambda i: (0, 0), pipeline_mode=pl.Buffered(1)),
                  pl.BlockSpec((kp, kp), lambda i: (0, 0), pipeline_mode=pl.Buffered(1))],
        out_specs=[tab_spec] * 4,
        out_shape=[tab] * 4,
        scratch_shapes=[pltpu.VMEM((P_NKEYS, P_HEADS, tm), F32),
                        small(F32), small(jnp.int32), small(F32), small(jnp.int32),
                        small(F32), small(F32), small(F32)],
        compiler_params=_cparams(1, 52),
        name="peer_router",
    )(h_t, wq_t, k1_packed, k2_packed)


def _gelu_exact(x):
    return 0.5 * x * (1.0 + lax.erf(x * float(np.sqrt(0.5))))


def _peer_kernel(h_ref, u_ref, vt_ref, cnt_ref, a_ref, rk_ref, b_ref, o_ref, *, rows_per_tile):
    e = pl.program_id(1)
    tm = h_ref.shape[1]
    act = _gelu_exact(jnp.dot(u_ref[...], h_ref[...], preferred_element_type=F32))
    gate_rows = []
    for ii in range(rows_per_tile):
        i = e * rows_per_tile + ii
        g = jnp.zeros((P_NKEYS, tm), F32)
        for h in range(P_HEADS):
            cnt = cnt_ref[h, pl.ds(i, 1), :]
            a = a_ref[h, pl.ds(i, 1), :]
            g = g + jnp.where(rk_ref[h] < cnt, b_ref[h] * a, 0.0)
        gate_rows.append(g)
    gates = jnp.concatenate(gate_rows, axis=0) if rows_per_tile > 1 else gate_rows[0]
    contrib = jnp.dot(vt_ref[...], (gates * act).astype(BF16), preferred_element_type=F32)

    @pl.when(e == 0)
    def _():
        o_ref[...] = contrib

    @pl.when(e > 0)
    def _():
        o_ref[...] += contrib


def peer_mixture(h_t, u, v_t, tables, tm=512, te=256):
    d, n = h_t.shape
    n_exp = u.shape[0]
    tab_spec = pl.BlockSpec((P_HEADS, P_NKEYS, tm), lambda i, e: (0, 0, i), pipeline_mode=pl.Buffered(1))
    return pl.pallas_call(
        functools.partial(_peer_kernel, rows_per_tile=te // P_NKEYS),
        grid=(n // tm, n_exp // te),
        in_specs=[pl.BlockSpec((d, tm), lambda i, e: (0, i), pipeline_mode=pl.Buffered(1)),
                  pl.BlockSpec((te, d), lambda i, e: (e, 0)),
                  pl.BlockSpec((d, te), lambda i, e: (0, e))] + [tab_spec] * 4,
        out_specs=pl.BlockSpec((d, tm), lambda i, e: (0, i)),
        out_shape=jax.ShapeDtypeStruct((d, n), F32),
        compiler_params=_cparams(2, 56),
        name="peer_mixture",
    )(h_t, u, v_t, *tables)


def _residual_t_kernel(x_ref, y_ref, g_ref, o_ref):
    o_ref[...] = x_ref[...] + g_ref[0] * y_ref[...].T


def residual_from_transposed(x, y_t, mod3, gate_chunk, n_rows, seq, n_batch):
    d = x.shape[1]
    tm = 256
    seg = functools.partial(_segment, tile_rows=tm, seq=seq, n_batch=n_batch)
    return pl.pallas_call(
        _residual_t_kernel,
        grid=(n_rows // tm,),
        in_specs=[pl.BlockSpec((tm, d), lambda i: (i, 0)),
                  pl.BlockSpec((d, tm), lambda i: (0, i)),
                  pl.BlockSpec((1, 1, d), lambda i: (seg(i), 0, gate_chunk))],
        out_specs=pl.BlockSpec((tm, d), lambda i: (i, 0)),
        out_shape=jax.ShapeDtypeStruct((n_rows, d), F32),
        compiler_params=_cparams(1, 48),
        name="residual_from_transposed",
    )(x, y_t, mod3)


def _rope_tables(seq, n_batch, n_ctx_rows):
    t = jnp.arange(seq)
    row = (t // GRID_W).astype(F32)
    col = (t % GRID_W).astype(F32)
    n_freq = A_ROPE // 4
    freqs = ROPE_THETA ** (-jnp.arange(n_freq, dtype=F32) / n_freq)
    ang = jnp.concatenate([row[:, None] * freqs, col[:, None] * freqs], axis=-1)
    cos, sin = jnp.cos(ang), jnp.sin(ang)
    zero = jnp.zeros_like(cos)
    pad = jnp.zeros((seq, LANES - A_ROPE), F32)
    c = jnp.concatenate([cos, cos, pad], axis=1)
    s1 = jnp.concatenate([-sin, zero, pad], axis=1)
    s2 = jnp.concatenate([zero, sin, pad], axis=1)
    ident = jnp.concatenate([jnp.ones((n_ctx_rows, A_ROPE), F32), jnp.zeros((n_ctx_rows, LANES - A_ROPE), F32)], 1)
    none = jnp.zeros((n_ctx_rows, LANES), F32)
    tile = lambda a, ctx: jnp.concatenate([jnp.tile(a, (n_batch, 1)), ctx], axis=0)
    return tile(c, ident), tile(s1, none), tile(s2, none)


def _pack_keys(k):
    h, nk, dk = k.shape
    eye = jnp.eye(h, dtype=k.dtype)
    blk = k.transpose(1, 0, 2)[:, :, None, :] * eye[None, :, :, None]
    return blk.reshape(nk * h, h * dk).astype(BF16)


def _prepare_layer(w_in, w_uq, w_ukv, w_br_a, w_br_b, w_br_c, w_out, peer_wq, peer_k1, peer_k2, peer_u, peer_v):
    d = w_in.shape[0]
    o_kr = A_QLORA + A_KVLORA
    o_q = o_kr + A_ROPE
    o_pool = o_q + 3 * B_WIDTH
    o_gate = o_pool + C_WIDTH
    w_small = jnp.concatenate([w_in[:, :o_q], jnp.zeros((d, LANES - A_ROPE), w_in.dtype)], axis=1)
    w_uq_pad = jnp.pad(w_uq.reshape(A_QLORA, A_HEADS, A_NOPE + A_ROPE),
                       ((0, 0), (0, 0), (0, A_HEAD_PAD - A_NOPE - A_ROPE))).reshape(A_QLORA, A_HEADS * A_HEAD_PAD)
    w_ukv_perm = w_ukv.reshape(A_KVLORA, A_HEADS, 2, A_NOPE).transpose(0, 2, 1, 3).reshape(A_KVLORA, -1)
    wq_t = peer_wq.reshape(d, P_HEADS, 2, P_DK // 2).transpose(2, 1, 3, 0).reshape(P_HEADS * P_DK, d)
    return dict(
        w_small=w_small.astype(BF16),
        w_qkv=w_in[:, o_q:o_pool].astype(BF16),
        w_pool=w_in[:, o_pool:o_gate].astype(BF16),
        w_gate=w_in[:, o_gate:].astype(BF16),
        w_uq=w_uq_pad.astype(BF16),
        w_ukv=w_ukv_perm.astype(BF16),
        w_br_a=w_br_a.astype(BF16), w_br_b=w_br_b.astype(BF16), w_br_c=w_br_c.astype(BF16),
        w_out=w_out.astype(BF16),
        wq_t=wq_t.astype(BF16),
        k1=_pack_keys(peer_k1), k2=_pack_keys(peer_k2),
        u=peer_u.astype(BF16), v_t=peer_v.T.astype(BF16),
    )


def kernel(x, c, ctx, c_ctx, w_ada, b_ada, norm1_g, norm2_g, w_in, q_norm_g, kv_norm_g, w_uq, w_ukv, na_rel_bias, pool_w, pool_scale, w_br_a, w_br_b, w_br_c, w_out, peer_wq, peer_k1, peer_k2, peer_u, peer_v, final_g):
    n_batch, seq, d = x.shape
    ctx_len = ctx.shape[1]
    depth = w_ada.shape[0]
    n_lat = n_batch * seq
    n_ctx = n_batch * ctx_len
    nt = n_lat + n_ctx
    ctx_blk0 = n_lat // ctx_len

    stream = jnp.concatenate([x.reshape(n_lat, d), ctx.reshape(n_ctx, d)], axis=0)
    mod_rows = 8
    cvec = jnp.concatenate([c, c_ctx[None], jnp.zeros((mod_rows - n_batch - 1, d), c.dtype)], axis=0)
    mod = ada_modulation(cvec, w_ada, b_ada)
    rope_c, rope_s1, rope_s2 = _rope_tables(seq, n_batch, n_ctx)

    for l in range(depth):
        ctx_out = l < depth - 1
        n_act = nt if ctx_out else n_lat
        wl = _prepare_layer(w_in[l], w_uq[l], w_ukv[l], w_br_a[l], w_br_b[l], w_br_c[l], w_out[l],
                            peer_wq[l], peer_k1[l], peer_k2[l], peer_u[l], peer_v[l])
        mod3 = mod[l].reshape(mod_rows, 1, 6 * d)

        h = ln_modulate(stream, norm1_g[l], mod3, 0, nt, seq, n_batch)
        p_small = matmul(h, wl["w_small"], F32)
        qkv = matmul(h, wl["w_qkv"], BF16)
        p_pool = matmul(h, wl["w_pool"], F32)
        gates = matmul(h[:n_act], wl["w_gate"], F32, act="sigmoid")

        q = mla_q(p_small, q_norm_g[l], wl["w_uq"], rope_c, rope_s1, rope_s2)
        k, v = mla_kv(p_small, kv_norm_g[l], wl["w_ukv"], rope_c, rope_s1, rope_s2)
        a_br = mla_latent_attention(q, k, v, n_batch, seq, ctx_len)
        b_br = na_latent_attention(qkv, na_bias_table(na_rel_bias[l]), n_batch, seq, ctx_len)
        pw = pool_w[l].astype(BF16)
        c_br = pool_mix(p_pool, pw, pool_scale[l], n_batch, seq, 0)
        if ctx_out:
            a_ctx = context_attention(q, k, v, n_batch, seq, ctx_len, A_HEADS, A_HEAD_PAD, A_VDIM, 0, 0, 0,
                                      float((A_NOPE + A_ROPE) ** -0.5))
            b_ctx = context_attention(qkv, qkv, qkv, n_batch, seq, ctx_len, B_HEADS, B_DH, B_DH,
                                      0, B_HEADS, 2 * B_HEADS, float(B_DH ** -0.5))
            c_ctx_br = pool_mix(p_pool, pw, pool_scale[l], n_batch, ctx_len, ctx_blk0)
            a_br = jnp.concatenate([a_br, a_ctx], axis=0)
            b_br = jnp.concatenate([b_br, b_ctx], axis=0)
            c_br = jnp.concatenate([c_br, c_ctx_br], axis=0)

        merged = merge_branches(a_br, b_br, c_br, wl["w_br_a"], wl["w_br_b"], wl["w_br_c"], gates)
        stream = matmul_gated_residual(merged, wl["w_out"], stream, mod3, 2, seq, n_batch)

        h2_t = ln_modulate(stream, norm2_g[l], mod3, 3, n_act, seq, n_batch, transpose=True)
        tables = peer_router(h2_t, wl["wq_t"], wl["k1"], wl["k2"])
        y_t = peer_mixture(h2_t, wl["u"], wl["v_t"], tables)
        stream = residual_from_transposed(stream, y_t, mod3, 5, n_act, seq, n_batch)

    return final_norm(stream, final_g, n_lat).reshape(n_batch, seq, d)
```

```python
import functools

import numpy as np
import jax
import jax.numpy as jnp
from jax import lax
from jax.experimental import pallas as pl
from jax.experimental.pallas import tpu as pltpu

F32 = jnp.float32
BF16 = jnp.bfloat16

GRID_W = 64
EPS = 1e-6
ROPE_THETA = 10000.0

A_HEADS = 16
A_NOPE = 128
A_ROPE = 64
A_VDIM = 128
A_QLORA = 768
A_KVLORA = 256
A_HEAD_PAD = 256

B_HEADS = 8
B_DH = 128
B_WIDTH = B_HEADS * B_DH
WIN_H = 8
WIN_W = 16

C_WIDTH = 1024
POOL_SIZES = (2, 4, 8, 16)
C_GW = C_WIDTH // len(POOL_SIZES)
POOL_HALO = 8

P_HEADS = 8
P_NKEYS = 128
P_DK = 256
P_TOPK = 16

LANES = 128
SMALL_W = A_QLORA + A_KVLORA + LANES

NEG_BIG = -1e30
MIB = 1024 * 1024


def _cparams(n_axes, vmem_mib):
    return pltpu.CompilerParams(dimension_semantics=("arbitrary",) * n_axes,
                                vmem_limit_bytes=vmem_mib * MIB)


def _segment(row_tile, tile_rows, seq, n_batch):
    return jnp.minimum((row_tile * tile_rows) // seq, n_batch)


def _ada_kernel(c_ref, w_ref, b_ref, o_ref):
    c = c_ref[...]
    a = (c * jax.nn.sigmoid(c)).astype(BF16)
    o_ref[0] = jnp.dot(a, w_ref[0].astype(BF16), preferred_element_type=F32) + b_ref[0]


def ada_modulation(cvec, w_ada, b_ada):
    n_layers, d, n_out = w_ada.shape
    rows = cvec.shape[0]
    tn = 512
    return pl.pallas_call(
        _ada_kernel,
        grid=(n_layers, n_out // tn),
        in_specs=[pl.BlockSpec((rows, d), lambda l, j: (0, 0)),
                  pl.BlockSpec((1, d, tn), lambda l, j: (l, 0, j)),
                  pl.BlockSpec((1, 1, tn), lambda l, j: (l, 0, j))],
        out_specs=pl.BlockSpec((1, rows, tn), lambda l, j: (l, 0, j)),
        out_shape=jax.ShapeDtypeStruct((n_layers, rows, n_out), F32),
        compiler_params=_cparams(2, 40),
        name="ada_modulation",
    )(cvec, w_ada, b_ada.reshape(n_layers, 1, n_out))


def _rms(x, g):
    return x * lax.rsqrt(jnp.mean(x * x, axis=-1, keepdims=True) + EPS) * g


def _ln_mod_kernel(x_ref, g_ref, sc_ref, sh_ref, o_ref, *, transpose):
    h = _rms(x_ref[...], g_ref[...]) * (1.0 + sc_ref[0]) + sh_ref[0]
    if transpose:
        o_ref[...] = h.T.astype(o_ref.dtype)
    else:
        o_ref[...] = h.astype(o_ref.dtype)


def ln_modulate(x, g, mod3, shift_chunk, n_rows, seq, n_batch, transpose=False):
    d = x.shape[1]
    tm = 256
    seg = functools.partial(_segment, tile_rows=tm, seq=seq, n_batch=n_batch)
    if transpose:
        out_spec = pl.BlockSpec((d, tm), lambda i: (0, i))
        out_shape = jax.ShapeDtypeStruct((d, n_rows), BF16)
    else:
        out_spec = pl.BlockSpec((tm, d), lambda i: (i, 0))
        out_shape = jax.ShapeDtypeStruct((n_rows, d), BF16)
    return pl.pallas_call(
        functools.partial(_ln_mod_kernel, transpose=transpose),
        grid=(n_rows // tm,),
        in_specs=[pl.BlockSpec((tm, d), lambda i: (i, 0)),
                  pl.BlockSpec((1, d), lambda i: (0, 0)),
                  pl.BlockSpec((1, 1, d), lambda i: (seg(i), 0, shift_chunk + 1)),
                  pl.BlockSpec((1, 1, d), lambda i: (seg(i), 0, shift_chunk))],
        out_specs=out_spec,
        out_shape=out_shape,
        compiler_params=_cparams(1, 48),
        name="ln_modulate_t" if transpose else "ln_modulate",
    )(x, g.reshape(1, d), mod3, mod3)


def _final_norm_kernel(x_ref, g_ref, o_ref):
    o_ref[...] = _rms(x_ref[...], g_ref[...])


def final_norm(x, g, n_rows):
    d = x.shape[1]
    tm = 256
    return pl.pallas_call(
        _final_norm_kernel,
        grid=(n_rows // tm,),
        in_specs=[pl.BlockSpec((tm, d), lambda i: (i, 0)),
                  pl.BlockSpec((1, d), lambda i: (0, 0))],
        out_specs=pl.BlockSpec((tm, d), lambda i: (i, 0)),
        out_shape=jax.ShapeDtypeStruct((n_rows, d), F32),
        compiler_params=_cparams(1, 48),
        name="final_norm",
    )(x, g.reshape(1, d))


def _mm_kernel(a_ref, b_ref, o_ref, *, act):
    acc = jnp.dot(a_ref[...], b_ref[...], preferred_element_type=F32)
    if act == "sigmoid":
        acc = jax.nn.sigmoid(acc)
    o_ref[...] = acc.astype(o_ref.dtype)


def matmul(a, b, out_dtype, act=None, n_rows=None, tn=1024, tm=512):
    m, k = a.shape
    m = n_rows or m
    n = b.shape[1]
    if n % tn:
        tn = n
    return pl.pallas_call(
        functools.partial(_mm_kernel, act=act),
        grid=(n // tn, m // tm),
        in_specs=[pl.BlockSpec((tm, k), lambda j, i: (i, 0)),
                  pl.BlockSpec((k, tn), lambda j, i: (0, j))],
        out_specs=pl.BlockSpec((tm, tn), lambda j, i: (i, j)),
        out_shape=jax.ShapeDtypeStruct((m, n), out_dtype),
        compiler_params=_cparams(2, 52),
        name="matmul_" + (act or "plain"),
    )(a, b)


def _mm_residual_kernel(a_ref, b_ref, x_ref, g_ref, o_ref):
    acc = jnp.dot(a_ref[...], b_ref[...], preferred_element_type=F32)
    o_ref[...] = x_ref[...] + g_ref[0] * acc


def matmul_gated_residual(a, b, x, mod3, gate_chunk, seq, n_batch, tn=1024, tm=512):
    m, k = a.shape
    n = b.shape[1]
    seg = functools.partial(_segment, tile_rows=tm, seq=seq, n_batch=n_batch)
    nj = n // tn
    return pl.pallas_call(
        _mm_residual_kernel,
        grid=(nj, m // tm),
        in_specs=[pl.BlockSpec((tm, k), lambda j, i: (i, 0)),
                  pl.BlockSpec((k, tn), lambda j, i: (0, j)),
                  pl.BlockSpec((tm, tn), lambda j, i: (i, j)),
                  pl.BlockSpec((1, 1, tn), lambda j, i: (seg(i), 0, gate_chunk * nj + j))],
        out_specs=pl.BlockSpec((tm, tn), lambda j, i: (i, j)),
        out_shape=jax.ShapeDtypeStruct((m, n), F32),
        compiler_params=_cparams(2, 52),
        name="matmul_gated_residual",
    )(a, b, x, mod3)


def _merge_kernel(a_ref, b_ref, c_ref, wa_ref, wb_ref, wc_ref, ga_ref, gb_ref, gc_ref, o_ref):
    m = ga_ref[...] * jnp.dot(a_ref[...], wa_ref[...], preferred_element_type=F32)
    m = m + gb_ref[...] * jnp.dot(b_ref[...], wb_ref[...], preferred_element_type=F32)
    m = m + gc_ref[...] * jnp.dot(c_ref[...], wc_ref[...], preferred_element_type=F32)
    o_ref[...] = m.astype(o_ref.dtype)


def merge_branches(a, b, c, wa, wb, wc, gates, tn=1024, tm=512):
    m = a.shape[0]
    n = wa.shape[1]
    nj = n // tn
    row = lambda j, i: (i, 0)
    col = lambda j, i: (0, j)
    return pl.pallas_call(
        _merge_kernel,
        grid=(nj, m // tm),
        in_specs=[pl.BlockSpec((tm, a.shape[1]), row),
                  pl.BlockSpec((tm, b.shape[1]), row),
                  pl.BlockSpec((tm, c.shape[1]), row),
                  pl.BlockSpec((wa.shape[0], tn), col),
                  pl.BlockSpec((wb.shape[0], tn), col),
                  pl.BlockSpec((wc.shape[0], tn), col),
                  pl.BlockSpec((tm, tn), lambda j, i: (i, j)),
                  pl.BlockSpec((tm, tn), lambda j, i: (i, nj + j)),
                  pl.BlockSpec((tm, tn), lambda j, i: (i, 2 * nj + j))],
        out_specs=pl.BlockSpec((tm, tn), lambda j, i: (i, j)),
        out_shape=jax.ShapeDtypeStruct((m, n), BF16),
        compiler_params=_cparams(2, 52),
        name="merge_branches",
    )(a, b, c, wa, wb, wc, gates, gates, gates)


def _rotate(v, c_ref, s1_ref, s2_ref):
    return (v * c_ref[...] + pltpu.roll(v, LANES - A_ROPE // 2, 1) * s1_ref[...]
            + pltpu.roll(v, A_ROPE // 2, 1) * s2_ref[...])


def _mla_q_kernel(cq_ref, g_ref, w_ref, c_ref, s1_ref, s2_ref, o_ref):
    y = _rms(cq_ref[...], g_ref[...]).astype(BF16)
    acc = jnp.dot(y, w_ref[...], preferred_element_type=F32)
    for h in range(A_HEADS):
        lo = h * A_HEAD_PAD
        o_ref[:, lo:lo + A_NOPE] = acc[:, lo:lo + A_NOPE].astype(o_ref.dtype)
        rot = _rotate(acc[:, lo + A_NOPE:lo + A_HEAD_PAD], c_ref, s1_ref, s2_ref)
        o_ref[:, lo + A_NOPE:lo + A_HEAD_PAD] = rot.astype(o_ref.dtype)


def mla_q(p_small, g, w_uq_pad, rope_c, rope_s1, rope_s2):
    nt = p_small.shape[0]
    tm = 256
    width = A_HEADS * A_HEAD_PAD
    tab = pl.BlockSpec((tm, LANES), lambda i: (i, 0))
    return pl.pallas_call(
        _mla_q_kernel,
        grid=(nt // tm,),
        in_specs=[pl.BlockSpec((tm, A_QLORA), lambda i: (i, 0)),
                  pl.BlockSpec((1, A_QLORA), lambda i: (0, 0)),
                  pl.BlockSpec((A_QLORA, width), lambda i: (0, 0)),
                  tab, tab, tab],
        out_specs=pl.BlockSpec((tm, width), lambda i: (i, 0)),
        out_shape=jax.ShapeDtypeStruct((nt, width), BF16),
        compiler_params=_cparams(1, 48),
        name="mla_q",
    )(p_small, g.reshape(1, A_QLORA), w_uq_pad, rope_c, rope_s1, rope_s2)


def _mla_kv_kernel(ckv_ref, kr_ref, g_ref, w_ref, c_ref, s1_ref, s2_ref, k_ref, v_ref):
    y = _rms(ckv_ref[...], g_ref[...]).astype(BF16)
    acc = jnp.dot(y, w_ref[...], preferred_element_type=F32)
    kr = _rotate(kr_ref[...], c_ref, s1_ref, s2_ref).astype(k_ref.dtype)
    for h in range(A_HEADS):
        lo = h * A_HEAD_PAD
        k_ref[:, lo:lo + A_NOPE] = acc[:, h * A_NOPE:(h + 1) * A_NOPE].astype(k_ref.dtype)
        k_ref[:, lo + A_NOPE:lo + A_HEAD_PAD] = kr
    v_ref[...] = acc[:, A_HEADS * A_NOPE:].astype(v_ref.dtype)


def mla_kv(p_small, g, w_ukv_perm, rope_c, rope_s1, rope_s2):
    nt = p_small.shape[0]
    tm = 256
    kw = A_HEADS * A_HEAD_PAD
    vw = A_HEADS * A_VDIM
    tab = pl.BlockSpec((tm, LANES), lambda i: (i, 0))
    return pl.pallas_call(
        _mla_kv_kernel,
        grid=(nt // tm,),
        in_specs=[pl.BlockSpec((tm, A_KVLORA), lambda i: (i, A_QLORA // A_KVLORA)),
                  pl.BlockSpec((tm, LANES), lambda i: (i, (A_QLORA + A_KVLORA) // LANES)),
                  pl.BlockSpec((1, A_KVLORA), lambda i: (0, 0)),
                  pl.BlockSpec((A_KVLORA, A_HEADS * (A_NOPE + A_VDIM)), lambda i: (0, 0)),
                  tab, tab, tab],
        out_specs=[pl.BlockSpec((tm, kw), lambda i: (i, 0)),
                   pl.BlockSpec((tm, vw), lambda i: (i, 0))],
        out_shape=[jax.ShapeDtypeStruct((nt, kw), BF16),
                   jax.ShapeDtypeStruct((nt, vw), BF16)],
        compiler_params=_cparams(1, 48),
        name="mla_kv",
    )(p_small, p_small, g.reshape(1, A_KVLORA), w_ukv_perm, rope_c, rope_s1, rope_s2)


def _qk(q, k):
    return lax.dot_general(q, k, (((1,), (1,)), ((), ())), preferred_element_type=F32)


LOG2_E = 1.4426950408889634


def _flash_two_sets_kernel(q_ref, kl_ref, vl_ref, kc_ref, vc_ref, o_ref, *, scale, chunk):
    q = q_ref[...]
    c = scale * LOG2_E
    t = _qk(q, kc_ref[...]) * c
    m = jnp.max(t, axis=-1, keepdims=True)
    p = jnp.exp2(t - m)
    den = jnp.sum(p, axis=-1, keepdims=True)
    acc = jnp.dot(p.astype(BF16), vc_ref[...], preferred_element_type=F32)
    for start in range(0, kl_ref.shape[0], chunk):
        t = _qk(q, kl_ref[start:start + chunk, :]) * c
        m_new = jnp.maximum(m, jnp.max(t, axis=-1, keepdims=True))
        alpha = jnp.exp2(m - m_new)
        p = jnp.exp2(t - m_new)
        den = alpha * den + jnp.sum(p, axis=-1, keepdims=True)
        acc = alpha * acc + jnp.dot(p.astype(BF16), vl_ref[start:start + chunk, :], preferred_element_type=F32)
        m = m_new
    o_ref[...] = (acc / den).astype(o_ref.dtype)


def mla_latent_attention(q, k, v, n_batch, seq, ctx_len, tq=512, chunk=512):
    ctx_blk0 = (n_batch * seq) // ctx_len
    nq = seq // tq
    scale = float((A_NOPE + A_ROPE) ** -0.5)
    return pl.pallas_call(
        functools.partial(_flash_two_sets_kernel, scale=scale, chunk=chunk),
        grid=(n_batch, A_HEADS, nq),
        in_specs=[pl.BlockSpec((tq, A_HEAD_PAD), lambda b, h, i: (b * nq + i, h)),
                  pl.BlockSpec((seq, A_HEAD_PAD), lambda b, h, i: (b, h)),
                  pl.BlockSpec((seq, A_VDIM), lambda b, h, i: (b, h)),
                  pl.BlockSpec((ctx_len, A_HEAD_PAD), lambda b, h, i: (ctx_blk0 + b, h)),
                  pl.BlockSpec((ctx_len, A_VDIM), lambda b, h, i: (ctx_blk0 + b, h))],
        out_specs=pl.BlockSpec((tq, A_VDIM), lambda b, h, i: (b * nq + i, h)),
        out_shape=jax.ShapeDtypeStruct((n_batch * seq, A_HEADS * A_VDIM), BF16),
        compiler_params=_cparams(3, 48),
        name="mla_latent_attention",
    )(q, k, v, k, v)


def _attn_one_set_kernel(q_ref, k_ref, v_ref, o_ref, *, scale):
    s = _qk(q_ref[...], k_ref[...]) * scale
    p = jnp.exp(s - jnp.max(s, axis=-1, keepdims=True))
    den = jnp.sum(p, axis=-1, keepdims=True)
    o = jnp.dot(p.astype(BF16), v_ref[...], preferred_element_type=F32)
    o_ref[...] = (o / den).astype(o_ref.dtype)


def context_attention(q, k, v, n_batch, seq, ctx_len, n_heads, dqk, dv, q_col0, k_col0, v_col0, scale):
    ctx_blk0 = (n_batch * seq) // ctx_len
    return pl.pallas_call(
        functools.partial(_attn_one_set_kernel, scale=scale),
        grid=(n_batch, n_heads),
        in_specs=[pl.BlockSpec((ctx_len, dqk), lambda b, h: (ctx_blk0 + b, q_col0 + h)),
                  pl.BlockSpec((ctx_len, dqk), lambda b, h: (ctx_blk0 + b, k_col0 + h)),
                  pl.BlockSpec((ctx_len, dv), lambda b, h: (ctx_blk0 + b, v_col0 + h))],
        out_specs=pl.BlockSpec((ctx_len, dv), lambda b, h: (b, h)),
        out_shape=jax.ShapeDtypeStruct((n_batch * ctx_len, n_heads * dv), BF16),
        compiler_params=_cparams(2, 32),
        name="context_attention",
    )(q, k, v)


def _na_kernel(q_ref, kl_ref, vl_ref, kc_ref, vc_ref, bias_ref, o_ref, *, rows, scale):
    r = pl.program_id(1)
    r0 = jnp.clip(r - WIN_H // 2, 0, rows - WIN_H)
    start = pl.multiple_of(r0 * GRID_W, GRID_W)
    win = WIN_H * GRID_W
    for h in range(B_HEADS):
        cols = slice(h * B_DH, (h + 1) * B_DH)
        q = q_ref[:, cols]
        s_l = _qk(q, kl_ref[pl.ds(start, win), cols]) * scale + bias_ref[0, h]
        s_c = _qk(q, kc_ref[:, cols]) * scale
        m = jnp.maximum(jnp.max(s_l, axis=-1, keepdims=True), jnp.max(s_c, axis=-1, keepdims=True))
        p_l = jnp.exp(s_l - m)
        p_c = jnp.exp(s_c - m)
        den = jnp.sum(p_l, axis=-1, keepdims=True) + jnp.sum(p_c, axis=-1, keepdims=True)
        o = jnp.dot(p_l.astype(BF16), vl_ref[pl.ds(start, win), cols], preferred_element_type=F32)
        o = o + jnp.dot(p_c.astype(BF16), vc_ref[:, cols], preferred_element_type=F32)
        o_ref[:, cols] = (o / den).astype(o_ref.dtype)


def na_latent_attention(qkv, bias, n_batch, seq, ctx_len):
    rows = seq // GRID_W
    ctx_blk0 = (n_batch * seq) // ctx_len
    win = WIN_H * GRID_W

    def bias_idx(b, r):
        return (r - jnp.clip(r - WIN_H // 2, 0, rows - WIN_H), 0, 0, 0)

    once = pl.Buffered(1)
    return pl.pallas_call(
        functools.partial(_na_kernel, rows=rows, scale=float(B_DH ** -0.5)),
        grid=(n_batch, rows),
        in_specs=[pl.BlockSpec((GRID_W, B_WIDTH), lambda b, r: (b * rows + r, 0)),
                  pl.BlockSpec((seq, B_WIDTH), lambda b, r: (b, 1), pipeline_mode=once),
                  pl.BlockSpec((seq, B_WIDTH), lambda b, r: (b, 2), pipeline_mode=once),
                  pl.BlockSpec((ctx_len, B_WIDTH), lambda b, r: (ctx_blk0 + b, 1)),
                  pl.BlockSpec((ctx_len, B_WIDTH), lambda b, r: (ctx_blk0 + b, 2)),
                  pl.BlockSpec((1, B_HEADS, GRID_W, win), bias_idx)],
        out_specs=pl.BlockSpec((GRID_W, B_WIDTH), lambda b, r: (b * rows + r, 0)),
        out_shape=jax.ShapeDtypeStruct((n_batch * seq, B_WIDTH), BF16),
        compiler_params=_cparams(2, 40),
        name="na_latent_attention",
    )(qkv, qkv, qkv, qkv, qkv, bias)


def na_bias_table(rel_bias):
    rel = rel_bias.astype(F32)
    by_row = jnp.stack([rel[:, WIN_H - 1 - d:2 * WIN_H - 1 - d, :] for d in range(WIN_H)], axis=1)
    pad = GRID_W - 1
    by_row = jnp.pad(by_row, ((0, 0), (0, 0), (0, 0), (pad, pad)))
    off = WIN_W - 1 + pad
    tab = jnp.stack([by_row[..., off - c:off - c + GRID_W] for c in range(GRID_W)], axis=3)
    c = jnp.arange(GRID_W)[:, None]
    kc = jnp.arange(GRID_W)[None, :]
    c0 = jnp.clip(c - WIN_W // 2, 0, GRID_W - WIN_W)
    inside = (kc >= c0) & (kc < c0 + WIN_W)
    tab = jnp.where(inside[None, None, None], tab, NEG_BIG)
    return tab.transpose(1, 0, 3, 2, 4).reshape(WIN_H, B_HEADS, GRID_W, WIN_H * GRID_W)


def _pool_kernel(x_ref, w_ref, sc_ref, o_ref, pad_ref, *, seq_len, chunk):
    g = pl.program_id(1)
    zeros = jnp.zeros((POOL_HALO, C_GW), F32)
    pad_ref[0:POOL_HALO, :] = zeros
    pad_ref[POOL_HALO + seq_len:2 * POOL_HALO + seq_len, :] = zeros
    pad_ref[POOL_HALO:POOL_HALO + seq_len, :] = x_ref[...]
    for gi, w in enumerate(POOL_SIZES):
        @pl.when(g == gi)
        def _(w=w):
            for c0 in range(0, seq_len, chunk):
                acc = None
                for k in range(-(w // 2), w - w // 2):
                    piece = pad_ref[POOL_HALO + c0 + k:POOL_HALO + c0 + k + chunk, :]
                    acc = piece if acc is None else acc + piece
                t = c0 + lax.broadcasted_iota(jnp.int32, (chunk, C_GW), 0)
                cnt = jnp.minimum(t + (w - w // 2), seq_len) - jnp.maximum(t - w // 2, 0)
                d = (acc / cnt.astype(F32) - x_ref[c0:c0 + chunk, :]).astype(BF16)
                y = jnp.dot(d, w_ref[0], preferred_element_type=F32) * sc_ref[...]
                o_ref[c0:c0 + chunk, :] = y.astype(o_ref.dtype)


def pool_mix(p_pool, pool_w, pool_scale, n_seq, seq_len, first_block):
    chunk = min(seq_len, 512)
    return pl.pallas_call(
        functools.partial(_pool_kernel, seq_len=seq_len, chunk=chunk),
        grid=(n_seq, len(POOL_SIZES)),
        in_specs=[pl.BlockSpec((seq_len, C_GW), lambda b, g: (first_block + b, g)),
                  pl.BlockSpec((1, C_GW, C_GW), lambda b, g: (g, 0, 0)),
                  pl.BlockSpec((1, C_GW), lambda b, g: (0, g))],
        out_specs=pl.BlockSpec((seq_len, C_GW), lambda b, g: (b, g)),
        out_shape=jax.ShapeDtypeStruct((n_seq * seq_len, C_WIDTH), BF16),
        scratch_shapes=[pltpu.VMEM((seq_len + 2 * POOL_HALO, C_GW), F32)],
        compiler_params=_cparams(2, 48),
        name="pool_mix",
    )(p_pool, pool_w, pool_scale.reshape(1, C_WIDTH))


_REGION = [(r1, r2) for r1 in range(P_TOPK) for r2 in range(P_TOPK) if (r1 + 1) * (r2 + 1) <= P_TOPK]


def _dominates(a, b):
    return a != b and a[0] <= b[0] and a[1] <= b[1]


def _router_kernel(h_ref, wq_ref, k1_ref, k2_ref, cnt_ref, a_ref, rk_ref, b_ref,
                   s_scr, v1_scr, i1_scr, v2_scr, i2_scr, cr_scr, e1_scr, e2_scr):
    tm = h_ref.shape[1]
    half = P_HEADS * P_DK // 2
    q_t = jnp.dot(wq_ref[...], h_ref[...], preferred_element_type=F32)
    key = lax.broadcasted_iota(jnp.int32, (P_NKEYS, P_HEADS, tm), 0)

    def top_sorted(k_ref, q_half, v_scr, i_scr):
        s_scr[...] = jnp.dot(k_ref[...], q_half.astype(BF16),
                             preferred_element_type=F32).reshape(P_NKEYS, P_HEADS, tm)

        def body(r, carry):
            s = s_scr[...]
            m = jnp.max(s, axis=0)
            idx = jnp.min(jnp.where(s == m[None], key, P_NKEYS), axis=0)
            s_scr[...] = jnp.where(key == idx[None], -jnp.inf, s)
            v_scr[r] = m
            i_scr[r] = idx
            return carry

        lax.fori_loop(0, P_TOPK, body, 0)

    top_sorted(k1_ref, q_t[:half], v1_scr, i1_scr)
    top_sorted(k2_ref, q_t[half:], v2_scr, i2_scr)

    v1 = [v1_scr[r] for r in range(P_TOPK)]
    v2 = [v2_scr[r] for r in range(P_TOPK)]
    sums = {c: v1[c[0]] + v2[c[1]] for c in _REGION}
    beaten = {c: float(sum(_dominates(o, c) for o in _REGION)) for c in _REGION}
    for ci, c in enumerate(_REGION):
        for o in _REGION[:ci]:
            if _dominates(o, c):
                continue
            o_first = (sums[o] >= sums[c]).astype(F32)
            beaten[c] = beaten[c] + o_first
            beaten[o] = beaten[o] + (1.0 - o_first)
    e1 = [jnp.exp(v1[r] - v1[0]) for r in range(P_TOPK)]
    e2 = [jnp.exp(v2[r] - v2[0]) for r in range(P_TOPK)]
    z = jnp.zeros_like(v1[0])
    counts = [jnp.zeros_like(v1[0]) for _ in range(P_TOPK)]
    for c in _REGION:
        chosen = beaten[c] < float(P_TOPK)
        z = z + jnp.where(chosen, e1[c[0]] * e2[c[1]], 0.0)
        counts[c[0]] = counts[c[0]] + jnp.where(chosen, 1.0, 0.0)
    inv_z = 1.0 / z
    for r in range(P_TOPK):
        cr_scr[r] = counts[r]
        e1_scr[r] = e1[r]
        e2_scr[r] = e2[r] * inv_z

    key2 = lax.broadcasted_iota(jnp.int32, (P_NKEYS, LANES), 0)

    def expand(n, carry, h):
        t0 = pl.multiple_of(n * LANES, LANES)
        row = lambda ref, r: ref[r, h:h + 1, pl.ds(t0, LANES)]
        cnt = jnp.zeros((P_NKEYS, LANES), F32)
        a = cnt
        for r in range(P_TOPK):
            hit = key2 == row(i1_scr, r)
            cnt = jnp.where(hit, row(cr_scr, r), cnt)
            a = jnp.where(hit, row(e1_scr, r), a)
        cnt_ref[h, :, pl.ds(t0, LANES)] = cnt
        a_ref[h, :, pl.ds(t0, LANES)] = a
        rk = jnp.full((P_NKEYS, LANES), float(P_TOPK), F32)
        b = jnp.zeros((P_NKEYS, LANES), F32)
        for r in range(P_TOPK):
            hit = key2 == row(i2_scr, r)
            rk = jnp.where(hit, float(r), rk)
            b = jnp.where(hit, row(e2_scr, r), b)
        rk_ref[h, :, pl.ds(t0, LANES)] = rk
        b_ref[h, :, pl.ds(t0, LANES)] = b
        return carry

    for h in range(P_HEADS):
        lax.fori_loop(0, tm // LANES, functools.partial(expand, h=h), 0)


def peer_router(h_t, wq_t, k1_packed, k2_packed, tm=256):
    d, n = h_t.shape
    qw = wq_t.shape[0]
    kp = k1_packed.shape[0]
    tab = jax.ShapeDtypeStruct((P_HEADS, P_NKEYS, n), F32)
    tab_spec = pl.BlockSpec((P_HEADS, P_NKEYS, tm), lambda i: (0, 0, i))
    small = lambda dt: pltpu.VMEM((P_TOPK, P_HEADS, tm), dt)
    return pl.pallas_call(
        _router_kernel,
        grid=(n // tm,),
        in_specs=[pl.BlockSpec((d, tm), lambda i: (0, i)),
                  pl.BlockSpec((qw, d), lambda i: (0, 0), pipeline_mode=pl.Buffered(1)),
                  pl.BlockSpec((kp, kp), lambda i: (0, 0), pipeline_mode=pl.Buffered(1)),
                  pl.BlockSpec((kp, kp), lambda i: (0, 0), pipeline_mode=pl.Buffered(1))],
        out_specs=[tab_spec] * 4,
        out_shape=[tab] * 4,
        scratch_shapes=[pltpu.VMEM((P_NKEYS, P_HEADS, tm), F32),
                        small(F32), small(jnp.int32), small(F32), small(jnp.int32),
                        small(F32), small(F32), small(F32)],
        compiler_params=_cparams(1, 52),
        name="peer_router",
    )(h_t, wq_t, k1_packed, k2_packed)


def _gelu_exact(x):
    return 0.5 * x * (1.0 + lax.erf(x * float(np.sqrt(0.5))))


def _peer_kernel(h_ref, u_ref, vt_ref, cnt_ref, a_ref, rk_ref, b_ref, o_ref, ga_scr):
    e = pl.program_id(1)
    te, tm = ga_scr.shape

    def step(first):
        pre = jnp.dot(u_ref[...], h_ref[...], preferred_element_type=F32)
        for r0 in range(0, te, P_NKEYS):
            i = e * (te // P_NKEYS) + r0 // P_NKEYS
            cnt_rows = [cnt_ref[h, pl.ds(i, 1), :] for h in range(P_HEADS)]
            a_rows = [a_ref[h, pl.ds(i, 1), :] for h in range(P_HEADS)]
            for t0 in range(0, tm, LANES):
                g = None
                for h in range(P_HEADS):
                    cnt = cnt_rows[h][:, t0:t0 + LANES]
                    a = a_rows[h][:, t0:t0 + LANES]
                    term = jnp.where(rk_ref[h, :, t0:t0 + LANES] < cnt, b_ref[h, :, t0:t0 + LANES] * a, 0.0)
                    g = term if g is None else g + term
                act = _gelu_exact(pre[r0:r0 + P_NKEYS, t0:t0 + LANES])
                ga_scr[r0:r0 + P_NKEYS, t0:t0 + LANES] = (g * act).astype(BF16)
        contrib = jnp.dot(vt_ref[...], ga_scr[...], preferred_element_type=F32)
        if first:
            o_ref[...] = contrib
        else:
            o_ref[...] += contrib

    pl.when(e == 0)(functools.partial(step, True))
    pl.when(e > 0)(functools.partial(step, False))


def peer_mixture(h_t, u, v_t, tables, tm=512, te=512):
    d, n = h_t.shape
    tab_spec = pl.BlockSpec((P_HEADS, P_NKEYS, tm), lambda i, e: (0, 0, i), pipeline_mode=pl.Buffered(1))
    return pl.pallas_call(
        _peer_kernel,
        grid=(n // tm, u.shape[0] // te),
        in_specs=[pl.BlockSpec((d, tm), lambda i, e: (0, i), pipeline_mode=pl.Buffered(1)),
                  pl.BlockSpec((te, d), lambda i, e: (e, 0)),
                  pl.BlockSpec((d, te), lambda i, e: (0, e))] + [tab_spec] * 4,
        out_specs=pl.BlockSpec((d, tm), lambda i, e: (0, i)),
        out_shape=jax.ShapeDtypeStruct((d, n), F32),
        scratch_shapes=[pltpu.VMEM((te, tm), BF16)],
        compiler_params=_cparams(2, 56),
        name="peer_mixture",
    )(h_t, u, v_t, *tables)


def _residual_t_kernel(x_ref, y_ref, g_ref, o_ref):
    o_ref[...] = x_ref[...] + g_ref[0] * y_ref[...].T


def residual_from_transposed(x, y_t, mod3, gate_chunk, n_rows, seq, n_batch):
    d = x.shape[1]
    tm = 256
    seg = functools.partial(_segment, tile_rows=tm, seq=seq, n_batch=n_batch)
    return pl.pallas_call(
        _residual_t_kernel,
        grid=(n_rows // tm,),
        in_specs=[pl.BlockSpec((tm, d), lambda i: (i, 0)),
                  pl.BlockSpec((d, tm), lambda i: (0, i)),
                  pl.BlockSpec((1, 1, d), lambda i: (seg(i), 0, gate_chunk))],
        out_specs=pl.BlockSpec((tm, d), lambda i: (i, 0)),
        out_shape=jax.ShapeDtypeStruct((n_rows, d), F32),
        compiler_params=_cparams(1, 48),
        name="residual_from_transposed",
    )(x, y_t, mod3)


def _rope_tables(seq, n_batch, n_ctx_rows):
    t = jnp.arange(seq)
    row = (t // GRID_W).astype(F32)
    col = (t % GRID_W).astype(F32)
    n_freq = A_ROPE // 4
    freqs = ROPE_THETA ** (-jnp.arange(n_freq, dtype=F32) / n_freq)
    ang = jnp.concatenate([row[:, None] * freqs, col[:, None] * freqs], axis=-1)
    cos, sin = jnp.cos(ang), jnp.sin(ang)
    zero = jnp.zeros_like(cos)
    pad = jnp.zeros((seq, LANES - A_ROPE), F32)
    c = jnp.concatenate([cos, cos, pad], axis=1)
    s1 = jnp.concatenate([-sin, zero, pad], axis=1)
    s2 = jnp.concatenate([zero, sin, pad], axis=1)
    ident = jnp.concatenate([jnp.ones((n_ctx_rows, A_ROPE), F32), jnp.zeros((n_ctx_rows, LANES - A_ROPE), F32)], 1)
    none = jnp.zeros((n_ctx_rows, LANES), F32)
    tile = lambda a, ctx: jnp.concatenate([jnp.tile(a, (n_batch, 1)), ctx], axis=0)
    return tile(c, ident), tile(s1, none), tile(s2, none)


def _pack_keys(k):
    h, nk, dk = k.shape
    eye = jnp.eye(h, dtype=k.dtype)
    blk = k.transpose(1, 0, 2)[:, :, None, :] * eye[None, :, :, None]
    return blk.reshape(nk * h, h * dk).astype(BF16)


def _prepare_layer(w_in, w_uq, w_ukv, w_br_a, w_br_b, w_br_c, w_out, peer_wq, peer_k1, peer_k2, peer_u, peer_v):
    d = w_in.shape[0]
    o_kr = A_QLORA + A_KVLORA
    o_q = o_kr + A_ROPE
    o_pool = o_q + 3 * B_WIDTH
    o_gate = o_pool + C_WIDTH
    w_small = jnp.concatenate([w_in[:, :o_q], jnp.zeros((d, LANES - A_ROPE), w_in.dtype)], axis=1)
    w_uq_pad = jnp.pad(w_uq.reshape(A_QLORA, A_HEADS, A_NOPE + A_ROPE),
                       ((0, 0), (0, 0), (0, A_HEAD_PAD - A_NOPE - A_ROPE))).reshape(A_QLORA, A_HEADS * A_HEAD_PAD)
    w_ukv_perm = w_ukv.reshape(A_KVLORA, A_HEADS, 2, A_NOPE).transpose(0, 2, 1, 3).reshape(A_KVLORA, -1)
    wq_t = peer_wq.reshape(d, P_HEADS, 2, P_DK // 2).transpose(2, 1, 3, 0).reshape(P_HEADS * P_DK, d)
    return dict(
        w_small=w_small.astype(BF16),
        w_qkv=w_in[:, o_q:o_pool].astype(BF16),
        w_pool=w_in[:, o_pool:o_gate].astype(BF16),
        w_gate=w_in[:, o_gate:].astype(BF16),
        w_uq=w_uq_pad.astype(BF16),
        w_ukv=w_ukv_perm.astype(BF16),
        w_br_a=w_br_a.astype(BF16), w_br_b=w_br_b.astype(BF16), w_br_c=w_br_c.astype(BF16),
        w_out=w_out.astype(BF16),
        wq_t=wq_t.astype(BF16),
        k1=_pack_keys(peer_k1), k2=_pack_keys(peer_k2),
        u=peer_u.astype(BF16), v_t=peer_v.T.astype(BF16),
    )


def kernel(x, c, ctx, c_ctx, w_ada, b_ada, norm1_g, norm2_g, w_in, q_norm_g, kv_norm_g, w_uq, w_ukv, na_rel_bias, pool_w, pool_scale, w_br_a, w_br_b, w_br_c, w_out, peer_wq, peer_k1, peer_k2, peer_u, peer_v, final_g):
    n_batch, seq, d = x.shape
    ctx_len = ctx.shape[1]
    depth = w_ada.shape[0]
    n_lat = n_batch * seq
    n_ctx = n_batch * ctx_len
    nt = n_lat + n_ctx
    ctx_blk0 = n_lat // ctx_len

    stream = jnp.concatenate([x.reshape(n_lat, d), ctx.reshape(n_ctx, d)], axis=0)
    mod_rows = 8
    cvec = jnp.concatenate([c, c_ctx[None], jnp.zeros((mod_rows - n_batch - 1, d), c.dtype)], axis=0)
    mod = ada_modulation(cvec, w_ada, b_ada)
    rope_c, rope_s1, rope_s2 = _rope_tables(seq, n_batch, n_ctx)

    for l in range(depth):
        ctx_out = l < depth - 1
        n_act = nt if ctx_out else n_lat
        wl = _prepare_layer(w_in[l], w_uq[l], w_ukv[l], w_br_a[l], w_br_b[l], w_br_c[l], w_out[l],
                            peer_wq[l], peer_k1[l], peer_k2[l], peer_u[l], peer_v[l])
        mod3 = mod[l].reshape(mod_rows, 1, 6 * d)

        h = ln_modulate(stream, norm1_g[l], mod3, 0, nt, seq, n_batch)
        p_small = matmul(h, wl["w_small"], F32)
        qkv = matmul(h, wl["w_qkv"], BF16)
        p_pool = matmul(h, wl["w_pool"], F32)
        gates = matmul(h, wl["w_gate"], F32, act="sigmoid", n_rows=n_act)

        q = mla_q(p_small, q_norm_g[l], wl["w_uq"], rope_c, rope_s1, rope_s2)
        k, v = mla_kv(p_small, kv_norm_g[l], wl["w_ukv"], rope_c, rope_s1, rope_s2)
        a_br = mla_latent_attention(q, k, v, n_batch, seq, ctx_len)
        b_br = na_latent_attention(qkv, na_bias_table(na_rel_bias[l]), n_batch, seq, ctx_len)
        pw = pool_w[l].astype(BF16)
        c_br = pool_mix(p_pool, pw, pool_scale[l], n_batch, seq, 0)
        if ctx_out:
            a_ctx = context_attention(q, k, v, n_batch, seq, ctx_len, A_HEADS, A_HEAD_PAD, A_VDIM, 0, 0, 0,
                                      float((A_NOPE + A_ROPE) ** -0.5))
            b_ctx = context_attention(qkv, qkv, qkv, n_batch, seq, ctx_len, B_HEADS, B_DH, B_DH,
                                      0, B_HEADS, 2 * B_HEADS, float(B_DH ** -0.5))
            c_ctx_br = pool_mix(p_pool, pw, pool_scale[l], n_batch, ctx_len, ctx_blk0)
            a_br = jnp.concatenate([a_br, a_ctx], axis=0)
            b_br = jnp.concatenate([b_br, b_ctx], axis=0)
            c_br = jnp.concatenate([c_br, c_ctx_br], axis=0)

        merged = merge_branches(a_br, b_br, c_br, wl["w_br_a"], wl["w_br_b"], wl["w_br_c"], gates)
        stream = matmul_gated_residual(merged, wl["w_out"], stream, mod3, 2, seq, n_batch)

        h2_t = ln_modulate(stream, norm2_g[l], mod3, 3, n_act, seq, n_batch, transpose=True)
        tables = peer_router(h2_t, wl["wq_t"], wl["k1"], wl["k2"])
        y_t = peer_mixture(h2_t, wl["u"], wl["v_t"], tables)
        stream = residual_from_transposed(stream, y_t, mod3, 5, n_act, seq, n_batch)

    return final_norm(stream, final_g, n_lat).reshape(n_batch, seq, d)
```

```python
import functools

import numpy as np
import jax
import jax.numpy as jnp
from jax import lax
from jax.experimental import pallas as pl
from jax.experimental.pallas import tpu as pltpu

F32 = jnp.float32
BF16 = jnp.bfloat16

GRID_W = 64
EPS = 1e-6
ROPE_THETA = 10000.0

A_HEADS = 16
A_NOPE = 128
A_ROPE = 64
A_VDIM = 128
A_QLORA = 768
A_KVLORA = 256
A_HEAD_PAD = 256

B_HEADS = 8
B_DH = 128
B_WIDTH = B_HEADS * B_DH
WIN_H = 8
WIN_W = 16

C_WIDTH = 1024
POOL_SIZES = (2, 4, 8, 16)
C_GW = C_WIDTH // len(POOL_SIZES)
POOL_HALO = 8

P_HEADS = 8
P_NKEYS = 128
P_DK = 256
P_TOPK = 16

LANES = 128
SUBLANES = 8
SMALL_W = A_QLORA + A_KVLORA + LANES

NEG_BIG = -1e30
MIB = 1024 * 1024


def _cparams(n_axes, vmem_mib):
    return pltpu.CompilerParams(dimension_semantics=("arbitrary",) * n_axes,
                                vmem_limit_bytes=vmem_mib * MIB)


def _segment(row_tile, tile_rows, seq, n_batch):
    return jnp.minimum((row_tile * tile_rows) // seq, n_batch)


def _ada_kernel(c_ref, w_ref, b_ref, o_ref):
    c = c_ref[...]
    a = (c * jax.nn.sigmoid(c)).astype(BF16)
    o_ref[0] = jnp.dot(a, w_ref[0].astype(BF16), preferred_element_type=F32) + b_ref[0]


def ada_modulation(cvec, w_ada, b_ada):
    n_layers, d, n_out = w_ada.shape
    rows = cvec.shape[0]
    tn = 512
    return pl.pallas_call(
        _ada_kernel,
        grid=(n_layers, n_out // tn),
        in_specs=[pl.BlockSpec((rows, d), lambda l, j: (0, 0)),
                  pl.BlockSpec((1, d, tn), lambda l, j: (l, 0, j)),
                  pl.BlockSpec((1, 1, tn), lambda l, j: (l, 0, j))],
        out_specs=pl.BlockSpec((1, rows, tn), lambda l, j: (l, 0, j)),
        out_shape=jax.ShapeDtypeStruct((n_layers, rows, n_out), F32),
        compiler_params=_cparams(2, 40),
        name="ada_modulation",
    )(cvec, w_ada, b_ada.reshape(n_layers, 1, n_out))


def _rms(x, g):
    return x * lax.rsqrt(jnp.mean(x * x, axis=-1, keepdims=True) + EPS) * g


def _ln_mod_kernel(x_ref, g_ref, sc_ref, sh_ref, o_ref, *, transpose):
    h = _rms(x_ref[...], g_ref[...]) * (1.0 + sc_ref[0]) + sh_ref[0]
    if transpose:
        o_ref[...] = h.T.astype(o_ref.dtype)
    else:
        o_ref[...] = h.astype(o_ref.dtype)


def ln_modulate(x, g, mod3, shift_chunk, n_rows, seq, n_batch, transpose=False):
    d = x.shape[1]
    tm = 256
    seg = functools.partial(_segment, tile_rows=tm, seq=seq, n_batch=n_batch)
    if transpose:
        out_spec = pl.BlockSpec((d, tm), lambda i: (0, i))
        out_shape = jax.ShapeDtypeStruct((d, n_rows), BF16)
    else:
        out_spec = pl.BlockSpec((tm, d), lambda i: (i, 0))
        out_shape = jax.ShapeDtypeStruct((n_rows, d), BF16)
    return pl.pallas_call(
        functools.partial(_ln_mod_kernel, transpose=transpose),
        grid=(n_rows // tm,),
        in_specs=[pl.BlockSpec((tm, d), lambda i: (i, 0)),
                  pl.BlockSpec((1, d), lambda i: (0, 0)),
                  pl.BlockSpec((1, 1, d), lambda i: (seg(i), 0, shift_chunk + 1)),
                  pl.BlockSpec((1, 1, d), lambda i: (seg(i), 0, shift_chunk))],
        out_specs=out_spec,
        out_shape=out_shape,
        compiler_params=_cparams(1, 48),
        name="ln_modulate_t" if transpose else "ln_modulate",
    )(x, g.reshape(1, d), mod3, mod3)


def _final_norm_kernel(x_ref, g_ref, o_ref):
    o_ref[...] = _rms(x_ref[...], g_ref[...])


def final_norm(x, g, n_rows):
    d = x.shape[1]
    tm = 256
    return pl.pallas_call(
        _final_norm_kernel,
        grid=(n_rows // tm,),
        in_specs=[pl.BlockSpec((tm, d), lambda i: (i, 0)),
                  pl.BlockSpec((1, d), lambda i: (0, 0))],
        out_specs=pl.BlockSpec((tm, d), lambda i: (i, 0)),
        out_shape=jax.ShapeDtypeStruct((n_rows, d), F32),
        compiler_params=_cparams(1, 48),
        name="final_norm",
    )(x, g.reshape(1, d))


def _mm_kernel(a_ref, b_ref, o_ref, *, act):
    acc = jnp.dot(a_ref[...], b_ref[...], preferred_element_type=F32)
    if act == "sigmoid":
        acc = jax.nn.sigmoid(acc)
    o_ref[...] = acc.astype(o_ref.dtype)


def matmul(a, b, out_dtype, act=None, n_rows=None, tn=1024, tm=512):
    m, k = a.shape
    m = n_rows or m
    n = b.shape[1]
    if n % tn:
        tn = n
    return pl.pallas_call(
        functools.partial(_mm_kernel, act=act),
        grid=(n // tn, m // tm),
        in_specs=[pl.BlockSpec((tm, k), lambda j, i: (i, 0)),
                  pl.BlockSpec((k, tn), lambda j, i: (0, j))],
        out_specs=pl.BlockSpec((tm, tn), lambda j, i: (i, j)),
        out_shape=jax.ShapeDtypeStruct((m, n), out_dtype),
        compiler_params=_cparams(2, 52),
        name="matmul_" + (act or "plain"),
    )(a, b)


def _mm_residual_kernel(a_ref, b_ref, x_ref, g_ref, o_ref):
    acc = jnp.dot(a_ref[...], b_ref[...], preferred_element_type=F32)
    o_ref[...] = x_ref[...] + g_ref[0] * acc


def matmul_gated_residual(a, b, x, mod3, gate_chunk, seq, n_batch, tn=1024, tm=512):
    m, k = a.shape
    n = b.shape[1]
    seg = functools.partial(_segment, tile_rows=tm, seq=seq, n_batch=n_batch)
    nj = n // tn
    return pl.pallas_call(
        _mm_residual_kernel,
        grid=(nj, m // tm),
        in_specs=[pl.BlockSpec((tm, k), lambda j, i: (i, 0)),
                  pl.BlockSpec((k, tn), lambda j, i: (0, j)),
                  pl.BlockSpec((tm, tn), lambda j, i: (i, j)),
                  pl.BlockSpec((1, 1, tn), lambda j, i: (seg(i), 0, gate_chunk * nj + j))],
        out_specs=pl.BlockSpec((tm, tn), lambda j, i: (i, j)),
        out_shape=jax.ShapeDtypeStruct((m, n), F32),
        compiler_params=_cparams(2, 52),
        name="matmul_gated_residual",
    )(a, b, x, mod3)


def _merge_kernel(a_ref, b_ref, c_ref, wa_ref, wb_ref, wc_ref, ga_ref, gb_ref, gc_ref, o_ref):
    m = ga_ref[...] * jnp.dot(a_ref[...], wa_ref[...], preferred_element_type=F32)
    m = m + gb_ref[...] * jnp.dot(b_ref[...], wb_ref[...], preferred_element_type=F32)
    m = m + gc_ref[...] * jnp.dot(c_ref[...], wc_ref[...], preferred_element_type=F32)
    o_ref[...] = m.astype(o_ref.dtype)


def merge_branches(a, b, c, wa, wb, wc, gates, tn=1024, tm=512):
    m = a.shape[0]
    n = wa.shape[1]
    nj = n // tn
    row = lambda j, i: (i, 0)
    col = lambda j, i: (0, j)
    return pl.pallas_call(
        _merge_kernel,
        grid=(nj, m // tm),
        in_specs=[pl.BlockSpec((tm, a.shape[1]), row),
                  pl.BlockSpec((tm, b.shape[1]), row),
                  pl.BlockSpec((tm, c.shape[1]), row),
                  pl.BlockSpec((wa.shape[0], tn), col),
                  pl.BlockSpec((wb.shape[0], tn), col),
                  pl.BlockSpec((wc.shape[0], tn), col),
                  pl.BlockSpec((tm, tn), lambda j, i: (i, j)),
                  pl.BlockSpec((tm, tn), lambda j, i: (i, nj + j)),
                  pl.BlockSpec((tm, tn), lambda j, i: (i, 2 * nj + j))],
        out_specs=pl.BlockSpec((tm, tn), lambda j, i: (i, j)),
        out_shape=jax.ShapeDtypeStruct((m, n), BF16),
        compiler_params=_cparams(2, 52),
        name="merge_branches",
    )(a, b, c, wa, wb, wc, gates, gates, gates)


def _rotate(v, c_ref, s1_ref, s2_ref):
    return (v * c_ref[...] + pltpu.roll(v, LANES - A_ROPE // 2, 1) * s1_ref[...]
            + pltpu.roll(v, A_ROPE // 2, 1) * s2_ref[...])


def _mla_q_kernel(cq_ref, g_ref, w_ref, c_ref, s1_ref, s2_ref, o_ref):
    y = _rms(cq_ref[...], g_ref[...]).astype(BF16)
    acc = jnp.dot(y, w_ref[...], preferred_element_type=F32)
    for h in range(A_HEADS):
        lo = h * A_HEAD_PAD
        o_ref[:, lo:lo + A_NOPE] = acc[:, lo:lo + A_NOPE].astype(o_ref.dtype)
        rot = _rotate(acc[:, lo + A_NOPE:lo + A_HEAD_PAD], c_ref, s1_ref, s2_ref)
        o_ref[:, lo + A_NOPE:lo + A_HEAD_PAD] = rot.astype(o_ref.dtype)


def mla_q(p_small, g, w_uq_pad, rope_c, rope_s1, rope_s2):
    nt = p_small.shape[0]
    tm = 256
    width = A_HEADS * A_HEAD_PAD
    tab = pl.BlockSpec((tm, LANES), lambda i: (i, 0))
    return pl.pallas_call(
        _mla_q_kernel,
        grid=(nt // tm,),
        in_specs=[pl.BlockSpec((tm, A_QLORA), lambda i: (i, 0)),
                  pl.BlockSpec((1, A_QLORA), lambda i: (0, 0)),
                  pl.BlockSpec((A_QLORA, width), lambda i: (0, 0)),
                  tab, tab, tab],
        out_specs=pl.BlockSpec((tm, width), lambda i: (i, 0)),
        out_shape=jax.ShapeDtypeStruct((nt, width), BF16),
        compiler_params=_cparams(1, 48),
        name="mla_q",
    )(p_small, g.reshape(1, A_QLORA), w_uq_pad, rope_c, rope_s1, rope_s2)


def _mla_kv_kernel(ckv_ref, kr_ref, g_ref, w_ref, c_ref, s1_ref, s2_ref, k_ref, v_ref):
    y = _rms(ckv_ref[...], g_ref[...]).astype(BF16)
    acc = jnp.dot(y, w_ref[...], preferred_element_type=F32)
    kr = _rotate(kr_ref[...], c_ref, s1_ref, s2_ref).astype(k_ref.dtype)
    for h in range(A_HEADS):
        lo = h * A_HEAD_PAD
        k_ref[:, lo:lo + A_NOPE] = acc[:, h * A_NOPE:(h + 1) * A_NOPE].astype(k_ref.dtype)
        k_ref[:, lo + A_NOPE:lo + A_HEAD_PAD] = kr
    v_ref[...] = acc[:, A_HEADS * A_NOPE:].astype(v_ref.dtype)


def mla_kv(p_small, g, w_ukv_perm, rope_c, rope_s1, rope_s2):
    nt = p_small.shape[0]
    tm = 256
    kw = A_HEADS * A_HEAD_PAD
    vw = A_HEADS * A_VDIM
    tab = pl.BlockSpec((tm, LANES), lambda i: (i, 0))
    return pl.pallas_call(
        _mla_kv_kernel,
        grid=(nt // tm,),
        in_specs=[pl.BlockSpec((tm, A_KVLORA), lambda i: (i, A_QLORA // A_KVLORA)),
                  pl.BlockSpec((tm, LANES), lambda i: (i, (A_QLORA + A_KVLORA) // LANES)),
                  pl.BlockSpec((1, A_KVLORA), lambda i: (0, 0)),
                  pl.BlockSpec((A_KVLORA, A_HEADS * (A_NOPE + A_VDIM)), lambda i: (0, 0)),
                  tab, tab, tab],
        out_specs=[pl.BlockSpec((tm, kw), lambda i: (i, 0)),
                   pl.BlockSpec((tm, vw), lambda i: (i, 0))],
        out_shape=[jax.ShapeDtypeStruct((nt, kw), BF16),
                   jax.ShapeDtypeStruct((nt, vw), BF16)],
        compiler_params=_cparams(1, 48),
        name="mla_kv",
    )(p_small, p_small, g.reshape(1, A_KVLORA), w_ukv_perm, rope_c, rope_s1, rope_s2)


def _qk(q, k):
    return lax.dot_general(q, k, (((1,), (1,)), ((), ())), preferred_element_type=F32)


LOG2_E = 1.4426950408889634


def _flash_two_sets_kernel(q_ref, kl_ref, vl_ref, kc_ref, vc_ref, o_ref, *, scale, chunk):
    q = q_ref[...]
    c = scale * LOG2_E
    n_chunks = kl_ref.shape[0] // chunk
    keys = lambda j: kl_ref[j * chunk:(j + 1) * chunk, :]
    t = _qk(q, kc_ref[...]) * c
    t_next = _qk(q, keys(0)) * c
    m = jnp.max(t, axis=-1, keepdims=True)
    p = jnp.exp2(t - m)
    den = jnp.sum(p, axis=-1, keepdims=True)
    acc = jnp.dot(p.astype(BF16), vc_ref[...], preferred_element_type=F32)
    for j in range(n_chunks):
        t = t_next
        if j + 1 < n_chunks:
            t_next = _qk(q, keys(j + 1)) * c
        m_new = jnp.maximum(m, jnp.max(t, axis=-1, keepdims=True))
        alpha = jnp.exp2(m - m_new)
        p = jnp.exp2(t - m_new)
        den = alpha * den + jnp.sum(p, axis=-1, keepdims=True)
        acc = alpha * acc + jnp.dot(p.astype(BF16), vl_ref[j * chunk:(j + 1) * chunk, :],
                                    preferred_element_type=F32)
        m = m_new
    o_ref[...] = (acc / den).astype(o_ref.dtype)


def mla_latent_attention(q, k, v, n_batch, seq, ctx_len, tq=512, chunk=512):
    ctx_blk0 = (n_batch * seq) // ctx_len
    nq = seq // tq
    scale = float((A_NOPE + A_ROPE) ** -0.5)
    return pl.pallas_call(
        functools.partial(_flash_two_sets_kernel, scale=scale, chunk=chunk),
        grid=(n_batch, A_HEADS, nq),
        in_specs=[pl.BlockSpec((tq, A_HEAD_PAD), lambda b, h, i: (b * nq + i, h)),
                  pl.BlockSpec((seq, A_HEAD_PAD), lambda b, h, i: (b, h)),
                  pl.BlockSpec((seq, A_VDIM), lambda b, h, i: (b, h)),
                  pl.BlockSpec((ctx_len, A_HEAD_PAD), lambda b, h, i: (ctx_blk0 + b, h)),
                  pl.BlockSpec((ctx_len, A_VDIM), lambda b, h, i: (ctx_blk0 + b, h))],
        out_specs=pl.BlockSpec((tq, A_VDIM), lambda b, h, i: (b * nq + i, h)),
        out_shape=jax.ShapeDtypeStruct((n_batch * seq, A_HEADS * A_VDIM), BF16),
        compiler_params=_cparams(3, 48),
        name="mla_latent_attention",
    )(q, k, v, k, v)


def _attn_one_set_kernel(q_ref, k_ref, v_ref, o_ref, *, scale):
    s = _qk(q_ref[...], k_ref[...]) * scale
    p = jnp.exp(s - jnp.max(s, axis=-1, keepdims=True))
    den = jnp.sum(p, axis=-1, keepdims=True)
    o = jnp.dot(p.astype(BF16), v_ref[...], preferred_element_type=F32)
    o_ref[...] = (o / den).astype(o_ref.dtype)


def context_attention(q, k, v, n_batch, seq, ctx_len, n_heads, dqk, dv, q_col0, k_col0, v_col0, scale):
    ctx_blk0 = (n_batch * seq) // ctx_len
    return pl.pallas_call(
        functools.partial(_attn_one_set_kernel, scale=scale),
        grid=(n_batch, n_heads),
        in_specs=[pl.BlockSpec((ctx_len, dqk), lambda b, h: (ctx_blk0 + b, q_col0 + h)),
                  pl.BlockSpec((ctx_len, dqk), lambda b, h: (ctx_blk0 + b, k_col0 + h)),
                  pl.BlockSpec((ctx_len, dv), lambda b, h: (ctx_blk0 + b, v_col0 + h))],
        out_specs=pl.BlockSpec((ctx_len, dv), lambda b, h: (b, h)),
        out_shape=jax.ShapeDtypeStruct((n_batch * ctx_len, n_heads * dv), BF16),
        compiler_params=_cparams(2, 32),
        name="context_attention",
    )(q, k, v)


def _na_kernel(q_ref, kl_ref, vl_ref, kc_ref, vc_ref, bias_ref, o_ref, *, rows, scale):
    r = pl.program_id(1)
    r0 = jnp.clip(r - WIN_H // 2, 0, rows - WIN_H)
    start = pl.multiple_of(r0 * GRID_W, GRID_W)
    win = WIN_H * GRID_W
    for h in range(B_HEADS):
        cols = slice(h * B_DH, (h + 1) * B_DH)
        q = q_ref[:, cols]
        s_l = _qk(q, kl_ref[pl.ds(start, win), cols]) * scale + bias_ref[0, h]
        s_c = _qk(q, kc_ref[:, cols]) * scale
        m = jnp.maximum(jnp.max(s_l, axis=-1, keepdims=True), jnp.max(s_c, axis=-1, keepdims=True))
        p_l = jnp.exp(s_l - m)
        p_c = jnp.exp(s_c - m)
        den = jnp.sum(p_l, axis=-1, keepdims=True) + jnp.sum(p_c, axis=-1, keepdims=True)
        o = jnp.dot(p_l.astype(BF16), vl_ref[pl.ds(start, win), cols], preferred_element_type=F32)
        o = o + jnp.dot(p_c.astype(BF16), vc_ref[:, cols], preferred_element_type=F32)
        o_ref[:, cols] = (o / den).astype(o_ref.dtype)


def na_latent_attention(qkv, bias, n_batch, seq, ctx_len):
    rows = seq // GRID_W
    ctx_blk0 = (n_batch * seq) // ctx_len
    win = WIN_H * GRID_W

    def bias_idx(b, r):
        return (r - jnp.clip(r - WIN_H // 2, 0, rows - WIN_H), 0, 0, 0)

    once = pl.Buffered(1)
    return pl.pallas_call(
        functools.partial(_na_kernel, rows=rows, scale=float(B_DH ** -0.5)),
        grid=(n_batch, rows),
        in_specs=[pl.BlockSpec((GRID_W, B_WIDTH), lambda b, r: (b * rows + r, 0)),
                  pl.BlockSpec((seq, B_WIDTH), lambda b, r: (b, 1), pipeline_mode=once),
                  pl.BlockSpec((seq, B_WIDTH), lambda b, r: (b, 2), pipeline_mode=once),
                  pl.BlockSpec((ctx_len, B_WIDTH), lambda b, r: (ctx_blk0 + b, 1)),
                  pl.BlockSpec((ctx_len, B_WIDTH), lambda b, r: (ctx_blk0 + b, 2)),
                  pl.BlockSpec((1, B_HEADS, GRID_W, win), bias_idx)],
        out_specs=pl.BlockSpec((GRID_W, B_WIDTH), lambda b, r: (b * rows + r, 0)),
        out_shape=jax.ShapeDtypeStruct((n_batch * seq, B_WIDTH), BF16),
        compiler_params=_cparams(2, 40),
        name="na_latent_attention",
    )(qkv, qkv, qkv, qkv, qkv, bias)


def na_bias_table(rel_bias):
    rel = rel_bias.astype(F32)
    by_row = jnp.stack([rel[:, WIN_H - 1 - d:2 * WIN_H - 1 - d, :] for d in range(WIN_H)], axis=1)
    pad = GRID_W - 1
    by_row = jnp.pad(by_row, ((0, 0), (0, 0), (0, 0), (pad, pad)))
    off = WIN_W - 1 + pad
    tab = jnp.stack([by_row[..., off - c:off - c + GRID_W] for c in range(GRID_W)], axis=3)
    c = jnp.arange(GRID_W)[:, None]
    kc = jnp.arange(GRID_W)[None, :]
    c0 = jnp.clip(c - WIN_W // 2, 0, GRID_W - WIN_W)
    inside = (kc >= c0) & (kc < c0 + WIN_W)
    tab = jnp.where(inside[None, None, None], tab, NEG_BIG)
    return tab.transpose(1, 0, 3, 2, 4).reshape(WIN_H, B_HEADS, GRID_W, WIN_H * GRID_W)


def _pool_kernel(x_ref, w_ref, sc_ref, o_ref, pad_ref, *, seq_len, chunk):
    g = pl.program_id(1)
    zeros = jnp.zeros((POOL_HALO, C_GW), F32)
    pad_ref[0:POOL_HALO, :] = zeros
    pad_ref[POOL_HALO + seq_len:2 * POOL_HALO + seq_len, :] = zeros
    pad_ref[POOL_HALO:POOL_HALO + seq_len, :] = x_ref[...]
    for gi, w in enumerate(POOL_SIZES):
        @pl.when(g == gi)
        def _(w=w):
            for c0 in range(0, seq_len, chunk):
                acc = None
                for k in range(-(w // 2), w - w // 2):
                    piece = pad_ref[POOL_HALO + c0 + k:POOL_HALO + c0 + k + chunk, :]
                    acc = piece if acc is None else acc + piece
                t = c0 + lax.broadcasted_iota(jnp.int32, (chunk, C_GW), 0)
                cnt = jnp.minimum(t + (w - w // 2), seq_len) - jnp.maximum(t - w // 2, 0)
                d = (acc / cnt.astype(F32) - x_ref[c0:c0 + chunk, :]).astype(BF16)
                y = jnp.dot(d, w_ref[0], preferred_element_type=F32) * sc_ref[...]
                o_ref[c0:c0 + chunk, :] = y.astype(o_ref.dtype)


def pool_mix(p_pool, pool_w, pool_scale, n_seq, seq_len, first_block):
    chunk = min(seq_len, 512)
    return pl.pallas_call(
        functools.partial(_pool_kernel, seq_len=seq_len, chunk=chunk),
        grid=(n_seq, len(POOL_SIZES)),
        in_specs=[pl.BlockSpec((seq_len, C_GW), lambda b, g: (first_block + b, g)),
                  pl.BlockSpec((1, C_GW, C_GW), lambda b, g: (g, 0, 0)),
                  pl.BlockSpec((1, C_GW), lambda b, g: (0, g))],
        out_specs=pl.BlockSpec((seq_len, C_GW), lambda b, g: (b, g)),
        out_shape=jax.ShapeDtypeStruct((n_seq * seq_len, C_WIDTH), BF16),
        scratch_shapes=[pltpu.VMEM((seq_len + 2 * POOL_HALO, C_GW), F32)],
        compiler_params=_cparams(2, 48),
        name="pool_mix",
    )(p_pool, pool_w, pool_scale.reshape(1, C_WIDTH))


_REGION = [(r1, r2) for r1 in range(P_TOPK) for r2 in range(P_TOPK) if (r1 + 1) * (r2 + 1) <= P_TOPK]


def _dominates(a, b):
    return a != b and a[0] <= b[0] and a[1] <= b[1]


def _router_kernel(h_ref, wq_ref, k1_ref, k2_ref, cnt_ref, a_ref, rk_ref, b_ref,
                   s_scr, v1_scr, i1_scr, v2_scr, i2_scr, cr_scr, e1_scr, e2_scr):
    tm = h_ref.shape[1]
    half = P_HEADS * P_DK // 2
    q_t = jnp.dot(wq_ref[...], h_ref[...], preferred_element_type=F32)
    key = lax.broadcasted_iota(jnp.int32, (P_NKEYS, P_HEADS, tm), 0)

    def top_sorted(k_ref, q_half, v_scr, i_scr):
        s_scr[...] = jnp.dot(k_ref[...], q_half.astype(BF16),
                             preferred_element_type=F32).reshape(P_NKEYS, P_HEADS, tm)

        def body(r, carry):
            s = s_scr[...]
            m = jnp.max(s, axis=0)
            idx = jnp.min(jnp.where(s == m[None], key, P_NKEYS), axis=0)
            s_scr[...] = jnp.where(key == idx[None], -jnp.inf, s)
            v_scr[r] = m
            i_scr[r] = idx
            return carry

        lax.fori_loop(0, P_TOPK, body, 0)

    top_sorted(k1_ref, q_t[:half], v1_scr, i1_scr)
    top_sorted(k2_ref, q_t[half:], v2_scr, i2_scr)

    v1 = [v1_scr[r] for r in range(P_TOPK)]
    v2 = [v2_scr[r] for r in range(P_TOPK)]
    sums = {c: v1[c[0]] + v2[c[1]] for c in _REGION}
    beaten = {c: float(sum(_dominates(o, c) for o in _REGION)) for c in _REGION}
    for ci, c in enumerate(_REGION):
        for o in _REGION[:ci]:
            if _dominates(o, c):
                continue
            o_first = (sums[o] >= sums[c]).astype(F32)
            beaten[c] = beaten[c] + o_first
            beaten[o] = beaten[o] + (1.0 - o_first)
    e1 = [jnp.exp(v1[r] - v1[0]) for r in range(P_TOPK)]
    e2 = [jnp.exp(v2[r] - v2[0]) for r in range(P_TOPK)]
    z = jnp.zeros_like(v1[0])
    counts = [jnp.zeros_like(v1[0]) for _ in range(P_TOPK)]
    for c in _REGION:
        chosen = beaten[c] < float(P_TOPK)
        z = z + jnp.where(chosen, e1[c[0]] * e2[c[1]], 0.0)
        counts[c[0]] = counts[c[0]] + jnp.where(chosen, 1.0, 0.0)
    inv_z = 1.0 / z
    for r in range(P_TOPK):
        cr_scr[r] = counts[r]
        e1_scr[r] = e1[r]
        e2_scr[r] = e2[r] * inv_z

    key2 = lax.broadcasted_iota(jnp.int32, (P_NKEYS, LANES), 0)

    def expand(n, carry, h):
        t0 = pl.multiple_of(n * LANES, LANES)
        row = lambda ref, r: ref[r, h:h + 1, pl.ds(t0, LANES)]
        cnt = jnp.zeros((P_NKEYS, LANES), F32)
        a = cnt
        for r in range(P_TOPK):
            hit = key2 == row(i1_scr, r)
            cnt = jnp.where(hit, row(cr_scr, r), cnt)
            a = jnp.where(hit, row(e1_scr, r), a)
        cnt_ref[h, :, pl.ds(t0, LANES)] = cnt
        a_ref[h, :, pl.ds(t0, LANES)] = a
        rk = jnp.full((P_NKEYS, LANES), float(P_TOPK), F32)
        b = jnp.zeros((P_NKEYS, LANES), F32)
        for r in range(P_TOPK):
            hit = key2 == row(i2_scr, r)
            rk = jnp.where(hit, float(r), rk)
            b = jnp.where(hit, row(e2_scr, r), b)
        rk_ref[h, :, pl.ds(t0, LANES)] = rk.astype(rk_ref.dtype)
        b_ref[h, :, pl.ds(t0, LANES)] = b.astype(b_ref.dtype)
        return carry

    for h in range(P_HEADS):
        lax.fori_loop(0, tm // LANES, functools.partial(expand, h=h), 0)


def peer_router(h_t, wq_t, k1_packed, k2_packed, tm=256):
    d, n = h_t.shape
    qw = wq_t.shape[0]
    kp = k1_packed.shape[0]
    tab = lambda dt: jax.ShapeDtypeStruct((P_HEADS, P_NKEYS, n), dt)
    tab_spec = pl.BlockSpec((P_HEADS, P_NKEYS, tm), lambda i: (0, 0, i))
    small = lambda dt: pltpu.VMEM((P_TOPK, P_HEADS, tm), dt)
    return pl.pallas_call(
        _router_kernel,
        grid=(n // tm,),
        in_specs=[pl.BlockSpec((d, tm), lambda i: (0, i)),
                  pl.BlockSpec((qw, d), lambda i: (0, 0), pipeline_mode=pl.Buffered(1)),
                  pl.BlockSpec((kp, kp), lambda i: (0, 0), pipeline_mode=pl.Buffered(1)),
                  pl.BlockSpec((kp, kp), lambda i: (0, 0), pipeline_mode=pl.Buffered(1))],
        out_specs=[tab_spec] * 4,
        out_shape=[tab(F32)] * 4,
        scratch_shapes=[pltpu.VMEM((P_NKEYS, P_HEADS, tm), F32),
                        small(F32), small(jnp.int32), small(F32), small(jnp.int32),
                        small(F32), small(F32), small(F32)],
        compiler_params=_cparams(1, 52),
        name="peer_router",
    )(h_t, wq_t, k1_packed, k2_packed)


def _gelu_exact(x):
    return 0.5 * x * (1.0 + lax.erf(x * float(np.sqrt(0.5))))


def _peer_kernel(h_ref, u_ref, v_ref, cnt_ref, a_ref, rk_ref, b_ref, o_ref, ga_scr):
    e = pl.program_id(1)
    te, tm = ga_scr.shape

    def step(first):
        halves = [(c0, jnp.dot(u_ref[c0:c0 + te // 2, :], h_ref[...], preferred_element_type=F32))
                  for c0 in (0, te // 2)]
        rows_per_half = te // 2 // P_NKEYS
        n_slab = P_NKEYS // SUBLANES
        for c0, pre in halves:
            i0 = e * (te // P_NKEYS) + c0 // P_NKEYS
            rows = [[(cnt_ref[h, pl.ds(i0 + k, 1), :], a_ref[h, pl.ds(i0 + k, 1), :]) for h in range(P_HEADS)]
                    for k in range(rows_per_half)]
            for t0 in range(0, tm, LANES):
                lanes = slice(t0, t0 + LANES)
                g = [[None] * n_slab for _ in range(rows_per_half)]
                for h in range(P_HEADS):
                    per_row = [[jnp.broadcast_to(x[:, lanes], (SUBLANES, LANES)) for x in rows[k][h]]
                               for k in range(rows_per_half)]
                    for s in range(n_slab):
                        slab = slice(s * SUBLANES, (s + 1) * SUBLANES)
                        rk = rk_ref[h, slab, lanes]
                        b = b_ref[h, slab, lanes]
                        for k, (cnt, a) in enumerate(per_row):
                            term = jnp.where(rk < cnt, b * a, jnp.zeros_like(b))
                            g[k][s] = term if g[k][s] is None else g[k][s] + term
                for k in range(rows_per_half):
                    for s in range(0, n_slab, 2):
                        r = k * P_NKEYS + s * SUBLANES
                        act = _gelu_exact(pre[r:r + 2 * SUBLANES, lanes])
                        gates = jnp.concatenate([g[k][s], g[k][s + 1]], axis=0)
                        ga_scr[c0 + r:c0 + r + 2 * SUBLANES, lanes] = (gates * act).astype(BF16)
            contrib = lax.dot_general(v_ref[c0:c0 + te // 2, :], ga_scr[c0:c0 + te // 2, :],
                                      (((0,), (0,)), ((), ())), preferred_element_type=F32)
            if first and c0 == 0:
                o_ref[...] = contrib
            else:
                o_ref[...] += contrib

    pl.when(e == 0)(functools.partial(step, True))
    pl.when(e > 0)(functools.partial(step, False))


def peer_mixture(h_t, u, v, tables, tm=512, te=512):
    d, n = h_t.shape
    tab_spec = pl.BlockSpec((P_HEADS, P_NKEYS, tm), lambda i, e: (0, 0, i), pipeline_mode=pl.Buffered(1))
    return pl.pallas_call(
        _peer_kernel,
        grid=(n // tm, u.shape[0] // te),
        in_specs=[pl.BlockSpec((d, tm), lambda i, e: (0, i), pipeline_mode=pl.Buffered(1)),
                  pl.BlockSpec((te, d), lambda i, e: (e, 0)),
                  pl.BlockSpec((te, d), lambda i, e: (e, 0))] + [tab_spec] * 4,
        out_specs=pl.BlockSpec((d, tm), lambda i, e: (0, i)),
        out_shape=jax.ShapeDtypeStruct((d, n), F32),
        scratch_shapes=[pltpu.VMEM((te, tm), BF16)],
        compiler_params=_cparams(2, 56),
        name="peer_mixture",
    )(h_t, u, v, *tables)


def _residual_t_kernel(x_ref, y_ref, g_ref, o_ref):
    o_ref[...] = x_ref[...] + g_ref[0] * y_ref[...].T


def residual_from_transposed(x, y_t, mod3, gate_chunk, n_rows, seq, n_batch):
    d = x.shape[1]
    tm = 256
    seg = functools.partial(_segment, tile_rows=tm, seq=seq, n_batch=n_batch)
    return pl.pallas_call(
        _residual_t_kernel,
        grid=(n_rows // tm,),
        in_specs=[pl.BlockSpec((tm, d), lambda i: (i, 0)),
                  pl.BlockSpec((d, tm), lambda i: (0, i)),
                  pl.BlockSpec((1, 1, d), lambda i: (seg(i), 0, gate_chunk))],
        out_specs=pl.BlockSpec((tm, d), lambda i: (i, 0)),
        out_shape=jax.ShapeDtypeStruct((n_rows, d), F32),
        compiler_params=_cparams(1, 48),
        name="residual_from_transposed",
    )(x, y_t, mod3)


def _rope_tables(seq, n_batch, n_ctx_rows):
    t = jnp.arange(seq)
    row = (t // GRID_W).astype(F32)
    col = (t % GRID_W).astype(F32)
    n_freq = A_ROPE // 4
    freqs = ROPE_THETA ** (-jnp.arange(n_freq, dtype=F32) / n_freq)
    ang = jnp.concatenate([row[:, None] * freqs, col[:, None] * freqs], axis=-1)
    cos, sin = jnp.cos(ang), jnp.sin(ang)
    zero = jnp.zeros_like(cos)
    pad = jnp.zeros((seq, LANES - A_ROPE), F32)
    c = jnp.concatenate([cos, cos, pad], axis=1)
    s1 = jnp.concatenate([-sin, zero, pad], axis=1)
    s2 = jnp.concatenate([zero, sin, pad], axis=1)
    ident = jnp.concatenate([jnp.ones((n_ctx_rows, A_ROPE), F32), jnp.zeros((n_ctx_rows, LANES - A_ROPE), F32)], 1)
    none = jnp.zeros((n_ctx_rows, LANES), F32)
    tile = lambda a, ctx: jnp.concatenate([jnp.tile(a, (n_batch, 1)), ctx], axis=0)
    return tile(c, ident), tile(s1, none), tile(s2, none)


def _pack_keys(k):
    h, nk, dk = k.shape
    eye = jnp.eye(h, dtype=k.dtype)
    blk = k.transpose(1, 0, 2)[:, :, None, :] * eye[None, :, :, None]
    return blk.reshape(nk * h, h * dk).astype(BF16)


def _prepare_layer(w_in, w_uq, w_ukv, w_br_a, w_br_b, w_br_c, w_out, peer_wq, peer_k1, peer_k2, peer_u, peer_v):
    d = w_in.shape[0]
    o_kr = A_QLORA + A_KVLORA
    o_q = o_kr + A_ROPE
    o_pool = o_q + 3 * B_WIDTH
    o_gate = o_pool + C_WIDTH
    w_small = jnp.concatenate([w_in[:, :o_q], jnp.zeros((d, LANES - A_ROPE), w_in.dtype)], axis=1)
    w_uq_pad = jnp.pad(w_uq.reshape(A_QLORA, A_HEADS, A_NOPE + A_ROPE),
                       ((0, 0), (0, 0), (0, A_HEAD_PAD - A_NOPE - A_ROPE))).reshape(A_QLORA, A_HEADS * A_HEAD_PAD)
    w_ukv_perm = w_ukv.reshape(A_KVLORA, A_HEADS, 2, A_NOPE).transpose(0, 2, 1, 3).reshape(A_KVLORA, -1)
    wq_t = peer_wq.reshape(d, P_HEADS, 2, P_DK // 2).transpose(2, 1, 3, 0).reshape(P_HEADS * P_DK, d)
    return dict(
        w_small=w_small.astype(BF16),
        w_qkv=w_in[:, o_q:o_pool].astype(BF16),
        w_pool=w_in[:, o_pool:o_gate].astype(BF16),
        w_gate=w_in[:, o_gate:].astype(BF16),
        w_uq=w_uq_pad.astype(BF16),
        w_ukv=w_ukv_perm.astype(BF16),
        w_br_a=w_br_a.astype(BF16), w_br_b=w_br_b.astype(BF16), w_br_c=w_br_c.astype(BF16),
        w_out=w_out.astype(BF16),
        wq_t=wq_t.astype(BF16),
        k1=_pack_keys(peer_k1), k2=_pack_keys(peer_k2),
        u=peer_u.astype(BF16), v=peer_v.astype(BF16),
    )


def kernel(x, c, ctx, c_ctx, w_ada, b_ada, norm1_g, norm2_g, w_in, q_norm_g, kv_norm_g, w_uq, w_ukv, na_rel_bias, pool_w, pool_scale, w_br_a, w_br_b, w_br_c, w_out, peer_wq, peer_k1, peer_k2, peer_u, peer_v, final_g):
    n_batch, seq, d = x.shape
    ctx_len = ctx.shape[1]
    depth = w_ada.shape[0]
    n_lat = n_batch * seq
    n_ctx = n_batch * ctx_len
    nt = n_lat + n_ctx
    ctx_blk0 = n_lat // ctx_len

    stream = jnp.concatenate([x.reshape(n_lat, d), ctx.reshape(n_ctx, d)], axis=0)
    mod_rows = 8
    cvec = jnp.concatenate([c, c_ctx[None], jnp.zeros((mod_rows - n_batch - 1, d), c.dtype)], axis=0)
    mod = ada_modulation(cvec, w_ada, b_ada)
    rope_c, rope_s1, rope_s2 = _rope_tables(seq, n_batch, n_ctx)

    for l in range(depth):
        ctx_out = l < depth - 1
        n_act = nt if ctx_out else n_lat
        wl = _prepare_layer(w_in[l], w_uq[l], w_ukv[l], w_br_a[l], w_br_b[l], w_br_c[l], w_out[l],
                            peer_wq[l], peer_k1[l], peer_k2[l], peer_u[l], peer_v[l])
        mod3 = mod[l].reshape(mod_rows, 1, 6 * d)

        h = ln_modulate(stream, norm1_g[l], mod3, 0, nt, seq, n_batch)
        p_small = matmul(h, wl["w_small"], F32)
        qkv = matmul(h, wl["w_qkv"], BF16)
        p_pool = matmul(h, wl["w_pool"], F32)
        gates = matmul(h, wl["w_gate"], F32, act="sigmoid", n_rows=n_act)

        q = mla_q(p_small, q_norm_g[l], wl["w_uq"], rope_c, rope_s1, rope_s2)
        k, v = mla_kv(p_small, kv_norm_g[l], wl["w_ukv"], rope_c, rope_s1, rope_s2)
        a_br = mla_latent_attention(q, k, v, n_batch, seq, ctx_len)
        b_br = na_latent_attention(qkv, na_bias_table(na_rel_bias[l]), n_batch, seq, ctx_len)
        pw = pool_w[l].astype(BF16)
        c_br = pool_mix(p_pool, pw, pool_scale[l], n_batch, seq, 0)
        if ctx_out:
            a_ctx = context_attention(q, k, v, n_batch, seq, ctx_len, A_HEADS, A_HEAD_PAD, A_VDIM, 0, 0, 0,
                                      float((A_NOPE + A_ROPE) ** -0.5))
            b_ctx = context_attention(qkv, qkv, qkv, n_batch, seq, ctx_len, B_HEADS, B_DH, B_DH,
                                      0, B_HEADS, 2 * B_HEADS, float(B_DH ** -0.5))
            c_ctx_br = pool_mix(p_pool, pw, pool_scale[l], n_batch, ctx_len, ctx_blk0)
            a_br = jnp.concatenate([a_br, a_ctx], axis=0)
            b_br = jnp.concatenate([b_br, b_ctx], axis=0)
            c_br = jnp.concatenate([c_br, c_ctx_br], axis=0)

        merged = merge_branches(a_br, b_br, c_br, wl["w_br_a"], wl["w_br_b"], wl["w_br_c"], gates)
        stream = matmul_gated_residual(merged, wl["w_out"], stream, mod3, 2, seq, n_batch)

        h2_t = ln_modulate(stream, norm2_g[l], mod3, 3, n_act, seq, n_batch, transpose=True)
        tables = peer_router(h2_t, wl["wq_t"], wl["k1"], wl["k2"])
        y_t = peer_mixture(h2_t, wl["u"], wl["v"], tables)
        stream = residual_from_transposed(stream, y_t, mod3, 5, n_act, seq, n_batch)

    return final_norm(stream, final_g, n_lat).reshape(n_batch, seq, d)
```

```python
import functools

import numpy as np
import jax
import jax.numpy as jnp
from jax import lax
from jax.experimental import pallas as pl
from jax.experimental.pallas import tpu as pltpu

F32 = jnp.float32
BF16 = jnp.bfloat16

GRID_W = 64
EPS = 1e-6
ROPE_THETA = 10000.0

A_HEADS = 16
A_NOPE = 128
A_ROPE = 64
A_VDIM = 128
A_QLORA = 768
A_KVLORA = 256
A_HEAD_PAD = 256

B_HEADS = 8
B_DH = 128
B_WIDTH = B_HEADS * B_DH
WIN_H = 8
WIN_W = 16

C_WIDTH = 1024
POOL_SIZES = (2, 4, 8, 16)
C_GW = C_WIDTH // len(POOL_SIZES)
POOL_HALO = 8

P_HEADS = 8
P_NKEYS = 128
P_DK = 256
P_TOPK = 16

LANES = 128
SUBLANES = 8
SMALL_W = A_QLORA + A_KVLORA + LANES

NEG_BIG = -1e30
MIB = 1024 * 1024


def _cparams(n_axes, vmem_mib):
    return pltpu.CompilerParams(dimension_semantics=("arbitrary",) * n_axes,
                                vmem_limit_bytes=vmem_mib * MIB)


def _segment(row_tile, tile_rows, seq, n_batch):
    return jnp.minimum((row_tile * tile_rows) // seq, n_batch)


def _ada_kernel(c_ref, w_ref, b_ref, o_ref):
    c = c_ref[...]
    a = (c * jax.nn.sigmoid(c)).astype(BF16)
    o_ref[0] = jnp.dot(a, w_ref[0].astype(BF16), preferred_element_type=F32) + b_ref[0]


def ada_modulation(cvec, w_ada, b_ada):
    n_layers, d, n_out = w_ada.shape
    rows = cvec.shape[0]
    tn = 512
    return pl.pallas_call(
        _ada_kernel,
        grid=(n_layers, n_out // tn),
        in_specs=[pl.BlockSpec((rows, d), lambda l, j: (0, 0)),
                  pl.BlockSpec((1, d, tn), lambda l, j: (l, 0, j)),
                  pl.BlockSpec((1, 1, tn), lambda l, j: (l, 0, j))],
        out_specs=pl.BlockSpec((1, rows, tn), lambda l, j: (l, 0, j)),
        out_shape=jax.ShapeDtypeStruct((n_layers, rows, n_out), F32),
        compiler_params=_cparams(2, 40),
        name="ada_modulation",
    )(cvec, w_ada, b_ada.reshape(n_layers, 1, n_out))


def _rms(x, g):
    return x * lax.rsqrt(jnp.mean(x * x, axis=-1, keepdims=True) + EPS) * g


def _ln_mod_kernel(x_ref, g_ref, sc_ref, sh_ref, o_ref, *, transpose):
    h = _rms(x_ref[...], g_ref[...]) * (1.0 + sc_ref[0]) + sh_ref[0]
    if transpose:
        o_ref[...] = h.T.astype(o_ref.dtype)
    else:
        o_ref[...] = h.astype(o_ref.dtype)


def ln_modulate(x, g, mod3, shift_chunk, n_rows, seq, n_batch, transpose=False):
    d = x.shape[1]
    tm = 256
    seg = functools.partial(_segment, tile_rows=tm, seq=seq, n_batch=n_batch)
    if transpose:
        out_spec = pl.BlockSpec((d, tm), lambda i: (0, i))
        out_shape = jax.ShapeDtypeStruct((d, n_rows), BF16)
    else:
        out_spec = pl.BlockSpec((tm, d), lambda i: (i, 0))
        out_shape = jax.ShapeDtypeStruct((n_rows, d), BF16)
    return pl.pallas_call(
        functools.partial(_ln_mod_kernel, transpose=transpose),
        grid=(n_rows // tm,),
        in_specs=[pl.BlockSpec((tm, d), lambda i: (i, 0)),
                  pl.BlockSpec((1, d), lambda i: (0, 0)),
                  pl.BlockSpec((1, 1, d), lambda i: (seg(i), 0, shift_chunk + 1)),
                  pl.BlockSpec((1, 1, d), lambda i: (seg(i), 0, shift_chunk))],
        out_specs=out_spec,
        out_shape=out_shape,
        compiler_params=_cparams(1, 48),
        name="ln_modulate_t" if transpose else "ln_modulate",
    )(x, g.reshape(1, d), mod3, mod3)


def _mm_kernel(a_ref, b_ref, o_ref, *, act):
    acc = jnp.dot(a_ref[...], b_ref[...], preferred_element_type=F32)
    if act == "sigmoid":
        acc = jax.nn.sigmoid(acc)
    o_ref[...] = acc.astype(o_ref.dtype)


def matmul(a, b, out_dtype, act=None, n_rows=None, tn=1024, tm=512):
    m, k = a.shape
    m = n_rows or m
    n = b.shape[1]
    if n % tn:
        tn = n
    return pl.pallas_call(
        functools.partial(_mm_kernel, act=act),
        grid=(n // tn, m // tm),
        in_specs=[pl.BlockSpec((tm, k), lambda j, i: (i, 0)),
                  pl.BlockSpec((k, tn), lambda j, i: (0, j))],
        out_specs=pl.BlockSpec((tm, tn), lambda j, i: (i, j)),
        out_shape=jax.ShapeDtypeStruct((m, n), out_dtype),
        compiler_params=_cparams(2, 52),
        name="matmul_" + (act or "plain"),
    )(a, b)


def _mm_residual_kernel(a_ref, b_ref, x_ref, g_ref, o_ref):
    acc = jnp.dot(a_ref[...], b_ref[...], preferred_element_type=F32)
    o_ref[...] = x_ref[...] + g_ref[0] * acc


def matmul_gated_residual(a, b, x, mod3, gate_chunk, seq, n_batch, tn=1024, tm=512):
    m, k = a.shape
    n = b.shape[1]
    seg = functools.partial(_segment, tile_rows=tm, seq=seq, n_batch=n_batch)
    nj = n // tn
    return pl.pallas_call(
        _mm_residual_kernel,
        grid=(nj, m // tm),
        in_specs=[pl.BlockSpec((tm, k), lambda j, i: (i, 0)),
                  pl.BlockSpec((k, tn), lambda j, i: (0, j)),
                  pl.BlockSpec((tm, tn), lambda j, i: (i, j)),
                  pl.BlockSpec((1, 1, tn), lambda j, i: (seg(i), 0, gate_chunk * nj + j))],
        out_specs=pl.BlockSpec((tm, tn), lambda j, i: (i, j)),
        out_shape=jax.ShapeDtypeStruct((m, n), F32),
        compiler_params=_cparams(2, 52),
        name="matmul_gated_residual",
    )(a, b, x, mod3)


def _merge_kernel(a_ref, b_ref, c_ref, wa_ref, wb_ref, wc_ref, ga_ref, gb_ref, gc_ref, o_ref):
    m = ga_ref[...] * jnp.dot(a_ref[...], wa_ref[...], preferred_element_type=F32)
    m = m + gb_ref[...] * jnp.dot(b_ref[...], wb_ref[...], preferred_element_type=F32)
    m = m + gc_ref[...] * jnp.dot(c_ref[...], wc_ref[...], preferred_element_type=F32)
    o_ref[...] = m.astype(o_ref.dtype)


def merge_branches(a, b, c, wa, wb, wc, gates, tn=1024, tm=512):
    m = a.shape[0]
    n = wa.shape[1]
    nj = n // tn
    row = lambda j, i: (i, 0)
    col = lambda j, i: (0, j)
    return pl.pallas_call(
        _merge_kernel,
        grid=(nj, m // tm),
        in_specs=[pl.BlockSpec((tm, a.shape[1]), row),
                  pl.BlockSpec((tm, b.shape[1]), row),
                  pl.BlockSpec((tm, c.shape[1]), row),
                  pl.BlockSpec((wa.shape[0], tn), col),
                  pl.BlockSpec((wb.shape[0], tn), col),
                  pl.BlockSpec((wc.shape[0], tn), col),
                  pl.BlockSpec((tm, tn), lambda j, i: (i, j)),
                  pl.BlockSpec((tm, tn), lambda j, i: (i, nj + j)),
                  pl.BlockSpec((tm, tn), lambda j, i: (i, 2 * nj + j))],
        out_specs=pl.BlockSpec((tm, tn), lambda j, i: (i, j)),
        out_shape=jax.ShapeDtypeStruct((m, n), BF16),
        compiler_params=_cparams(2, 52),
        name="merge_branches",
    )(a, b, c, wa, wb, wc, gates, gates, gates)


def _rotate(v, c_ref, s1_ref, s2_ref):
    return (v * c_ref[...] + pltpu.roll(v, LANES - A_ROPE // 2, 1) * s1_ref[...]
            + pltpu.roll(v, A_ROPE // 2, 1) * s2_ref[...])


def _mla_q_kernel(cq_ref, g_ref, w_ref, c_ref, s1_ref, s2_ref, o_ref):
    y = _rms(cq_ref[...], g_ref[...]).astype(BF16)
    acc = jnp.dot(y, w_ref[...], preferred_element_type=F32)
    for h in range(A_HEADS):
        lo = h * A_HEAD_PAD
        o_ref[:, lo:lo + A_NOPE] = acc[:, lo:lo + A_NOPE].astype(o_ref.dtype)
        rot = _rotate(acc[:, lo + A_NOPE:lo + A_HEAD_PAD], c_ref, s1_ref, s2_ref)
        o_ref[:, lo + A_NOPE:lo + A_HEAD_PAD] = rot.astype(o_ref.dtype)


def mla_q(p_small, g, w_uq_pad, rope_c, rope_s1, rope_s2):
    nt = p_small.shape[0]
    tm = 256
    width = A_HEADS * A_HEAD_PAD
    tab = pl.BlockSpec((tm, LANES), lambda i: (i, 0))
    return pl.pallas_call(
        _mla_q_kernel,
        grid=(nt // tm,),
        in_specs=[pl.BlockSpec((tm, A_QLORA), lambda i: (i, 0)),
                  pl.BlockSpec((1, A_QLORA), lambda i: (0, 0)),
                  pl.BlockSpec((A_QLORA, width), lambda i: (0, 0)),
                  tab, tab, tab],
        out_specs=pl.BlockSpec((tm, width), lambda i: (i, 0)),
        out_shape=jax.ShapeDtypeStruct((nt, width), BF16),
        compiler_params=_cparams(1, 48),
        name="mla_q",
    )(p_small, g.reshape(1, A_QLORA), w_uq_pad, rope_c, rope_s1, rope_s2)


def _mla_kv_kernel(ckv_ref, kr_ref, g_ref, w_ref, c_ref, s1_ref, s2_ref, k_ref, v_ref):
    y = _rms(ckv_ref[...], g_ref[...]).astype(BF16)
    acc = jnp.dot(y, w_ref[...], preferred_element_type=F32)
    kr = _rotate(kr_ref[...], c_ref, s1_ref, s2_ref).astype(k_ref.dtype)
    for h in range(A_HEADS):
        lo = h * A_HEAD_PAD
        k_ref[:, lo:lo + A_NOPE] = acc[:, h * A_NOPE:(h + 1) * A_NOPE].astype(k_ref.dtype)
        k_ref[:, lo + A_NOPE:lo + A_HEAD_PAD] = kr
    v_ref[...] = acc[:, A_HEADS * A_NOPE:].astype(v_ref.dtype)


def mla_kv(p_small, g, w_ukv_perm, rope_c, rope_s1, rope_s2):
    nt = p_small.shape[0]
    tm = 256
    kw = A_HEADS * A_HEAD_PAD
    vw = A_HEADS * A_VDIM
    tab = pl.BlockSpec((tm, LANES), lambda i: (i, 0))
    return pl.pallas_call(
        _mla_kv_kernel,
        grid=(nt // tm,),
        in_specs=[pl.BlockSpec((tm, A_KVLORA), lambda i: (i, A_QLORA // A_KVLORA)),
                  pl.BlockSpec((tm, LANES), lambda i: (i, (A_QLORA + A_KVLORA) // LANES)),
                  pl.BlockSpec((1, A_KVLORA), lambda i: (0, 0)),
                  pl.BlockSpec((A_KVLORA, A_HEADS * (A_NOPE + A_VDIM)), lambda i: (0, 0)),
                  tab, tab, tab],
        out_specs=[pl.BlockSpec((tm, kw), lambda i: (i, 0)),
                   pl.BlockSpec((tm, vw), lambda i: (i, 0))],
        out_shape=[jax.ShapeDtypeStruct((nt, kw), BF16),
                   jax.ShapeDtypeStruct((nt, vw), BF16)],
        compiler_params=_cparams(1, 48),
        name="mla_kv",
    )(p_small, p_small, g.reshape(1, A_KVLORA), w_ukv_perm, rope_c, rope_s1, rope_s2)


def _qk(q, k):
    return lax.dot_general(q, k, (((1,), (1,)), ((), ())), preferred_element_type=F32)


LOG2_E = 1.4426950408889634


def _flash_two_sets_kernel(q_ref, kl_ref, vl_ref, kc_ref, vc_ref, o_ref, *, scale, chunk):
    q = q_ref[...]
    c = scale * LOG2_E
    n_chunks = kl_ref.shape[0] // chunk
    keys = lambda j: kl_ref[j * chunk:(j + 1) * chunk, :]
    t = _qk(q, kc_ref[...]) * c
    t_next = _qk(q, keys(0)) * c
    m = jnp.max(t, axis=-1, keepdims=True)
    p = jnp.exp2(t - m)
    den = jnp.sum(p, axis=-1, keepdims=True)
    acc = jnp.dot(p.astype(BF16), vc_ref[...], preferred_element_type=F32)
    for j in range(n_chunks):
        t = t_next
        if j + 1 < n_chunks:
            t_next = _qk(q, keys(j + 1)) * c
        m_new = jnp.maximum(m, jnp.max(t, axis=-1, keepdims=True))
        alpha = jnp.exp2(m - m_new)
        p = jnp.exp2(t - m_new)
        den = alpha * den + jnp.sum(p, axis=-1, keepdims=True)
        acc = alpha * acc + jnp.dot(p.astype(BF16), vl_ref[j * chunk:(j + 1) * chunk, :],
                                    preferred_element_type=F32)
        m = m_new
    o_ref[...] = (acc / den).astype(o_ref.dtype)


def mla_latent_attention(q, k, v, n_batch, seq, ctx_len, tq=1024, chunk=512):
    ctx_blk0 = (n_batch * seq) // ctx_len
    nq = seq // tq
    scale = float((A_NOPE + A_ROPE) ** -0.5)
    return pl.pallas_call(
        functools.partial(_flash_two_sets_kernel, scale=scale, chunk=chunk),
        grid=(n_batch, A_HEADS, nq),
        in_specs=[pl.BlockSpec((tq, A_HEAD_PAD), lambda b, h, i: (b * nq + i, h)),
                  pl.BlockSpec((seq, A_HEAD_PAD), lambda b, h, i: (b, h)),
                  pl.BlockSpec((seq, A_VDIM), lambda b, h, i: (b, h)),
                  pl.BlockSpec((ctx_len, A_HEAD_PAD), lambda b, h, i: (ctx_blk0 + b, h)),
                  pl.BlockSpec((ctx_len, A_VDIM), lambda b, h, i: (ctx_blk0 + b, h))],
        out_specs=pl.BlockSpec((tq, A_VDIM), lambda b, h, i: (b * nq + i, h)),
        out_shape=jax.ShapeDtypeStruct((n_batch * seq, A_HEADS * A_VDIM), BF16),
        compiler_params=_cparams(3, 48),
        name="mla_latent_attention",
    )(q, k, v, k, v)


def _attn_one_set_kernel(q_ref, k_ref, v_ref, o_ref, *, scale):
    s = _qk(q_ref[...], k_ref[...]) * scale
    p = jnp.exp(s - jnp.max(s, axis=-1, keepdims=True))
    den = jnp.sum(p, axis=-1, keepdims=True)
    o = jnp.dot(p.astype(BF16), v_ref[...], preferred_element_type=F32)
    o_ref[...] = (o / den).astype(o_ref.dtype)


def context_attention(q, k, v, n_batch, seq, ctx_len, n_heads, dqk, dv, q_col0, k_col0, v_col0, scale):
    ctx_blk0 = (n_batch * seq) // ctx_len
    return pl.pallas_call(
        functools.partial(_attn_one_set_kernel, scale=scale),
        grid=(n_batch, n_heads),
        in_specs=[pl.BlockSpec((ctx_len, dqk), lambda b, h: (ctx_blk0 + b, q_col0 + h)),
                  pl.BlockSpec((ctx_len, dqk), lambda b, h: (ctx_blk0 + b, k_col0 + h)),
                  pl.BlockSpec((ctx_len, dv), lambda b, h: (ctx_blk0 + b, v_col0 + h))],
        out_specs=pl.BlockSpec((ctx_len, dv), lambda b, h: (b, h)),
        out_shape=jax.ShapeDtypeStruct((n_batch * ctx_len, n_heads * dv), BF16),
        compiler_params=_cparams(2, 32),
        name="context_attention",
    )(q, k, v)


def _na_kernel(q_ref, kl_ref, vl_ref, kc_ref, vc_ref, bias_ref, o_ref, *, rows, scale):
    r = pl.program_id(1)
    r0 = jnp.clip(r - WIN_H // 2, 0, rows - WIN_H)
    start = pl.multiple_of(r0 * GRID_W, GRID_W)
    win = WIN_H * GRID_W
    outs = []
    for h in range(B_HEADS):
        cols = slice(h * B_DH, (h + 1) * B_DH)
        q = q_ref[:, cols]
        s_l = _qk(q, kl_ref[pl.ds(start, win), cols]) * scale + bias_ref[0, h]
        s_c = _qk(q, kc_ref[:, cols]) * scale
        m = jnp.maximum(jnp.max(s_l, axis=-1, keepdims=True), jnp.max(s_c, axis=-1, keepdims=True))
        p_l = jnp.exp(s_l - m)
        p_c = jnp.exp(s_c - m)
        den = jnp.sum(p_l, axis=-1, keepdims=True) + jnp.sum(p_c, axis=-1, keepdims=True)
        o = jnp.dot(p_l.astype(BF16), vl_ref[pl.ds(start, win), cols], preferred_element_type=F32)
        o = o + jnp.dot(p_c.astype(BF16), vc_ref[:, cols], preferred_element_type=F32)
        outs.append(o / den)
    o_ref[...] = jnp.concatenate(outs, axis=1).astype(o_ref.dtype)


def na_latent_attention(qkv, bias, n_batch, seq, ctx_len):
    rows = seq // GRID_W
    ctx_blk0 = (n_batch * seq) // ctx_len
    win = WIN_H * GRID_W

    def bias_idx(b, r):
        return (r - jnp.clip(r - WIN_H // 2, 0, rows - WIN_H), 0, 0, 0)

    once = pl.Buffered(1)
    return pl.pallas_call(
        functools.partial(_na_kernel, rows=rows, scale=float(B_DH ** -0.5)),
        grid=(n_batch, rows),
        in_specs=[pl.BlockSpec((GRID_W, B_WIDTH), lambda b, r: (b * rows + r, 0)),
                  pl.BlockSpec((seq, B_WIDTH), lambda b, r: (b, 1), pipeline_mode=once),
                  pl.BlockSpec((seq, B_WIDTH), lambda b, r: (b, 2), pipeline_mode=once),
                  pl.BlockSpec((ctx_len, B_WIDTH), lambda b, r: (ctx_blk0 + b, 1)),
                  pl.BlockSpec((ctx_len, B_WIDTH), lambda b, r: (ctx_blk0 + b, 2)),
                  pl.BlockSpec((1, B_HEADS, GRID_W, win), bias_idx)],
        out_specs=pl.BlockSpec((GRID_W, B_WIDTH), lambda b, r: (b * rows + r, 0)),
        out_shape=jax.ShapeDtypeStruct((n_batch * seq, B_WIDTH), BF16),
        compiler_params=_cparams(2, 40),
        name="na_latent_attention",
    )(qkv, qkv, qkv, qkv, qkv, bias)


def na_bias_table(rel_bias):
    rel = rel_bias.astype(F32)
    n_layers = rel.shape[0]
    by_row = jnp.stack([rel[:, :, WIN_H - 1 - d:2 * WIN_H - 1 - d, :] for d in range(WIN_H)], axis=2)
    pad = GRID_W - 1
    by_row = jnp.pad(by_row, ((0, 0), (0, 0), (0, 0), (0, 0), (pad, pad)))
    off = WIN_W - 1 + pad
    tab = jnp.stack([by_row[..., off - c:off - c + GRID_W] for c in range(GRID_W)], axis=4)
    c = jnp.arange(GRID_W)[:, None]
    kc = jnp.arange(GRID_W)[None, :]
    c0 = jnp.clip(c - WIN_W // 2, 0, GRID_W - WIN_W)
    inside = (kc >= c0) & (kc < c0 + WIN_W)
    tab = jnp.where(inside[None, None, None, None], tab, NEG_BIG)
    return tab.transpose(0, 2, 1, 4, 3, 5).reshape(n_layers, WIN_H, B_HEADS, GRID_W, WIN_H * GRID_W)


def _pool_kernel(x_ref, w_ref, sc_ref, o_ref, pad_ref, *, seq_len, chunk):
    g = pl.program_id(1)
    zeros = jnp.zeros((POOL_HALO, C_GW), F32)
    pad_ref[0:POOL_HALO, :] = zeros
    pad_ref[POOL_HALO + seq_len:2 * POOL_HALO + seq_len, :] = zeros
    pad_ref[POOL_HALO:POOL_HALO + seq_len, :] = x_ref[...]
    for gi, w in enumerate(POOL_SIZES):
        @pl.when(g == gi)
        def _(w=w):
            for c0 in range(0, seq_len, chunk):
                acc = None
                for k in range(-(w // 2), w - w // 2):
                    piece = pad_ref[POOL_HALO + c0 + k:POOL_HALO + c0 + k + chunk, :]
                    acc = piece if acc is None else acc + piece
                t = c0 + lax.broadcasted_iota(jnp.int32, (chunk, C_GW), 0)
                cnt = jnp.minimum(t + (w - w // 2), seq_len) - jnp.maximum(t - w // 2, 0)
                d = (acc / cnt.astype(F32) - x_ref[c0:c0 + chunk, :]).astype(BF16)
                y = jnp.dot(d, w_ref[0], preferred_element_type=F32) * sc_ref[...]
                o_ref[c0:c0 + chunk, :] = y.astype(o_ref.dtype)


def pool_mix(p_pool, pool_w, pool_scale, n_seq, seq_len, first_block):
    chunk = min(seq_len, 512)
    return pl.pallas_call(
        functools.partial(_pool_kernel, seq_len=seq_len, chunk=chunk),
        grid=(n_seq, len(POOL_SIZES)),
        in_specs=[pl.BlockSpec((seq_len, C_GW), lambda b, g: (first_block + b, g)),
                  pl.BlockSpec((1, C_GW, C_GW), lambda b, g: (g, 0, 0)),
                  pl.BlockSpec((1, C_GW), lambda b, g: (0, g))],
        out_specs=pl.BlockSpec((seq_len, C_GW), lambda b, g: (b, g)),
        out_shape=jax.ShapeDtypeStruct((n_seq * seq_len, C_WIDTH), BF16),
        scratch_shapes=[pltpu.VMEM((seq_len + 2 * POOL_HALO, C_GW), F32)],
        compiler_params=_cparams(2, 48),
        name="pool_mix",
    )(p_pool, pool_w, pool_scale.reshape(1, C_WIDTH))


_REGION = [(r1, r2) for r1 in range(P_TOPK) for r2 in range(P_TOPK) if (r1 + 1) * (r2 + 1) <= P_TOPK]


def _dominates(a, b):
    return a != b and a[0] <= b[0] and a[1] <= b[1]


def _router_kernel(h_ref, wq_ref, k1_ref, k2_ref, cnt_ref, a_ref, rk_ref, b_ref,
                   s_scr, v1_scr, i1_scr, v2_scr, i2_scr, cr_scr, e1_scr, e2_scr):
    tm = h_ref.shape[1]
    half = P_HEADS * P_DK // 2
    q_t = jnp.dot(wq_ref[...], h_ref[...], preferred_element_type=F32)
    key = lax.broadcasted_iota(jnp.int32, (P_NKEYS, P_HEADS, tm), 0)

    def top_sorted(k_ref, q_half, v_scr, i_scr):
        s_scr[...] = jnp.dot(k_ref[...], q_half.astype(BF16),
                             preferred_element_type=F32).reshape(P_NKEYS, P_HEADS, tm)

        def body(r, carry):
            s = s_scr[...]
            m = jnp.max(s, axis=0)
            idx = jnp.min(jnp.where(s == m[None], key, P_NKEYS), axis=0)
            s_scr[...] = jnp.where(key == idx[None], -jnp.inf, s)
            v_scr[r] = m
            i_scr[r] = idx
            return carry

        lax.fori_loop(0, P_TOPK, body, 0)

    top_sorted(k1_ref, q_t[:half], v1_scr, i1_scr)
    top_sorted(k2_ref, q_t[half:], v2_scr, i2_scr)

    v1 = [v1_scr[r] for r in range(P_TOPK)]
    v2 = [v2_scr[r] for r in range(P_TOPK)]
    sums = {c: v1[c[0]] + v2[c[1]] for c in _REGION}
    beaten = {c: float(sum(_dominates(o, c) for o in _REGION)) for c in _REGION}
    for ci, c in enumerate(_REGION):
        for o in _REGION[:ci]:
            if _dominates(o, c):
                continue
            o_first = (sums[o] >= sums[c]).astype(F32)
            beaten[c] = beaten[c] + o_first
            beaten[o] = beaten[o] + (1.0 - o_first)
    e1 = [jnp.exp(v1[r] - v1[0]) for r in range(P_TOPK)]
    e2 = [jnp.exp(v2[r] - v2[0]) for r in range(P_TOPK)]
    z = jnp.zeros_like(v1[0])
    counts = [jnp.zeros_like(v1[0]) for _ in range(P_TOPK)]
    for c in _REGION:
        chosen = beaten[c] < float(P_TOPK)
        z = z + jnp.where(chosen, e1[c[0]] * e2[c[1]], 0.0)
        counts[c[0]] = counts[c[0]] + jnp.where(chosen, 1.0, 0.0)
    inv_z = 1.0 / z
    for r in range(P_TOPK):
        cr_scr[r] = counts[r]
        e1_scr[r] = e1[r]
        e2_scr[r] = e2[r] * inv_z

    key2 = lax.broadcasted_iota(jnp.int32, (P_NKEYS, LANES), 0)

    def expand(n, carry, h):
        t0 = pl.multiple_of(n * LANES, LANES)
        row = lambda ref, r: ref[r, h:h + 1, pl.ds(t0, LANES)]
        cnt = jnp.zeros((P_NKEYS, LANES), F32)
        a = cnt
        for r in range(P_TOPK):
            hit = key2 == row(i1_scr, r)
            cnt = jnp.where(hit, row(cr_scr, r), cnt)
            a = jnp.where(hit, row(e1_scr, r), a)
        cnt_ref[h, :, pl.ds(t0, LANES)] = cnt
        a_ref[h, :, pl.ds(t0, LANES)] = a
        rk = jnp.full((P_NKEYS, LANES), float(P_TOPK), F32)
        b = jnp.zeros((P_NKEYS, LANES), F32)
        for r in range(P_TOPK):
            hit = key2 == row(i2_scr, r)
            rk = jnp.where(hit, float(r), rk)
            b = jnp.where(hit, row(e2_scr, r), b)
        rk_ref[h, :, pl.ds(t0, LANES)] = rk.astype(rk_ref.dtype)
        b_ref[h, :, pl.ds(t0, LANES)] = b.astype(b_ref.dtype)
        return carry

    for h in range(P_HEADS):
        lax.fori_loop(0, tm // LANES, functools.partial(expand, h=h), 0)


def peer_router(h_t, wq_t, k1_packed, k2_packed, tm=256):
    d, n = h_t.shape
    qw = wq_t.shape[0]
    kp = k1_packed.shape[0]
    tab = lambda dt: jax.ShapeDtypeStruct((P_HEADS, P_NKEYS, n), dt)
    tab_spec = pl.BlockSpec((P_HEADS, P_NKEYS, tm), lambda i: (0, 0, i))
    small = lambda dt: pltpu.VMEM((P_TOPK, P_HEADS, tm), dt)
    return pl.pallas_call(
        _router_kernel,
        grid=(n // tm,),
        in_specs=[pl.BlockSpec((d, tm), lambda i: (0, i)),
                  pl.BlockSpec((qw, d), lambda i: (0, 0), pipeline_mode=pl.Buffered(1)),
                  pl.BlockSpec((kp, kp), lambda i: (0, 0), pipeline_mode=pl.Buffered(1)),
                  pl.BlockSpec((kp, kp), lambda i: (0, 0), pipeline_mode=pl.Buffered(1))],
        out_specs=[tab_spec] * 4,
        out_shape=[tab(F32)] * 4,
        scratch_shapes=[pltpu.VMEM((P_NKEYS, P_HEADS, tm), F32),
                        small(F32), small(jnp.int32), small(F32), small(jnp.int32),
                        small(F32), small(F32), small(F32)],
        compiler_params=_cparams(1, 52),
        name="peer_router",
    )(h_t, wq_t, k1_packed, k2_packed)


def _gelu_exact(x):
    return 0.5 * x * (1.0 + lax.erf(x * float(np.sqrt(0.5))))


def _peer_kernel(h_ref, u_ref, v_ref, cnt_ref, a_ref, rk_ref, b_ref, o_ref):
    e = pl.program_id(1)
    te = u_ref.shape[0]
    tm = h_ref.shape[1]

    def step(first):
        halves = [(c0, jnp.dot(u_ref[c0:c0 + te // 2, :], h_ref[...], preferred_element_type=F32))
                  for c0 in (0, te // 2)]
        rows_per_half = te // 2 // P_NKEYS
        n_slab = P_NKEYS // SUBLANES
        ga_rows = []
        for c0, pre in halves:
            i0 = e * (te // P_NKEYS) + c0 // P_NKEYS
            rows = [[(cnt_ref[h, pl.ds(i0 + k, 1), :], a_ref[h, pl.ds(i0 + k, 1), :]) for h in range(P_HEADS)]
                    for k in range(rows_per_half)]
            tiles = {}
            for t0 in range(0, tm, LANES):
                lanes = slice(t0, t0 + LANES)
                g = [[None] * n_slab for _ in range(rows_per_half)]
                for h in range(P_HEADS):
                    per_row = [[jnp.broadcast_to(x[:, lanes], (SUBLANES, LANES)) for x in rows[k][h]]
                               for k in range(rows_per_half)]
                    for s in range(n_slab):
                        slab = slice(s * SUBLANES, (s + 1) * SUBLANES)
                        rk = rk_ref[h, slab, lanes]
                        b = b_ref[h, slab, lanes]
                        for k, (cnt, a) in enumerate(per_row):
                            term = jnp.where(rk < cnt, b * a, jnp.zeros_like(b))
                            g[k][s] = term if g[k][s] is None else g[k][s] + term
                for k in range(rows_per_half):
                    for s in range(n_slab):
                        r = k * P_NKEYS + s * SUBLANES
                        tiles[(k, s, t0)] = g[k][s] * _gelu_exact(pre[r:r + SUBLANES, lanes])
            for k in range(rows_per_half):
                for s in range(n_slab):
                    ga_rows.append(jnp.concatenate([tiles[(k, s, t0)] for t0 in range(0, tm, LANES)], axis=1))
        ga = jnp.concatenate(ga_rows, axis=0).astype(BF16)
        contrib = lax.dot_general(v_ref[...], ga, (((0,), (0,)), ((), ())), preferred_element_type=F32)
        if first:
            o_ref[...] = contrib
        else:
            o_ref[...] += contrib

    pl.when(e == 0)(functools.partial(step, True))
    pl.when(e > 0)(functools.partial(step, False))


def peer_mixture(h_t, u, v, tables, tm=512, te=512):
    d, n = h_t.shape
    tab_spec = pl.BlockSpec((P_HEADS, P_NKEYS, tm), lambda i, e: (0, 0, i), pipeline_mode=pl.Buffered(1))
    return pl.pallas_call(
        _peer_kernel,
        grid=(n // tm, u.shape[0] // te),
        in_specs=[pl.BlockSpec((d, tm), lambda i, e: (0, i), pipeline_mode=pl.Buffered(1)),
                  pl.BlockSpec((te, d), lambda i, e: (e, 0)),
                  pl.BlockSpec((te, d), lambda i, e: (e, 0))] + [tab_spec] * 4,
        out_specs=pl.BlockSpec((d, tm), lambda i, e: (0, i)),
        out_shape=jax.ShapeDtypeStruct((d, n), F32),
        compiler_params=_cparams(2, 56),
        name="peer_mixture",
    )(h_t, u, v, *tables)


def _residual_t_ln_kernel(x_ref, y_ref, g_ref, ng_ref, sc_ref, sh_ref, o_ref, h_ref):
    x = x_ref[...] + g_ref[0] * y_ref[...].T
    o_ref[...] = x
    h_ref[...] = (_rms(x, ng_ref[...]) * (1.0 + sc_ref[0]) + sh_ref[0]).astype(h_ref.dtype)


def _residual_t_final_kernel(x_ref, y_ref, g_ref, ng_ref, o_ref):
    o_ref[...] = _rms(x_ref[...] + g_ref[0] * y_ref[...].T, ng_ref[...])


def residual_from_transposed(x, y_t, mod3, gate_chunk, n_rows, seq, n_batch, norm_g, next_mod3=None):
    d = x.shape[1]
    tm = 256
    seg = functools.partial(_segment, tile_rows=tm, seq=seq, n_batch=n_batch)
    row = pl.BlockSpec((tm, d), lambda i: (i, 0))
    in_specs = [row,
                pl.BlockSpec((d, tm), lambda i: (0, i)),
                pl.BlockSpec((1, 1, d), lambda i: (seg(i), 0, gate_chunk)),
                pl.BlockSpec((1, d), lambda i: (0, 0))]
    if next_mod3 is None:
        return pl.pallas_call(
            _residual_t_final_kernel,
            grid=(n_rows // tm,),
            in_specs=in_specs,
            out_specs=row,
            out_shape=jax.ShapeDtypeStruct((n_rows, d), F32),
            compiler_params=_cparams(1, 48),
            name="residual_final_norm",
        )(x, y_t, mod3, norm_g.reshape(1, d))
    return pl.pallas_call(
        _residual_t_ln_kernel,
        grid=(n_rows // tm,),
        in_specs=in_specs + [pl.BlockSpec((1, 1, d), lambda i: (seg(i), 0, 1)),
                             pl.BlockSpec((1, 1, d), lambda i: (seg(i), 0, 0))],
        out_specs=[row, row],
        out_shape=[jax.ShapeDtypeStruct((n_rows, d), F32), jax.ShapeDtypeStruct((n_rows, d), BF16)],
        compiler_params=_cparams(1, 48),
        name="residual_next_norm",
    )(x, y_t, mod3, norm_g.reshape(1, d), next_mod3, next_mod3)


def _rope_tables(seq, n_batch, n_ctx_rows):
    t = jnp.arange(seq)
    row = (t // GRID_W).astype(F32)
    col = (t % GRID_W).astype(F32)
    n_freq = A_ROPE // 4
    freqs = ROPE_THETA ** (-jnp.arange(n_freq, dtype=F32) / n_freq)
    ang = jnp.concatenate([row[:, None] * freqs, col[:, None] * freqs], axis=-1)
    cos, sin = jnp.cos(ang), jnp.sin(ang)
    zero = jnp.zeros_like(cos)
    pad = jnp.zeros((seq, LANES - A_ROPE), F32)
    c = jnp.concatenate([cos, cos, pad], axis=1)
    s1 = jnp.concatenate([-sin, zero, pad], axis=1)
    s2 = jnp.concatenate([zero, sin, pad], axis=1)
    ident = jnp.concatenate([jnp.ones((n_ctx_rows, A_ROPE), F32), jnp.zeros((n_ctx_rows, LANES - A_ROPE), F32)], 1)
    none = jnp.zeros((n_ctx_rows, LANES), F32)
    tile = lambda a, ctx: jnp.concatenate([jnp.tile(a, (n_batch, 1)), ctx], axis=0)
    return tile(c, ident), tile(s1, none), tile(s2, none)


def _pack_keys(k):
    h, nk, dk = k.shape
    eye = jnp.eye(h, dtype=k.dtype)
    blk = k.transpose(1, 0, 2)[:, :, None, :] * eye[None, :, :, None]
    return blk.reshape(nk * h, h * dk).astype(BF16)


def _prepare_layer(w_in, w_uq, w_ukv, w_br_a, w_br_b, w_br_c, w_out, peer_wq, peer_k1, peer_k2, peer_u, peer_v):
    d = w_in.shape[0]
    o_kr = A_QLORA + A_KVLORA
    o_q = o_kr + A_ROPE
    o_pool = o_q + 3 * B_WIDTH
    o_gate = o_pool + C_WIDTH
    w_small = jnp.concatenate([w_in[:, :o_q], jnp.zeros((d, LANES - A_ROPE), w_in.dtype)], axis=1)
    w_uq_pad = jnp.pad(w_uq.reshape(A_QLORA, A_HEADS, A_NOPE + A_ROPE),
                       ((0, 0), (0, 0), (0, A_HEAD_PAD - A_NOPE - A_ROPE))).reshape(A_QLORA, A_HEADS * A_HEAD_PAD)
    w_ukv_perm = w_ukv.reshape(A_KVLORA, A_HEADS, 2, A_NOPE).transpose(0, 2, 1, 3).reshape(A_KVLORA, -1)
    wq_t = peer_wq.reshape(d, P_HEADS, 2, P_DK // 2).transpose(2, 1, 3, 0).reshape(P_HEADS * P_DK, d)
    return dict(
        w_small=w_small.astype(BF16),
        w_qkv=w_in[:, o_q:o_pool].astype(BF16),
        w_pool=w_in[:, o_pool:o_gate].astype(BF16),
        w_gate=w_in[:, o_gate:].astype(BF16),
        w_uq=w_uq_pad.astype(BF16),
        w_ukv=w_ukv_perm.astype(BF16),
        w_br_a=w_br_a.astype(BF16), w_br_b=w_br_b.astype(BF16), w_br_c=w_br_c.astype(BF16),
        w_out=w_out.astype(BF16),
        wq_t=wq_t.astype(BF16),
        k1=_pack_keys(peer_k1), k2=_pack_keys(peer_k2),
        u=peer_u.astype(BF16), v=peer_v.astype(BF16),
    )


def kernel(x, c, ctx, c_ctx, w_ada, b_ada, norm1_g, norm2_g, w_in, q_norm_g, kv_norm_g, w_uq, w_ukv, na_rel_bias, pool_w, pool_scale, w_br_a, w_br_b, w_br_c, w_out, peer_wq, peer_k1, peer_k2, peer_u, peer_v, final_g):
    n_batch, seq, d = x.shape
    ctx_len = ctx.shape[1]
    depth = w_ada.shape[0]
    n_lat = n_batch * seq
    n_ctx = n_batch * ctx_len
    nt = n_lat + n_ctx
    ctx_blk0 = n_lat // ctx_len

    stream = jnp.concatenate([x.reshape(n_lat, d), ctx.reshape(n_ctx, d)], axis=0)
    mod_rows = 8
    cvec = jnp.concatenate([c, c_ctx[None], jnp.zeros((mod_rows - n_batch - 1, d), c.dtype)], axis=0)
    mod = ada_modulation(cvec, w_ada, b_ada)
    rope_c, rope_s1, rope_s2 = _rope_tables(seq, n_batch, n_ctx)
    na_bias = na_bias_table(na_rel_bias)
    mod3s = [mod[l].reshape(mod_rows, 1, 6 * d) for l in range(depth)]

    h = ln_modulate(stream, norm1_g[0], mod3s[0], 0, nt, seq, n_batch)
    for l in range(depth):
        ctx_out = l < depth - 1
        n_act = nt if ctx_out else n_lat
        wl = _prepare_layer(w_in[l], w_uq[l], w_ukv[l], w_br_a[l], w_br_b[l], w_br_c[l], w_out[l],
                            peer_wq[l], peer_k1[l], peer_k2[l], peer_u[l], peer_v[l])
        mod3 = mod3s[l]

        p_small = matmul(h, wl["w_small"], F32)
        qkv = matmul(h, wl["w_qkv"], BF16)
        p_pool = matmul(h, wl["w_pool"], F32)
        gates = matmul(h, wl["w_gate"], F32, act="sigmoid", n_rows=n_act)

        q = mla_q(p_small, q_norm_g[l], wl["w_uq"], rope_c, rope_s1, rope_s2)
        k, v = mla_kv(p_small, kv_norm_g[l], wl["w_ukv"], rope_c, rope_s1, rope_s2)
        a_br = mla_latent_attention(q, k, v, n_batch, seq, ctx_len)
        b_br = na_latent_attention(qkv, na_bias[l], n_batch, seq, ctx_len)
        pw = pool_w[l].astype(BF16)
        c_br = pool_mix(p_pool, pw, pool_scale[l], n_batch, seq, 0)
        if ctx_out:
            a_ctx = context_attention(q, k, v, n_batch, seq, ctx_len, A_HEADS, A_HEAD_PAD, A_VDIM, 0, 0, 0,
                                      float((A_NOPE + A_ROPE) ** -0.5))
            b_ctx = context_attention(qkv, qkv, qkv, n_batch, seq, ctx_len, B_HEADS, B_DH, B_DH,
                                      0, B_HEADS, 2 * B_HEADS, float(B_DH ** -0.5))
            c_ctx_br = pool_mix(p_pool, pw, pool_scale[l], n_batch, ctx_len, ctx_blk0)
            a_br = jnp.concatenate([a_br, a_ctx], axis=0)
            b_br = jnp.concatenate([b_br, b_ctx], axis=0)
            c_br = jnp.concatenate([c_br, c_ctx_br], axis=0)

        merged = merge_branches(a_br, b_br, c_br, wl["w_br_a"], wl["w_br_b"], wl["w_br_c"], gates)
        stream = matmul_gated_residual(merged, wl["w_out"], stream, mod3, 2, seq, n_batch)

        h2_t = ln_modulate(stream, norm2_g[l], mod3, 3, n_act, seq, n_batch, transpose=True)
        tables = peer_router(h2_t, wl["wq_t"], wl["k1"], wl["k2"])
        y_t = peer_mixture(h2_t, wl["u"], wl["v"], tables)
        if ctx_out:
            stream, h = residual_from_transposed(stream, y_t, mod3, 5, n_act, seq, n_batch,
                                                 norm1_g[l + 1], mod3s[l + 1])
        else:
            out = residual_from_transposed(stream, y_t, mod3, 5, n_act, seq, n_batch, final_g)

    return out.reshape(n_batch, seq, d)
```

```python
import functools

import numpy as np
import jax
import jax.numpy as jnp
from jax import lax
from jax.experimental import pallas as pl
from jax.experimental.pallas import tpu as pltpu

F32 = jnp.float32
BF16 = jnp.bfloat16

GRID_W = 64
EPS = 1e-6
ROPE_THETA = 10000.0

A_HEADS = 16
A_NOPE = 128
A_ROPE = 64
A_VDIM = 128
A_QLORA = 768
A_KVLORA = 256
A_HEAD_PAD = 256

B_HEADS = 8
B_DH = 128
B_WIDTH = B_HEADS * B_DH
WIN_H = 8
WIN_W = 16

C_WIDTH = 1024
POOL_SIZES = (2, 4, 8, 16)
C_GW = C_WIDTH // len(POOL_SIZES)
POOL_HALO = 8

P_HEADS = 8
P_NKEYS = 128
P_DK = 256
P_TOPK = 16

LANES = 128
SUBLANES = 8
SMALL_W = A_QLORA + A_KVLORA + LANES

NEG_BIG = -1e30
MIB = 1024 * 1024


def _cparams(n_axes, vmem_mib):
    return pltpu.CompilerParams(dimension_semantics=("arbitrary",) * n_axes,
                                vmem_limit_bytes=vmem_mib * MIB)


def _segment(row_tile, tile_rows, seq, n_batch):
    return jnp.minimum((row_tile * tile_rows) // seq, n_batch)


def _ada_kernel(c_ref, w_ref, b_ref, o_ref):
    c = c_ref[...]
    a = (c * jax.nn.sigmoid(c)).astype(BF16)
    o_ref[0] = jnp.dot(a, w_ref[0].astype(BF16), preferred_element_type=F32) + b_ref[0]


def ada_modulation(cvec, w_ada, b_ada):
    n_layers, d, n_out = w_ada.shape
    rows = cvec.shape[0]
    tn = 512
    return pl.pallas_call(
        _ada_kernel,
        grid=(n_layers, n_out // tn),
        in_specs=[pl.BlockSpec((rows, d), lambda l, j: (0, 0)),
                  pl.BlockSpec((1, d, tn), lambda l, j: (l, 0, j)),
                  pl.BlockSpec((1, 1, tn), lambda l, j: (l, 0, j))],
        out_specs=pl.BlockSpec((1, rows, tn), lambda l, j: (l, 0, j)),
        out_shape=jax.ShapeDtypeStruct((n_layers, rows, n_out), F32),
        compiler_params=_cparams(2, 40),
        name="ada_modulation",
    )(cvec, w_ada, b_ada.reshape(n_layers, 1, n_out))


def _rms(x, g):
    return x * lax.rsqrt(jnp.mean(x * x, axis=-1, keepdims=True) + EPS) * g


def _ln_mod_kernel(x_ref, g_ref, sc_ref, sh_ref, o_ref, *, transpose):
    h = _rms(x_ref[...], g_ref[...]) * (1.0 + sc_ref[0]) + sh_ref[0]
    if transpose:
        o_ref[...] = h.T.astype(o_ref.dtype)
    else:
        o_ref[...] = h.astype(o_ref.dtype)


def ln_modulate(x, g, mod3, shift_chunk, n_rows, seq, n_batch, transpose=False):
    d = x.shape[1]
    tm = 256
    seg = functools.partial(_segment, tile_rows=tm, seq=seq, n_batch=n_batch)
    if transpose:
        out_spec = pl.BlockSpec((d, tm), lambda i: (0, i))
        out_shape = jax.ShapeDtypeStruct((d, n_rows), BF16)
    else:
        out_spec = pl.BlockSpec((tm, d), lambda i: (i, 0))
        out_shape = jax.ShapeDtypeStruct((n_rows, d), BF16)
    return pl.pallas_call(
        functools.partial(_ln_mod_kernel, transpose=transpose),
        grid=(n_rows // tm,),
        in_specs=[pl.BlockSpec((tm, d), lambda i: (i, 0)),
                  pl.BlockSpec((1, d), lambda i: (0, 0)),
                  pl.BlockSpec((1, 1, d), lambda i: (seg(i), 0, shift_chunk + 1)),
                  pl.BlockSpec((1, 1, d), lambda i: (seg(i), 0, shift_chunk))],
        out_specs=out_spec,
        out_shape=out_shape,
        compiler_params=_cparams(1, 48),
        name="ln_modulate_t" if transpose else "ln_modulate",
    )(x, g.reshape(1, d), mod3, mod3)


def _mm_kernel(a_ref, b_ref, o_ref, *, act):
    acc = jnp.dot(a_ref[...], b_ref[...], preferred_element_type=F32)
    if act == "sigmoid":
        acc = jax.nn.sigmoid(acc)
    o_ref[...] = acc.astype(o_ref.dtype)


def matmul(a, b, out_dtype, act=None, n_rows=None, tn=1024, tm=512):
    m, k = a.shape
    m = n_rows or m
    n = b.shape[1]
    if n % tn:
        tn = n
    return pl.pallas_call(
        functools.partial(_mm_kernel, act=act),
        grid=(n // tn, m // tm),
        in_specs=[pl.BlockSpec((tm, k), lambda j, i: (i, 0)),
                  pl.BlockSpec((k, tn), lambda j, i: (0, j))],
        out_specs=pl.BlockSpec((tm, tn), lambda j, i: (i, j)),
        out_shape=jax.ShapeDtypeStruct((m, n), out_dtype),
        compiler_params=_cparams(2, 52),
        name="matmul_" + (act or "plain"),
    )(a, b)


def _mm_residual_kernel(a_ref, b_ref, x_ref, g_ref, o_ref):
    acc = jnp.dot(a_ref[...], b_ref[...], preferred_element_type=F32)
    o_ref[...] = x_ref[...] + g_ref[0] * acc


def matmul_gated_residual(a, b, x, mod3, gate_chunk, seq, n_batch, tn=1024, tm=512):
    m, k = a.shape
    n = b.shape[1]
    seg = functools.partial(_segment, tile_rows=tm, seq=seq, n_batch=n_batch)
    nj = n // tn
    return pl.pallas_call(
        _mm_residual_kernel,
        grid=(nj, m // tm),
        in_specs=[pl.BlockSpec((tm, k), lambda j, i: (i, 0)),
                  pl.BlockSpec((k, tn), lambda j, i: (0, j)),
                  pl.BlockSpec((tm, tn), lambda j, i: (i, j)),
                  pl.BlockSpec((1, 1, tn), lambda j, i: (seg(i), 0, gate_chunk * nj + j))],
        out_specs=pl.BlockSpec((tm, tn), lambda j, i: (i, j)),
        out_shape=jax.ShapeDtypeStruct((m, n), F32),
        compiler_params=_cparams(2, 52),
        name="matmul_gated_residual",
    )(a, b, x, mod3)


def _merge_kernel(a_ref, b_ref, c_ref, wa_ref, wb_ref, wc_ref, ga_ref, gb_ref, gc_ref, o_ref):
    m = ga_ref[...] * jnp.dot(a_ref[...], wa_ref[...], preferred_element_type=F32)
    m = m + gb_ref[...] * jnp.dot(b_ref[...], wb_ref[...], preferred_element_type=F32)
    m = m + gc_ref[...] * jnp.dot(c_ref[...], wc_ref[...], preferred_element_type=F32)
    o_ref[...] = m.astype(o_ref.dtype)


def merge_branches(a, b, c, wa, wb, wc, gates, tn=1024, tm=512):
    m = a.shape[0]
    n = wa.shape[1]
    nj = n // tn
    row = lambda j, i: (i, 0)
    col = lambda j, i: (0, j)
    return pl.pallas_call(
        _merge_kernel,
        grid=(nj, m // tm),
        in_specs=[pl.BlockSpec((tm, a.shape[1]), row),
                  pl.BlockSpec((tm, b.shape[1]), row),
                  pl.BlockSpec((tm, c.shape[1]), row),
                  pl.BlockSpec((wa.shape[0], tn), col),
                  pl.BlockSpec((wb.shape[0], tn), col),
                  pl.BlockSpec((wc.shape[0], tn), col),
                  pl.BlockSpec((tm, tn), lambda j, i: (i, j)),
                  pl.BlockSpec((tm, tn), lambda j, i: (i, nj + j)),
                  pl.BlockSpec((tm, tn), lambda j, i: (i, 2 * nj + j))],
        out_specs=pl.BlockSpec((tm, tn), lambda j, i: (i, j)),
        out_shape=jax.ShapeDtypeStruct((m, n), BF16),
        compiler_params=_cparams(2, 52),
        name="merge_branches",
    )(a, b, c, wa, wb, wc, gates, gates, gates)


def _rotate(v, c_ref, s1_ref, s2_ref):
    return (v * c_ref[...] + pltpu.roll(v, LANES - A_ROPE // 2, 1) * s1_ref[...]
            + pltpu.roll(v, A_ROPE // 2, 1) * s2_ref[...])


def _mla_q_kernel(cq_ref, g_ref, w_ref, c_ref, s1_ref, s2_ref, o_ref):
    y = _rms(cq_ref[...], g_ref[...]).astype(BF16)
    acc = jnp.dot(y, w_ref[...], preferred_element_type=F32)
    for h in range(A_HEADS):
        lo = h * A_HEAD_PAD
        o_ref[:, lo:lo + A_NOPE] = acc[:, lo:lo + A_NOPE].astype(o_ref.dtype)
        rot = _rotate(acc[:, lo + A_NOPE:lo + A_HEAD_PAD], c_ref, s1_ref, s2_ref)
        o_ref[:, lo + A_NOPE:lo + A_HEAD_PAD] = rot.astype(o_ref.dtype)


def mla_q(p_small, g, w_uq_pad, rope_c, rope_s1, rope_s2):
    nt = p_small.shape[0]
    tm = 256
    width = A_HEADS * A_HEAD_PAD
    tab = pl.BlockSpec((tm, LANES), lambda i: (i, 0))
    return pl.pallas_call(
        _mla_q_kernel,
        grid=(nt // tm,),
        in_specs=[pl.BlockSpec((tm, A_QLORA), lambda i: (i, 0)),
                  pl.BlockSpec((1, A_QLORA), lambda i: (0, 0)),
                  pl.BlockSpec((A_QLORA, width), lambda i: (0, 0)),
                  tab, tab, tab],
        out_specs=pl.BlockSpec((tm, width), lambda i: (i, 0)),
        out_shape=jax.ShapeDtypeStruct((nt, width), BF16),
        compiler_params=_cparams(1, 48),
        name="mla_q",
    )(p_small, g.reshape(1, A_QLORA), w_uq_pad, rope_c, rope_s1, rope_s2)


def _mla_kv_kernel(ckv_ref, kr_ref, g_ref, w_ref, c_ref, s1_ref, s2_ref, k_ref, v_ref):
    y = _rms(ckv_ref[...], g_ref[...]).astype(BF16)
    acc = jnp.dot(y, w_ref[...], preferred_element_type=F32)
    kr = _rotate(kr_ref[...], c_ref, s1_ref, s2_ref).astype(k_ref.dtype)
    for h in range(A_HEADS):
        lo = h * A_HEAD_PAD
        k_ref[:, lo:lo + A_NOPE] = acc[:, h * A_NOPE:(h + 1) * A_NOPE].astype(k_ref.dtype)
        k_ref[:, lo + A_NOPE:lo + A_HEAD_PAD] = kr
    v_ref[...] = acc[:, A_HEADS * A_NOPE:].astype(v_ref.dtype)


def mla_kv(p_small, g, w_ukv_perm, rope_c, rope_s1, rope_s2):
    nt = p_small.shape[0]
    tm = 256
    kw = A_HEADS * A_HEAD_PAD
    vw = A_HEADS * A_VDIM
    tab = pl.BlockSpec((tm, LANES), lambda i: (i, 0))
    return pl.pallas_call(
        _mla_kv_kernel,
        grid=(nt // tm,),
        in_specs=[pl.BlockSpec((tm, A_KVLORA), lambda i: (i, A_QLORA // A_KVLORA)),
                  pl.BlockSpec((tm, LANES), lambda i: (i, (A_QLORA + A_KVLORA) // LANES)),
                  pl.BlockSpec((1, A_KVLORA), lambda i: (0, 0)),
                  pl.BlockSpec((A_KVLORA, A_HEADS * (A_NOPE + A_VDIM)), lambda i: (0, 0)),
                  tab, tab, tab],
        out_specs=[pl.BlockSpec((tm, kw), lambda i: (i, 0)),
                   pl.BlockSpec((tm, vw), lambda i: (i, 0))],
        out_shape=[jax.ShapeDtypeStruct((nt, kw), BF16),
                   jax.ShapeDtypeStruct((nt, vw), BF16)],
        compiler_params=_cparams(1, 48),
        name="mla_kv",
    )(p_small, p_small, g.reshape(1, A_KVLORA), w_ukv_perm, rope_c, rope_s1, rope_s2)


def _qk(q, k):
    return lax.dot_general(q, k, (((1,), (1,)), ((), ())), preferred_element_type=F32)


LOG2_E = 1.4426950408889634


def _flash_two_sets_kernel(q_ref, kl_ref, vl_ref, kc_ref, vc_ref, o_ref, *, scale, chunk):
    q = q_ref[...]
    c = scale * LOG2_E
    n_chunks = kl_ref.shape[0] // chunk
    keys = lambda j: kl_ref[j * chunk:(j + 1) * chunk, :]
    t = _qk(q, kc_ref[...]) * c
    t_next = _qk(q, keys(0)) * c
    m = jnp.max(t, axis=-1, keepdims=True)
    p = jnp.exp2(t - m)
    den = jnp.sum(p, axis=-1, keepdims=True)
    acc = jnp.dot(p.astype(BF16), vc_ref[...], preferred_element_type=F32)
    for j in range(n_chunks):
        t = t_next
        if j + 1 < n_chunks:
            t_next = _qk(q, keys(j + 1)) * c
        m_new = jnp.maximum(m, jnp.max(t, axis=-1, keepdims=True))
        alpha = jnp.exp2(m - m_new)
        p = jnp.exp2(t - m_new)
        den = alpha * den + jnp.sum(p, axis=-1, keepdims=True)
        acc = alpha * acc + jnp.dot(p.astype(BF16), vl_ref[j * chunk:(j + 1) * chunk, :],
                                    preferred_element_type=F32)
        m = m_new
    o_ref[...] = (acc / den).astype(o_ref.dtype)


def mla_latent_attention(q, k, v, n_batch, seq, ctx_len, tq=1024, chunk=512):
    ctx_blk0 = (n_batch * seq) // ctx_len
    nq = seq // tq
    scale = float((A_NOPE + A_ROPE) ** -0.5)
    return pl.pallas_call(
        functools.partial(_flash_two_sets_kernel, scale=scale, chunk=chunk),
        grid=(n_batch, A_HEADS, nq),
        in_specs=[pl.BlockSpec((tq, A_HEAD_PAD), lambda b, h, i: (b * nq + i, h)),
                  pl.BlockSpec((seq, A_HEAD_PAD), lambda b, h, i: (b, h)),
                  pl.BlockSpec((seq, A_VDIM), lambda b, h, i: (b, h)),
                  pl.BlockSpec((ctx_len, A_HEAD_PAD), lambda b, h, i: (ctx_blk0 + b, h)),
                  pl.BlockSpec((ctx_len, A_VDIM), lambda b, h, i: (ctx_blk0 + b, h))],
        out_specs=pl.BlockSpec((tq, A_VDIM), lambda b, h, i: (b * nq + i, h)),
        out_shape=jax.ShapeDtypeStruct((n_batch * seq, A_HEADS * A_VDIM), BF16),
        compiler_params=_cparams(3, 48),
        name="mla_latent_attention",
    )(q, k, v, k, v)


def _attn_one_set_kernel(q_ref, k_ref, v_ref, o_ref, *, scale):
    s = _qk(q_ref[...], k_ref[...]) * scale
    p = jnp.exp(s - jnp.max(s, axis=-1, keepdims=True))
    den = jnp.sum(p, axis=-1, keepdims=True)
    o = jnp.dot(p.astype(BF16), v_ref[...], preferred_element_type=F32)
    o_ref[...] = (o / den).astype(o_ref.dtype)


def context_attention(q, k, v, n_batch, seq, ctx_len, n_heads, dqk, dv, q_col0, k_col0, v_col0, scale):
    ctx_blk0 = (n_batch * seq) // ctx_len
    return pl.pallas_call(
        functools.partial(_attn_one_set_kernel, scale=scale),
        grid=(n_batch, n_heads),
        in_specs=[pl.BlockSpec((ctx_len, dqk), lambda b, h: (ctx_blk0 + b, q_col0 + h)),
                  pl.BlockSpec((ctx_len, dqk), lambda b, h: (ctx_blk0 + b, k_col0 + h)),
                  pl.BlockSpec((ctx_len, dv), lambda b, h: (ctx_blk0 + b, v_col0 + h))],
        out_specs=pl.BlockSpec((ctx_len, dv), lambda b, h: (b, h)),
        out_shape=jax.ShapeDtypeStruct((n_batch * ctx_len, n_heads * dv), BF16),
        compiler_params=_cparams(2, 32),
        name="context_attention",
    )(q, k, v)


def _na_kernel(q_ref, kl_ref, vl_ref, kc_ref, vc_ref, bias_ref, o_ref, *, rows, scale):
    r = pl.program_id(1)
    r0 = jnp.clip(r - WIN_H // 2, 0, rows - WIN_H)
    start = pl.multiple_of(r0 * GRID_W, GRID_W)
    win = WIN_H * GRID_W
    heads = [slice(h * B_DH, (h + 1) * B_DH) for h in range(B_HEADS)]
    scores = [(_qk(q_ref[:, cols], kl_ref[pl.ds(start, win), cols]), _qk(q_ref[:, cols], kc_ref[:, cols]))
              for cols in heads]
    probs = []
    for h, (s_l, s_c) in enumerate(scores):
        s_l = s_l * scale + bias_ref[0, h]
        s_c = s_c * scale
        m = jnp.maximum(jnp.max(s_l, axis=-1, keepdims=True), jnp.max(s_c, axis=-1, keepdims=True))
        p_l = jnp.exp(s_l - m)
        p_c = jnp.exp(s_c - m)
        den = jnp.sum(p_l, axis=-1, keepdims=True) + jnp.sum(p_c, axis=-1, keepdims=True)
        probs.append((p_l.astype(BF16), p_c.astype(BF16), den))
    outs = []
    for cols, (p_l, p_c, den) in zip(heads, probs):
        o = jnp.dot(p_l, vl_ref[pl.ds(start, win), cols], preferred_element_type=F32)
        o = o + jnp.dot(p_c, vc_ref[:, cols], preferred_element_type=F32)
        outs.append(o / den)
    o_ref[...] = jnp.concatenate(outs, axis=1).astype(o_ref.dtype)


def na_latent_attention(qkv, bias, n_batch, seq, ctx_len):
    rows = seq // GRID_W
    ctx_blk0 = (n_batch * seq) // ctx_len
    win = WIN_H * GRID_W

    def bias_idx(b, r):
        return (r - jnp.clip(r - WIN_H // 2, 0, rows - WIN_H), 0, 0, 0)

    once = pl.Buffered(1)
    return pl.pallas_call(
        functools.partial(_na_kernel, rows=rows, scale=float(B_DH ** -0.5)),
        grid=(n_batch, rows),
        in_specs=[pl.BlockSpec((GRID_W, B_WIDTH), lambda b, r: (b * rows + r, 0)),
                  pl.BlockSpec((seq, B_WIDTH), lambda b, r: (b, 1), pipeline_mode=once),
                  pl.BlockSpec((seq, B_WIDTH), lambda b, r: (b, 2), pipeline_mode=once),
                  pl.BlockSpec((ctx_len, B_WIDTH), lambda b, r: (ctx_blk0 + b, 1)),
                  pl.BlockSpec((ctx_len, B_WIDTH), lambda b, r: (ctx_blk0 + b, 2)),
                  pl.BlockSpec((1, B_HEADS, GRID_W, win), bias_idx)],
        out_specs=pl.BlockSpec((GRID_W, B_WIDTH), lambda b, r: (b * rows + r, 0)),
        out_shape=jax.ShapeDtypeStruct((n_batch * seq, B_WIDTH), BF16),
        compiler_params=_cparams(2, 40),
        name="na_latent_attention",
    )(qkv, qkv, qkv, qkv, qkv, bias)


def na_bias_table(rel_bias):
    rel = rel_bias.astype(F32)
    n_layers = rel.shape[0]
    by_row = jnp.stack([rel[:, :, WIN_H - 1 - d:2 * WIN_H - 1 - d, :] for d in range(WIN_H)], axis=2)
    pad = GRID_W - 1
    by_row = jnp.pad(by_row, ((0, 0), (0, 0), (0, 0), (0, 0), (pad, pad)))
    off = WIN_W - 1 + pad
    tab = jnp.stack([by_row[..., off - c:off - c + GRID_W] for c in range(GRID_W)], axis=4)
    c = jnp.arange(GRID_W)[:, None]
    kc = jnp.arange(GRID_W)[None, :]
    c0 = jnp.clip(c - WIN_W // 2, 0, GRID_W - WIN_W)
    inside = (kc >= c0) & (kc < c0 + WIN_W)
    tab = jnp.where(inside[None, None, None, None], tab, NEG_BIG)
    return tab.transpose(0, 2, 1, 4, 3, 5).reshape(n_layers, WIN_H, B_HEADS, GRID_W, WIN_H * GRID_W)


def _pool_kernel(x_ref, w_ref, sc_ref, o_ref, pad_ref, *, seq_len, chunk):
    g = pl.program_id(1)
    zeros = jnp.zeros((POOL_HALO, C_GW), F32)
    pad_ref[0:POOL_HALO, :] = zeros
    pad_ref[POOL_HALO + seq_len:2 * POOL_HALO + seq_len, :] = zeros
    pad_ref[POOL_HALO:POOL_HALO + seq_len, :] = x_ref[...]
    for gi, w in enumerate(POOL_SIZES):
        @pl.when(g == gi)
        def _(w=w):
            for c0 in range(0, seq_len, chunk):
                acc = None
                for k in range(-(w // 2), w - w // 2):
                    piece = pad_ref[POOL_HALO + c0 + k:POOL_HALO + c0 + k + chunk, :]
                    acc = piece if acc is None else acc + piece
                t = c0 + lax.broadcasted_iota(jnp.int32, (chunk, C_GW), 0)
                cnt = jnp.minimum(t + (w - w // 2), seq_len) - jnp.maximum(t - w // 2, 0)
                d = (acc / cnt.astype(F32) - x_ref[c0:c0 + chunk, :]).astype(BF16)
                y = jnp.dot(d, w_ref[0], preferred_element_type=F32) * sc_ref[...]
                o_ref[c0:c0 + chunk, :] = y.astype(o_ref.dtype)


def pool_mix(p_pool, pool_w, pool_scale, n_seq, seq_len, first_block):
    chunk = min(seq_len, 512)
    return pl.pallas_call(
        functools.partial(_pool_kernel, seq_len=seq_len, chunk=chunk),
        grid=(n_seq, len(POOL_SIZES)),
        in_specs=[pl.BlockSpec((seq_len, C_GW), lambda b, g: (first_block + b, g)),
                  pl.BlockSpec((1, C_GW, C_GW), lambda b, g: (g, 0, 0)),
                  pl.BlockSpec((1, C_GW), lambda b, g: (0, g))],
        out_specs=pl.BlockSpec((seq_len, C_GW), lambda b, g: (b, g)),
        out_shape=jax.ShapeDtypeStruct((n_seq * seq_len, C_WIDTH), BF16),
        scratch_shapes=[pltpu.VMEM((seq_len + 2 * POOL_HALO, C_GW), F32)],
        compiler_params=_cparams(2, 48),
        name="pool_mix",
    )(p_pool, pool_w, pool_scale.reshape(1, C_WIDTH))


_REGION = [(r1, r2) for r1 in range(P_TOPK) for r2 in range(P_TOPK) if (r1 + 1) * (r2 + 1) <= P_TOPK]


def _dominates(a, b):
    return a != b and a[0] <= b[0] and a[1] <= b[1]


def _router_kernel(h_ref, wq_ref, k1_ref, k2_ref, cnt_ref, a_ref, rk_ref, b_ref,
                   s_scr, v1_scr, i1_scr, v2_scr, i2_scr, cr_scr, e1_scr, e2_scr):
    tm = h_ref.shape[1]
    half = P_HEADS * P_DK // 2
    q_t = jnp.dot(wq_ref[...], h_ref[...], preferred_element_type=F32)
    key = lax.broadcasted_iota(jnp.int32, (P_NKEYS, P_HEADS, tm), 0)

    def top_sorted(k_ref, q_half, v_scr, i_scr):
        s_scr[...] = jnp.dot(k_ref[...], q_half.astype(BF16),
                             preferred_element_type=F32).reshape(P_NKEYS, P_HEADS, tm)

        def body(r, carry):
            s = s_scr[...]
            m = jnp.max(s, axis=0)
            idx = jnp.min(jnp.where(s == m[None], key, P_NKEYS), axis=0)
            s_scr[...] = jnp.where(key == idx[None], -jnp.inf, s)
            v_scr[r] = m
            i_scr[r] = idx
            return carry

        lax.fori_loop(0, P_TOPK, body, 0)

    top_sorted(k1_ref, q_t[:half], v1_scr, i1_scr)
    top_sorted(k2_ref, q_t[half:], v2_scr, i2_scr)

    v1 = [v1_scr[r] for r in range(P_TOPK)]
    v2 = [v2_scr[r] for r in range(P_TOPK)]
    sums = {c: v1[c[0]] + v2[c[1]] for c in _REGION}
    beaten = {c: float(sum(_dominates(o, c) for o in _REGION)) for c in _REGION}
    for ci, c in enumerate(_REGION):
        for o in _REGION[:ci]:
            if _dominates(o, c):
                continue
            o_first = (sums[o] >= sums[c]).astype(F32)
            beaten[c] = beaten[c] + o_first
            beaten[o] = beaten[o] + (1.0 - o_first)
    e1 = [jnp.exp(v1[r] - v1[0]) for r in range(P_TOPK)]
    e2 = [jnp.exp(v2[r] - v2[0]) for r in range(P_TOPK)]
    z = jnp.zeros_like(v1[0])
    counts = [jnp.zeros_like(v1[0]) for _ in range(P_TOPK)]
    for c in _REGION:
        chosen = beaten[c] < float(P_TOPK)
        z = z + jnp.where(chosen, e1[c[0]] * e2[c[1]], 0.0)
        counts[c[0]] = counts[c[0]] + jnp.where(chosen, 1.0, 0.0)
    inv_z = 1.0 / z
    for r in range(P_TOPK):
        cr_scr[r] = counts[r]
        e1_scr[r] = e1[r]
        e2_scr[r] = e2[r] * inv_z

    key2 = lax.broadcasted_iota(jnp.int32, (P_NKEYS, LANES), 0)

    def expand(n, carry, h):
        t0 = pl.multiple_of(n * LANES, LANES)
        row = lambda ref, r: ref[r, h:h + 1, pl.ds(t0, LANES)]
        cnt = jnp.zeros((P_NKEYS, LANES), F32)
        a = cnt
        for r in range(P_TOPK):
            hit = key2 == row(i1_scr, r)
            cnt = jnp.where(hit, row(cr_scr, r), cnt)
            a = jnp.where(hit, row(e1_scr, r), a)
        cnt_ref[h, :, pl.ds(t0, LANES)] = cnt
        a_ref[h, :, pl.ds(t0, LANES)] = a
        rk = jnp.full((P_NKEYS, LANES), float(P_TOPK), F32)
        b = jnp.zeros((P_NKEYS, LANES), F32)
        for r in range(P_TOPK):
            hit = key2 == row(i2_scr, r)
            rk = jnp.where(hit, float(r), rk)
            b = jnp.where(hit, row(e2_scr, r), b)
        rk_ref[h, :, pl.ds(t0, LANES)] = rk.astype(rk_ref.dtype)
        b_ref[h, :, pl.ds(t0, LANES)] = b.astype(b_ref.dtype)
        return carry

    for h in range(P_HEADS):
        lax.fori_loop(0, tm // LANES, functools.partial(expand, h=h), 0)


def peer_router(h_t, wq_t, k1_packed, k2_packed, tm=256):
    d, n = h_t.shape
    qw = wq_t.shape[0]
    kp = k1_packed.shape[0]
    tab = lambda dt: jax.ShapeDtypeStruct((P_HEADS, P_NKEYS, n), dt)
    tab_spec = pl.BlockSpec((P_HEADS, P_NKEYS, tm), lambda i: (0, 0, i))
    small = lambda dt: pltpu.VMEM((P_TOPK, P_HEADS, tm), dt)
    return pl.pallas_call(
        _router_kernel,
        grid=(n // tm,),
        in_specs=[pl.BlockSpec((d, tm), lambda i: (0, i)),
                  pl.BlockSpec((qw, d), lambda i: (0, 0), pipeline_mode=pl.Buffered(1)),
                  pl.BlockSpec((kp, kp), lambda i: (0, 0), pipeline_mode=pl.Buffered(1)),
                  pl.BlockSpec((kp, kp), lambda i: (0, 0), pipeline_mode=pl.Buffered(1))],
        out_specs=[tab_spec] * 4,
        out_shape=[tab(F32)] * 4,
        scratch_shapes=[pltpu.VMEM((P_NKEYS, P_HEADS, tm), F32),
                        small(F32), small(jnp.int32), small(F32), small(jnp.int32),
                        small(F32), small(F32), small(F32)],
        compiler_params=_cparams(1, 52),
        name="peer_router",
    )(h_t, wq_t, k1_packed, k2_packed)


def _gelu_exact(x):
    return 0.5 * x * (1.0 + lax.erf(x * float(np.sqrt(0.5))))


def _peer_kernel(h_ref, u_ref, v_ref, cnt_ref, a_ref, rk_ref, b_ref, o_ref):
    e = pl.program_id(1)
    te = u_ref.shape[0]
    tm = h_ref.shape[1]

    def step(first):
        halves = [(c0, jnp.dot(u_ref[c0:c0 + te // 2, :], h_ref[...], preferred_element_type=F32))
                  for c0 in (0, te // 2)]
        rows_per_half = te // 2 // P_NKEYS
        n_slab = P_NKEYS // SUBLANES
        ga_rows = []
        for c0, pre in halves:
            i0 = e * (te // P_NKEYS) + c0 // P_NKEYS
            rows = [[(cnt_ref[h, pl.ds(i0 + k, 1), :], a_ref[h, pl.ds(i0 + k, 1), :]) for h in range(P_HEADS)]
                    for k in range(rows_per_half)]
            tiles = {}
            for t0 in range(0, tm, LANES):
                lanes = slice(t0, t0 + LANES)
                g = [[None] * n_slab for _ in range(rows_per_half)]
                for h in range(P_HEADS):
                    per_row = [[jnp.broadcast_to(x[:, lanes], (SUBLANES, LANES)) for x in rows[k][h]]
                               for k in range(rows_per_half)]
                    for s in range(n_slab):
                        slab = slice(s * SUBLANES, (s + 1) * SUBLANES)
                        rk = rk_ref[h, slab, lanes]
                        b = b_ref[h, slab, lanes]
                        for k, (cnt, a) in enumerate(per_row):
                            term = jnp.where(rk < cnt, b * a, jnp.zeros_like(b))
                            g[k][s] = term if g[k][s] is None else g[k][s] + term
                for k in range(rows_per_half):
                    for s in range(n_slab):
                        r = k * P_NKEYS + s * SUBLANES
                        tiles[(k, s, t0)] = g[k][s] * _gelu_exact(pre[r:r + SUBLANES, lanes])
            for k in range(rows_per_half):
                for s in range(n_slab):
                    ga_rows.append(jnp.concatenate([tiles[(k, s, t0)] for t0 in range(0, tm, LANES)], axis=1))
        ga = jnp.concatenate(ga_rows, axis=0).astype(BF16)
        contrib = lax.dot_general(v_ref[...], ga, (((0,), (0,)), ((), ())), preferred_element_type=F32)
        if first:
            o_ref[...] = contrib
        else:
            o_ref[...] += contrib

    pl.when(e == 0)(functools.partial(step, True))
    pl.when(e > 0)(functools.partial(step, False))


def peer_mixture(h_t, u_all, v_all, layer, tables, tm=512, te=512):
    d, n = h_t.shape
    tab_spec = pl.BlockSpec((P_HEADS, P_NKEYS, tm), lambda i, e: (0, 0, i), pipeline_mode=pl.Buffered(1))
    expert_tile = pl.BlockSpec((None, te, d), lambda i, e: (layer, e, 0))
    return pl.pallas_call(
        _peer_kernel,
        grid=(n // tm, u_all.shape[1] // te),
        in_specs=[pl.BlockSpec((d, tm), lambda i, e: (0, i), pipeline_mode=pl.Buffered(1)),
                  expert_tile, expert_tile] + [tab_spec] * 4,
        out_specs=pl.BlockSpec((d, tm), lambda i, e: (0, i)),
        out_shape=jax.ShapeDtypeStruct((d, n), F32),
        compiler_params=_cparams(2, 56),
        name="peer_mixture",
    )(h_t, u_all, v_all, *tables)


def _residual_t_ln_kernel(x_ref, y_ref, g_ref, ng_ref, sc_ref, sh_ref, o_ref, h_ref):
    x = x_ref[...] + g_ref[0] * y_ref[...].T
    o_ref[...] = x
    h_ref[...] = (_rms(x, ng_ref[...]) * (1.0 + sc_ref[0]) + sh_ref[0]).astype(h_ref.dtype)


def _residual_t_final_kernel(x_ref, y_ref, g_ref, ng_ref, o_ref):
    o_ref[...] = _rms(x_ref[...] + g_ref[0] * y_ref[...].T, ng_ref[...])


def residual_from_transposed(x, y_t, mod3, gate_chunk, n_rows, seq, n_batch, norm_g, next_mod3=None):
    d = x.shape[1]
    tm = 256
    seg = functools.partial(_segment, tile_rows=tm, seq=seq, n_batch=n_batch)
    row = pl.BlockSpec((tm, d), lambda i: (i, 0))
    in_specs = [row,
                pl.BlockSpec((d, tm), lambda i: (0, i)),
                pl.BlockSpec((1, 1, d), lambda i: (seg(i), 0, gate_chunk)),
                pl.BlockSpec((1, d), lambda i: (0, 0))]
    if next_mod3 is None:
        return pl.pallas_call(
            _residual_t_final_kernel,
            grid=(n_rows // tm,),
            in_specs=in_specs,
            out_specs=row,
            out_shape=jax.ShapeDtypeStruct((n_rows, d), F32),
            compiler_params=_cparams(1, 48),
            name="residual_final_norm",
        )(x, y_t, mod3, norm_g.reshape(1, d))
    return pl.pallas_call(
        _residual_t_ln_kernel,
        grid=(n_rows // tm,),
        in_specs=in_specs + [pl.BlockSpec((1, 1, d), lambda i: (seg(i), 0, 1)),
                             pl.BlockSpec((1, 1, d), lambda i: (seg(i), 0, 0))],
        out_specs=[row, row],
        out_shape=[jax.ShapeDtypeStruct((n_rows, d), F32), jax.ShapeDtypeStruct((n_rows, d), BF16)],
        compiler_params=_cparams(1, 48),
        name="residual_next_norm",
    )(x, y_t, mod3, norm_g.reshape(1, d), next_mod3, next_mod3)


def _rope_tables(seq, n_batch, n_ctx_rows):
    t = jnp.arange(seq)
    row = (t // GRID_W).astype(F32)
    col = (t % GRID_W).astype(F32)
    n_freq = A_ROPE // 4
    freqs = ROPE_THETA ** (-jnp.arange(n_freq, dtype=F32) / n_freq)
    ang = jnp.concatenate([row[:, None] * freqs, col[:, None] * freqs], axis=-1)
    cos, sin = jnp.cos(ang), jnp.sin(ang)
    zero = jnp.zeros_like(cos)
    pad = jnp.zeros((seq, LANES - A_ROPE), F32)
    c = jnp.concatenate([cos, cos, pad], axis=1)
    s1 = jnp.concatenate([-sin, zero, pad], axis=1)
    s2 = jnp.concatenate([zero, sin, pad], axis=1)
    ident = jnp.concatenate([jnp.ones((n_ctx_rows, A_ROPE), F32), jnp.zeros((n_ctx_rows, LANES - A_ROPE), F32)], 1)
    none = jnp.zeros((n_ctx_rows, LANES), F32)
    tile = lambda a, ctx: jnp.concatenate([jnp.tile(a, (n_batch, 1)), ctx], axis=0)
    return tile(c, ident), tile(s1, none), tile(s2, none)


def _pack_keys(k):
    h, nk, dk = k.shape
    eye = jnp.eye(h, dtype=k.dtype)
    blk = k.transpose(1, 0, 2)[:, :, None, :] * eye[None, :, :, None]
    return blk.reshape(nk * h, h * dk).astype(BF16)


def _prepare_layer(w_in, w_uq, w_ukv, w_br_a, w_br_b, w_br_c, w_out, peer_wq, peer_k1, peer_k2):
    d = w_in.shape[0]
    o_kr = A_QLORA + A_KVLORA
    o_q = o_kr + A_ROPE
    o_pool = o_q + 3 * B_WIDTH
    o_gate = o_pool + C_WIDTH
    w_small = jnp.concatenate([w_in[:, :o_q], jnp.zeros((d, LANES - A_ROPE), w_in.dtype)], axis=1)
    w_uq_pad = jnp.pad(w_uq.reshape(A_QLORA, A_HEADS, A_NOPE + A_ROPE),
                       ((0, 0), (0, 0), (0, A_HEAD_PAD - A_NOPE - A_ROPE))).reshape(A_QLORA, A_HEADS * A_HEAD_PAD)
    w_ukv_perm = w_ukv.reshape(A_KVLORA, A_HEADS, 2, A_NOPE).transpose(0, 2, 1, 3).reshape(A_KVLORA, -1)
    wq_t = peer_wq.reshape(d, P_HEADS, 2, P_DK // 2).transpose(2, 1, 3, 0).reshape(P_HEADS * P_DK, d)
    return dict(
        w_small=w_small.astype(BF16),
        w_qkv=w_in[:, o_q:o_pool].astype(BF16),
        w_pool=w_in[:, o_pool:o_gate].astype(BF16),
        w_gate=w_in[:, o_gate:].astype(BF16),
        w_uq=w_uq_pad.astype(BF16),
        w_ukv=w_ukv_perm.astype(BF16),
        w_br_a=w_br_a.astype(BF16), w_br_b=w_br_b.astype(BF16), w_br_c=w_br_c.astype(BF16),
        w_out=w_out.astype(BF16),
        wq_t=wq_t.astype(BF16),
        k1=_pack_keys(peer_k1), k2=_pack_keys(peer_k2),
    )


def kernel(x, c, ctx, c_ctx, w_ada, b_ada, norm1_g, norm2_g, w_in, q_norm_g, kv_norm_g, w_uq, w_ukv, na_rel_bias, pool_w, pool_scale, w_br_a, w_br_b, w_br_c, w_out, peer_wq, peer_k1, peer_k2, peer_u, peer_v, final_g):
    n_batch, seq, d = x.shape
    ctx_len = ctx.shape[1]
    depth = w_ada.shape[0]
    n_lat = n_batch * seq
    n_ctx = n_batch * ctx_len
    nt = n_lat + n_ctx
    ctx_blk0 = n_lat // ctx_len

    stream = jnp.concatenate([x.reshape(n_lat, d), ctx.reshape(n_ctx, d)], axis=0)
    mod_rows = 8
    cvec = jnp.concatenate([c, c_ctx[None], jnp.zeros((mod_rows - n_batch - 1, d), c.dtype)], axis=0)
    mod = ada_modulation(cvec, w_ada, b_ada)
    rope_c, rope_s1, rope_s2 = _rope_tables(seq, n_batch, n_ctx)
    na_bias = na_bias_table(na_rel_bias)
    mod3s = [mod[l].reshape(mod_rows, 1, 6 * d) for l in range(depth)]
    u_all = peer_u.astype(BF16)
    v_all = peer_v.astype(BF16)

    h = ln_modulate(stream, norm1_g[0], mod3s[0], 0, nt, seq, n_batch)
    for l in range(depth):
        ctx_out = l < depth - 1
        n_act = nt if ctx_out else n_lat
        wl = _prepare_layer(w_in[l], w_uq[l], w_ukv[l], w_br_a[l], w_br_b[l], w_br_c[l], w_out[l],
                            peer_wq[l], peer_k1[l], peer_k2[l])
        mod3 = mod3s[l]

        p_small = matmul(h, wl["w_small"], F32)
        qkv = matmul(h, wl["w_qkv"], BF16)
        p_pool = matmul(h, wl["w_pool"], F32)
        gates = matmul(h, wl["w_gate"], F32, act="sigmoid", n_rows=n_act)

        q = mla_q(p_small, q_norm_g[l], wl["w_uq"], rope_c, rope_s1, rope_s2)
        k, v = mla_kv(p_small, kv_norm_g[l], wl["w_ukv"], rope_c, rope_s1, rope_s2)
        a_br = mla_latent_attention(q, k, v, n_batch, seq, ctx_len)
        b_br = na_latent_attention(qkv, na_bias[l], n_batch, seq, ctx_len)
        pw = pool_w[l].astype(BF16)
        c_br = pool_mix(p_pool, pw, pool_scale[l], n_batch, seq, 0)
        if ctx_out:
            a_ctx = context_attention(q, k, v, n_batch, seq, ctx_len, A_HEADS, A_HEAD_PAD, A_VDIM, 0, 0, 0,
                                      float((A_NOPE + A_ROPE) ** -0.5))
            b_ctx = context_attention(qkv, qkv, qkv, n_batch, seq, ctx_len, B_HEADS, B_DH, B_DH,
                                      0, B_HEADS, 2 * B_HEADS, float(B_DH ** -0.5))
            c_ctx_br = pool_mix(p_pool, pw, pool_scale[l], n_batch, ctx_len, ctx_blk0)
            a_br = jnp.concatenate([a_br, a_ctx], axis=0)
            b_br = jnp.concatenate([b_br, b_ctx], axis=0)
            c_br = jnp.concatenate([c_br, c_ctx_br], axis=0)

        merged = merge_branches(a_br, b_br, c_br, wl["w_br_a"], wl["w_br_b"], wl["w_br_c"], gates)
        stream = matmul_gated_residual(merged, wl["w_out"], stream, mod3, 2, seq, n_batch)

        h2_t = ln_modulate(stream, norm2_g[l], mod3, 3, n_act, seq, n_batch, transpose=True)
        tables = peer_router(h2_t, wl["wq_t"], wl["k1"], wl["k2"])
        y_t = peer_mixture(h2_t, u_all, v_all, l, tables)
        if ctx_out:
            stream, h = residual_from_transposed(stream, y_t, mod3, 5, n_act, seq, n_batch,
                                                 norm1_g[l + 1], mod3s[l + 1])
        else:
            out = residual_from_transposed(stream, y_t, mod3, 5, n_act, seq, n_batch, final_g)

    return out.reshape(n_batch, seq, d)
```

```python
import functools

import numpy as np
import jax
import jax.numpy as jnp
from jax import lax
from jax.experimental import pallas as pl
from jax.experimental.pallas import tpu as pltpu

F32 = jnp.float32
BF16 = jnp.bfloat16

GRID_W = 64
EPS = 1e-6
ROPE_THETA = 10000.0

A_HEADS = 16
A_NOPE = 128
A_ROPE = 64
A_VDIM = 128
A_QLORA = 768
A_KVLORA = 256
A_HEAD_PAD = 256

B_HEADS = 8
B_DH = 128
B_WIDTH = B_HEADS * B_DH
WIN_H = 8
WIN_W = 16

C_WIDTH = 1024
POOL_SIZES = (2, 4, 8, 16)
C_GW = C_WIDTH // len(POOL_SIZES)
POOL_HALO = 8

P_HEADS = 8
P_NKEYS = 128
P_DK = 256
P_TOPK = 16

LANES = 128
SUBLANES = 8
SMALL_W = A_QLORA + A_KVLORA + LANES

NEG_BIG = -1e30
MIB = 1024 * 1024


def _cparams(n_axes, vmem_mib):
    return pltpu.CompilerParams(dimension_semantics=("arbitrary",) * n_axes,
                                vmem_limit_bytes=vmem_mib * MIB)


def _segment(row_tile, tile_rows, seq, n_batch):
    return jnp.minimum((row_tile * tile_rows) // seq, n_batch)


def _ada_kernel(c_ref, w_ref, b_ref, o_ref):
    c = c_ref[...]
    a = (c * jax.nn.sigmoid(c)).astype(BF16)
    o_ref[0] = jnp.dot(a, w_ref[0].astype(BF16), preferred_element_type=F32) + b_ref[0]


def ada_modulation(cvec, w_ada, b_ada):
    n_layers, d, n_out = w_ada.shape
    rows = cvec.shape[0]
    tn = 512
    return pl.pallas_call(
        _ada_kernel,
        grid=(n_layers, n_out // tn),
        in_specs=[pl.BlockSpec((rows, d), lambda l, j: (0, 0)),
                  pl.BlockSpec((1, d, tn), lambda l, j: (l, 0, j)),
                  pl.BlockSpec((1, 1, tn), lambda l, j: (l, 0, j))],
        out_specs=pl.BlockSpec((1, rows, tn), lambda l, j: (l, 0, j)),
        out_shape=jax.ShapeDtypeStruct((n_layers, rows, n_out), F32),
        compiler_params=_cparams(2, 40),
        name="ada_modulation",
    )(cvec, w_ada, b_ada.reshape(n_layers, 1, n_out))


def _rms(x, g):
    return x * lax.rsqrt(jnp.mean(x * x, axis=-1, keepdims=True) + EPS) * g


def _ln_mod_kernel(x_ref, g_ref, sc_ref, sh_ref, o_ref, *, transpose):
    h = _rms(x_ref[...], g_ref[...]) * (1.0 + sc_ref[0]) + sh_ref[0]
    if transpose:
        o_ref[...] = h.T.astype(o_ref.dtype)
    else:
        o_ref[...] = h.astype(o_ref.dtype)


def ln_modulate(x, g, mod3, shift_chunk, n_rows, seq, n_batch, transpose=False):
    d = x.shape[1]
    tm = 256
    seg = functools.partial(_segment, tile_rows=tm, seq=seq, n_batch=n_batch)
    if transpose:
        out_spec = pl.BlockSpec((d, tm), lambda i: (0, i))
        out_shape = jax.ShapeDtypeStruct((d, n_rows), BF16)
    else:
        out_spec = pl.BlockSpec((tm, d), lambda i: (i, 0))
        out_shape = jax.ShapeDtypeStruct((n_rows, d), BF16)
    return pl.pallas_call(
        functools.partial(_ln_mod_kernel, transpose=transpose),
        grid=(n_rows // tm,),
        in_specs=[pl.BlockSpec((tm, d), lambda i: (i, 0)),
                  pl.BlockSpec((1, d), lambda i: (0, 0)),
                  pl.BlockSpec((1, 1, d), lambda i: (seg(i), 0, shift_chunk + 1)),
                  pl.BlockSpec((1, 1, d), lambda i: (seg(i), 0, shift_chunk))],
        out_specs=out_spec,
        out_shape=out_shape,
        compiler_params=_cparams(1, 48),
        name="ln_modulate_t" if transpose else "ln_modulate",
    )(x, g.reshape(1, d), mod3, mod3)


def _mm_kernel(a_ref, b_ref, o_ref, *, act):
    acc = jnp.dot(a_ref[...], b_ref[...], preferred_element_type=F32)
    if act == "sigmoid":
        acc = jax.nn.sigmoid(acc)
    o_ref[...] = acc.astype(o_ref.dtype)


def matmul(a, b, out_dtype, act=None, n_rows=None, tn=1024, tm=512):
    m, k = a.shape
    m = n_rows or m
    n = b.shape[1]
    if n % tn:
        tn = n
    return pl.pallas_call(
        functools.partial(_mm_kernel, act=act),
        grid=(n // tn, m // tm),
        in_specs=[pl.BlockSpec((tm, k), lambda j, i: (i, 0)),
                  pl.BlockSpec((k, tn), lambda j, i: (0, j))],
        out_specs=pl.BlockSpec((tm, tn), lambda j, i: (i, j)),
        out_shape=jax.ShapeDtypeStruct((m, n), out_dtype),
        compiler_params=_cparams(2, 52),
        name="matmul_" + (act or "plain"),
    )(a, b)


def _mm_residual_kernel(a_ref, b_ref, x_ref, g_ref, o_ref):
    acc = jnp.dot(a_ref[...], b_ref[...], preferred_element_type=F32)
    o_ref[...] = x_ref[...] + g_ref[0] * acc


def matmul_gated_residual(a, b, x, mod3, gate_chunk, seq, n_batch, tn=1024, tm=512):
    m, k = a.shape
    n = b.shape[1]
    seg = functools.partial(_segment, tile_rows=tm, seq=seq, n_batch=n_batch)
    nj = n // tn
    return pl.pallas_call(
        _mm_residual_kernel,
        grid=(nj, m // tm),
        in_specs=[pl.BlockSpec((tm, k), lambda j, i: (i, 0)),
                  pl.BlockSpec((k, tn), lambda j, i: (0, j)),
                  pl.BlockSpec((tm, tn), lambda j, i: (i, j)),
                  pl.BlockSpec((1, 1, tn), lambda j, i: (seg(i), 0, gate_chunk * nj + j))],
        out_specs=pl.BlockSpec((tm, tn), lambda j, i: (i, j)),
        out_shape=jax.ShapeDtypeStruct((m, n), F32),
        compiler_params=_cparams(2, 52),
        name="matmul_gated_residual",
    )(a, b, x, mod3)


def _merge_kernel(a_ref, b_ref, c_ref, wa_ref, wb_ref, wc_ref, ga_ref, gb_ref, gc_ref, o_ref):
    m = ga_ref[...] * jnp.dot(a_ref[...], wa_ref[...], preferred_element_type=F32)
    m = m + gb_ref[...] * jnp.dot(b_ref[...], wb_ref[...], preferred_element_type=F32)
    m = m + gc_ref[...] * jnp.dot(c_ref[...], wc_ref[...], preferred_element_type=F32)
    o_ref[...] = m.astype(o_ref.dtype)


def merge_branches(a, b, c, wa, wb, wc, gates, tn=1024, tm=512):
    m = a.shape[0]
    n = wa.shape[1]
    nj = n // tn
    row = lambda j, i: (i, 0)
    col = lambda j, i: (0, j)
    return pl.pallas_call(
        _merge_kernel,
        grid=(nj, m // tm),
        in_specs=[pl.BlockSpec((tm, a.shape[1]), row),
                  pl.BlockSpec((tm, b.shape[1]), row),
                  pl.BlockSpec((tm, c.shape[1]), row),
                  pl.BlockSpec((wa.shape[0], tn), col),
                  pl.BlockSpec((wb.shape[0], tn), col),
                  pl.BlockSpec((wc.shape[0], tn), col),
                  pl.BlockSpec((tm, tn), lambda j, i: (i, j)),
                  pl.BlockSpec((tm, tn), lambda j, i: (i, nj + j)),
                  pl.BlockSpec((tm, tn), lambda j, i: (i, 2 * nj + j))],
        out_specs=pl.BlockSpec((tm, tn), lambda j, i: (i, j)),
        out_shape=jax.ShapeDtypeStruct((m, n), BF16),
        compiler_params=_cparams(2, 52),
        name="merge_branches",
    )(a, b, c, wa, wb, wc, gates, gates, gates)


def _rotate(v, c_ref, s1_ref, s2_ref):
    return (v * c_ref[...] + pltpu.roll(v, LANES - A_ROPE // 2, 1) * s1_ref[...]
            + pltpu.roll(v, A_ROPE // 2, 1) * s2_ref[...])


def _mla_q_kernel(cq_ref, g_ref, w_ref, c_ref, s1_ref, s2_ref, o_ref):
    y = _rms(cq_ref[...], g_ref[...]).astype(BF16)
    acc = jnp.dot(y, w_ref[...], preferred_element_type=F32)
    for h in range(A_HEADS):
        lo = h * A_HEAD_PAD
        o_ref[:, lo:lo + A_NOPE] = acc[:, lo:lo + A_NOPE].astype(o_ref.dtype)
        rot = _rotate(acc[:, lo + A_NOPE:lo + A_HEAD_PAD], c_ref, s1_ref, s2_ref)
        o_ref[:, lo + A_NOPE:lo + A_HEAD_PAD] = rot.astype(o_ref.dtype)


def mla_q(p_small, g, w_uq_pad, rope_c, rope_s1, rope_s2):
    nt = p_small.shape[0]
    tm = 256
    width = A_HEADS * A_HEAD_PAD
    tab = pl.BlockSpec((tm, LANES), lambda i: (i, 0))
    return pl.pallas_call(
        _mla_q_kernel,
        grid=(nt // tm,),
        in_specs=[pl.BlockSpec((tm, A_QLORA), lambda i: (i, 0)),
                  pl.BlockSpec((1, A_QLORA), lambda i: (0, 0)),
                  pl.BlockSpec((A_QLORA, width), lambda i: (0, 0)),
                  tab, tab, tab],
        out_specs=pl.BlockSpec((tm, width), lambda i: (i, 0)),
        out_shape=jax.ShapeDtypeStruct((nt, width), BF16),
        compiler_params=_cparams(1, 48),
        name="mla_q",
    )(p_small, g.reshape(1, A_QLORA), w_uq_pad, rope_c, rope_s1, rope_s2)


def _mla_kv_kernel(ckv_ref, kr_ref, g_ref, w_ref, c_ref, s1_ref, s2_ref, k_ref, v_ref):
    y = _rms(ckv_ref[...], g_ref[...]).astype(BF16)
    acc = jnp.dot(y, w_ref[...], preferred_element_type=F32)
    kr = _rotate(kr_ref[...], c_ref, s1_ref, s2_ref).astype(k_ref.dtype)
    for h in range(A_HEADS):
        lo = h * A_HEAD_PAD
        k_ref[:, lo:lo + A_NOPE] = acc[:, h * A_NOPE:(h + 1) * A_NOPE].astype(k_ref.dtype)
        k_ref[:, lo + A_NOPE:lo + A_HEAD_PAD] = kr
    v_ref[...] = acc[:, A_HEADS * A_NOPE:].astype(v_ref.dtype)


def mla_kv(p_small, g, w_ukv_perm, rope_c, rope_s1, rope_s2):
    nt = p_small.shape[0]
    tm = 256
    kw = A_HEADS * A_HEAD_PAD
    vw = A_HEADS * A_VDIM
    tab = pl.BlockSpec((tm, LANES), lambda i: (i, 0))
    return pl.pallas_call(
        _mla_kv_kernel,
        grid=(nt // tm,),
        in_specs=[pl.BlockSpec((tm, A_KVLORA), lambda i: (i, A_QLORA // A_KVLORA)),
                  pl.BlockSpec((tm, LANES), lambda i: (i, (A_QLORA + A_KVLORA) // LANES)),
                  pl.BlockSpec((1, A_KVLORA), lambda i: (0, 0)),
                  pl.BlockSpec((A_KVLORA, A_HEADS * (A_NOPE + A_VDIM)), lambda i: (0, 0)),
                  tab, tab, tab],
        out_specs=[pl.BlockSpec((tm, kw), lambda i: (i, 0)),
                   pl.BlockSpec((tm, vw), lambda i: (i, 0))],
        out_shape=[jax.ShapeDtypeStruct((nt, kw), BF16),
                   jax.ShapeDtypeStruct((nt, vw), BF16)],
        compiler_params=_cparams(1, 48),
        name="mla_kv",
    )(p_small, p_small, g.reshape(1, A_KVLORA), w_ukv_perm, rope_c, rope_s1, rope_s2)


def _qk(q, k):
    return lax.dot_general(q, k, (((1,), (1,)), ((), ())), preferred_element_type=F32)


LOG2_E = 1.4426950408889634


def _flash_two_sets_kernel(q_ref, kl_ref, vl_ref, kc_ref, vc_ref, o_ref, *, scale, chunk):
    q = q_ref[...]
    c = scale * LOG2_E
    n_chunks = kl_ref.shape[0] // chunk
    keys = lambda j: kl_ref[j * chunk:(j + 1) * chunk, :]
    t = _qk(q, kc_ref[...]) * c
    t_next = _qk(q, keys(0)) * c
    m = jnp.max(t, axis=-1, keepdims=True)
    p = jnp.exp2(t - m)
    den = jnp.sum(p, axis=-1, keepdims=True)
    acc = jnp.dot(p.astype(BF16), vc_ref[...], preferred_element_type=F32)
    for j in range(n_chunks):
        t = t_next
        if j + 1 < n_chunks:
            t_next = _qk(q, keys(j + 1)) * c
        m_new = jnp.maximum(m, jnp.max(t, axis=-1, keepdims=True))
        alpha = jnp.exp2(m - m_new)
        p = jnp.exp2(t - m_new)
        den = alpha * den + jnp.sum(p, axis=-1, keepdims=True)
        acc = alpha * acc + jnp.dot(p.astype(BF16), vl_ref[j * chunk:(j + 1) * chunk, :],
                                    preferred_element_type=F32)
        m = m_new
    o_ref[...] = (acc / den).astype(o_ref.dtype)


def mla_latent_attention(q, k, v, n_batch, seq, ctx_len, tq=1024, chunk=512):
    ctx_blk0 = (n_batch * seq) // ctx_len
    nq = seq // tq
    scale = float((A_NOPE + A_ROPE) ** -0.5)
    return pl.pallas_call(
        functools.partial(_flash_two_sets_kernel, scale=scale, chunk=chunk),
        grid=(n_batch, A_HEADS, nq),
        in_specs=[pl.BlockSpec((tq, A_HEAD_PAD), lambda b, h, i: (b * nq + i, h)),
                  pl.BlockSpec((seq, A_HEAD_PAD), lambda b, h, i: (b, h)),
                  pl.BlockSpec((seq, A_VDIM), lambda b, h, i: (b, h)),
                  pl.BlockSpec((ctx_len, A_HEAD_PAD), lambda b, h, i: (ctx_blk0 + b, h)),
                  pl.BlockSpec((ctx_len, A_VDIM), lambda b, h, i: (ctx_blk0 + b, h))],
        out_specs=pl.BlockSpec((tq, A_VDIM), lambda b, h, i: (b * nq + i, h)),
        out_shape=jax.ShapeDtypeStruct((n_batch * seq, A_HEADS * A_VDIM), BF16),
        compiler_params=_cparams(3, 48),
        name="mla_latent_attention",
    )(q, k, v, k, v)


def _attn_one_set_kernel(q_ref, k_ref, v_ref, o_ref, *, scale):
    s = _qk(q_ref[...], k_ref[...]) * scale
    p = jnp.exp(s - jnp.max(s, axis=-1, keepdims=True))
    den = jnp.sum(p, axis=-1, keepdims=True)
    o = jnp.dot(p.astype(BF16), v_ref[...], preferred_element_type=F32)
    o_ref[...] = (o / den).astype(o_ref.dtype)


def context_attention(q, k, v, n_batch, seq, ctx_len, n_heads, dqk, dv, q_col0, k_col0, v_col0, scale):
    ctx_blk0 = (n_batch * seq) // ctx_len
    return pl.pallas_call(
        functools.partial(_attn_one_set_kernel, scale=scale),
        grid=(n_batch, n_heads),
        in_specs=[pl.BlockSpec((ctx_len, dqk), lambda b, h: (ctx_blk0 + b, q_col0 + h)),
                  pl.BlockSpec((ctx_len, dqk), lambda b, h: (ctx_blk0 + b, k_col0 + h)),
                  pl.BlockSpec((ctx_len, dv), lambda b, h: (ctx_blk0 + b, v_col0 + h))],
        out_specs=pl.BlockSpec((ctx_len, dv), lambda b, h: (b, h)),
        out_shape=jax.ShapeDtypeStruct((n_batch * ctx_len, n_heads * dv), BF16),
        compiler_params=_cparams(2, 32),
        name="context_attention",
    )(q, k, v)


def _na_kernel(q_ref, kl_ref, vl_ref, kc_ref, vc_ref, bias_ref, o_ref, *, rows, scale):
    r = pl.program_id(1)
    r0 = jnp.clip(r - WIN_H // 2, 0, rows - WIN_H)
    start = pl.multiple_of(r0 * GRID_W, GRID_W)
    win = WIN_H * GRID_W
    heads = [slice(h * B_DH, (h + 1) * B_DH) for h in range(B_HEADS)]
    scores = [(_qk(q_ref[:, cols], kl_ref[pl.ds(start, win), cols]), _qk(q_ref[:, cols], kc_ref[:, cols]))
              for cols in heads]
    probs = []
    for h, (s_l, s_c) in enumerate(scores):
        s_l = s_l * scale + bias_ref[0, h]
        s_c = s_c * scale
        m = jnp.maximum(jnp.max(s_l, axis=-1, keepdims=True), jnp.max(s_c, axis=-1, keepdims=True))
        p_l = jnp.exp(s_l - m)
        p_c = jnp.exp(s_c - m)
        den = jnp.sum(p_l, axis=-1, keepdims=True) + jnp.sum(p_c, axis=-1, keepdims=True)
        probs.append((p_l.astype(BF16), p_c.astype(BF16), den))
    outs = []
    for cols, (p_l, p_c, den) in zip(heads, probs):
        o = jnp.dot(p_l, vl_ref[pl.ds(start, win), cols], preferred_element_type=F32)
        o = o + jnp.dot(p_c, vc_ref[:, cols], preferred_element_type=F32)
        outs.append(o / den)
    o_ref[...] = jnp.concatenate(outs, axis=1).astype(o_ref.dtype)


def na_latent_attention(qkv, bias, n_batch, seq, ctx_len):
    rows = seq // GRID_W
    ctx_blk0 = (n_batch * seq) // ctx_len
    win = WIN_H * GRID_W

    def bias_idx(b, r):
        return (r - jnp.clip(r - WIN_H // 2, 0, rows - WIN_H), 0, 0, 0)

    once = pl.Buffered(1)
    return pl.pallas_call(
        functools.partial(_na_kernel, rows=rows, scale=float(B_DH ** -0.5)),
        grid=(n_batch, rows),
        in_specs=[pl.BlockSpec((GRID_W, B_WIDTH), lambda b, r: (b * rows + r, 0)),
                  pl.BlockSpec((seq, B_WIDTH), lambda b, r: (b, 1), pipeline_mode=once),
                  pl.BlockSpec((seq, B_WIDTH), lambda b, r: (b, 2), pipeline_mode=once),
                  pl.BlockSpec((ctx_len, B_WIDTH), lambda b, r: (ctx_blk0 + b, 1)),
                  pl.BlockSpec((ctx_len, B_WIDTH), lambda b, r: (ctx_blk0 + b, 2)),
                  pl.BlockSpec((1, B_HEADS, GRID_W, win), bias_idx)],
        out_specs=pl.BlockSpec((GRID_W, B_WIDTH), lambda b, r: (b * rows + r, 0)),
        out_shape=jax.ShapeDtypeStruct((n_batch * seq, B_WIDTH), BF16),
        compiler_params=_cparams(2, 40),
        name="na_latent_attention",
    )(qkv, qkv, qkv, qkv, qkv, bias)


def na_bias_table(rel_bias):
    rel = rel_bias.astype(F32)
    n_layers = rel.shape[0]
    by_row = jnp.stack([rel[:, :, WIN_H - 1 - d:2 * WIN_H - 1 - d, :] for d in range(WIN_H)], axis=2)
    pad = GRID_W - 1
    by_row = jnp.pad(by_row, ((0, 0), (0, 0), (0, 0), (0, 0), (pad, pad)))
    off = WIN_W - 1 + pad
    tab = jnp.stack([by_row[..., off - c:off - c + GRID_W] for c in range(GRID_W)], axis=4)
    c = jnp.arange(GRID_W)[:, None]
    kc = jnp.arange(GRID_W)[None, :]
    c0 = jnp.clip(c - WIN_W // 2, 0, GRID_W - WIN_W)
    inside = (kc >= c0) & (kc < c0 + WIN_W)
    tab = jnp.where(inside[None, None, None, None], tab, NEG_BIG)
    return tab.transpose(0, 2, 1, 4, 3, 5).reshape(n_layers, WIN_H, B_HEADS, GRID_W, WIN_H * GRID_W)


def _pool_kernel(x_ref, w_ref, sc_ref, o_ref, pad_ref, *, seq_len, chunk):
    g = pl.program_id(1)
    zeros = jnp.zeros((POOL_HALO, C_GW), F32)
    pad_ref[0:POOL_HALO, :] = zeros
    pad_ref[POOL_HALO + seq_len:2 * POOL_HALO + seq_len, :] = zeros
    pad_ref[POOL_HALO:POOL_HALO + seq_len, :] = x_ref[...]
    for gi, w in enumerate(POOL_SIZES):
        @pl.when(g == gi)
        def _(w=w):
            for c0 in range(0, seq_len, chunk):
                acc = None
                for k in range(-(w // 2), w - w // 2):
                    piece = pad_ref[POOL_HALO + c0 + k:POOL_HALO + c0 + k + chunk, :]
                    acc = piece if acc is None else acc + piece
                t = c0 + lax.broadcasted_iota(jnp.int32, (chunk, C_GW), 0)
                cnt = jnp.minimum(t + (w - w // 2), seq_len) - jnp.maximum(t - w // 2, 0)
                d = (acc / cnt.astype(F32) - x_ref[c0:c0 + chunk, :]).astype(BF16)
                y = jnp.dot(d, w_ref[0], preferred_element_type=F32) * sc_ref[...]
                o_ref[c0:c0 + chunk, :] = y.astype(o_ref.dtype)


def pool_mix(p_pool, pool_w, pool_scale, n_seq, seq_len, first_block):
    chunk = min(seq_len, 512)
    return pl.pallas_call(
        functools.partial(_pool_kernel, seq_len=seq_len, chunk=chunk),
        grid=(n_seq, len(POOL_SIZES)),
        in_specs=[pl.BlockSpec((seq_len, C_GW), lambda b, g: (first_block + b, g)),
                  pl.BlockSpec((1, C_GW, C_GW), lambda b, g: (g, 0, 0)),
                  pl.BlockSpec((1, C_GW), lambda b, g: (0, g))],
        out_specs=pl.BlockSpec((seq_len, C_GW), lambda b, g: (b, g)),
        out_shape=jax.ShapeDtypeStruct((n_seq * seq_len, C_WIDTH), BF16),
        scratch_shapes=[pltpu.VMEM((seq_len + 2 * POOL_HALO, C_GW), F32)],
        compiler_params=_cparams(2, 48),
        name="pool_mix",
    )(p_pool, pool_w, pool_scale.reshape(1, C_WIDTH))


_REGION = [(r1, r2) for r1 in range(P_TOPK) for r2 in range(P_TOPK) if (r1 + 1) * (r2 + 1) <= P_TOPK]


def _dominates(a, b):
    return a != b and a[0] <= b[0] and a[1] <= b[1]


def _router_kernel(h_ref, wq_ref, k1_ref, k2_ref, cnt_ref, a_ref, rk_ref, b_ref,
                   s1_scr, s2_scr, v1_scr, i1_scr, v2_scr, i2_scr, cr_scr, e1_scr, e2_scr):
    tm = h_ref.shape[1]
    half = P_HEADS * P_DK // 2
    q_t = jnp.dot(wq_ref[...], h_ref[...], preferred_element_type=F32)
    key = lax.broadcasted_iota(jnp.int32, (P_NKEYS, P_HEADS, tm), 0)
    s1_scr[...] = jnp.dot(k1_ref[...], q_t[:half].astype(BF16),
                          preferred_element_type=F32).reshape(P_NKEYS, P_HEADS, tm)
    s2_scr[...] = jnp.dot(k2_ref[...], q_t[half:].astype(BF16),
                          preferred_element_type=F32).reshape(P_NKEYS, P_HEADS, tm)

    def first_max(s):
        level = [(s[k], k) for k in range(P_NKEYS)]
        while len(level) > 1:
            nxt = []
            for (va, ia), (vb, ib) in zip(level[0::2], level[1::2]):
                take = vb > va
                nxt.append((jnp.where(take, vb, va), jnp.where(take, ib, ia)))
            level = nxt
        return level[0]

    def extract(r, carry):
        for s_scr, v_scr, i_scr in ((s1_scr, v1_scr, i1_scr), (s2_scr, v2_scr, i2_scr)):
            s = s_scr[...]
            m, idx = first_max(s)
            s_scr[...] = jnp.where(key == idx[None], -jnp.inf, s)
            v_scr[r] = m
            i_scr[r] = idx
        return carry

    lax.fori_loop(0, P_TOPK, extract, 0)

    v1 = [v1_scr[r] for r in range(P_TOPK)]
    v2 = [v2_scr[r] for r in range(P_TOPK)]
    sums = {c: v1[c[0]] + v2[c[1]] for c in _REGION}
    beaten = {c: float(sum(_dominates(o, c) for o in _REGION)) for c in _REGION}
    for ci, c in enumerate(_REGION):
        for o in _REGION[:ci]:
            if _dominates(o, c):
                continue
            o_first = (sums[o] >= sums[c]).astype(F32)
            beaten[c] = beaten[c] + o_first
            beaten[o] = beaten[o] + (1.0 - o_first)
    e1 = [jnp.exp(v1[r] - v1[0]) for r in range(P_TOPK)]
    e2 = [jnp.exp(v2[r] - v2[0]) for r in range(P_TOPK)]
    z = jnp.zeros_like(v1[0])
    counts = [jnp.zeros_like(v1[0]) for _ in range(P_TOPK)]
    for c in _REGION:
        chosen = beaten[c] < float(P_TOPK)
        z = z + jnp.where(chosen, e1[c[0]] * e2[c[1]], 0.0)
        counts[c[0]] = counts[c[0]] + jnp.where(chosen, 1.0, 0.0)
    inv_z = 1.0 / z
    for r in range(P_TOPK):
        cr_scr[r] = counts[r]
        e1_scr[r] = e1[r]
        e2_scr[r] = e2[r] * inv_z

    key2 = lax.broadcasted_iota(jnp.int32, (P_NKEYS, LANES), 0)

    def expand(n, carry, h):
        t0 = pl.multiple_of(n * LANES, LANES)
        row = lambda ref, r: ref[r, h:h + 1, pl.ds(t0, LANES)]
        cnt = jnp.zeros((P_NKEYS, LANES), F32)
        a = cnt
        for r in range(P_TOPK):
            hit = key2 == row(i1_scr, r)
            cnt = jnp.where(hit, row(cr_scr, r), cnt)
            a = jnp.where(hit, row(e1_scr, r), a)
        cnt_ref[h, :, pl.ds(t0, LANES)] = cnt
        a_ref[h, :, pl.ds(t0, LANES)] = a
        rk = jnp.full((P_NKEYS, LANES), float(P_TOPK), F32)
        b = jnp.zeros((P_NKEYS, LANES), F32)
        for r in range(P_TOPK):
            hit = key2 == row(i2_scr, r)
            rk = jnp.where(hit, float(r), rk)
            b = jnp.where(hit, row(e2_scr, r), b)
        rk_ref[h, :, pl.ds(t0, LANES)] = rk.astype(rk_ref.dtype)
        b_ref[h, :, pl.ds(t0, LANES)] = b.astype(b_ref.dtype)
        return carry

    for h in range(P_HEADS):
        lax.fori_loop(0, tm // LANES, functools.partial(expand, h=h), 0)


def peer_router(h_t, wq_t, k1_packed, k2_packed, tm=256):
    d, n = h_t.shape
    qw = wq_t.shape[0]
    kp = k1_packed.shape[0]
    tab = lambda dt: jax.ShapeDtypeStruct((P_HEADS, P_NKEYS, n), dt)
    tab_spec = pl.BlockSpec((P_HEADS, P_NKEYS, tm), lambda i: (0, 0, i))
    small = lambda dt: pltpu.VMEM((P_TOPK, P_HEADS, tm), dt)
    return pl.pallas_call(
        _router_kernel,
        grid=(n // tm,),
        in_specs=[pl.BlockSpec((d, tm), lambda i: (0, i)),
                  pl.BlockSpec((qw, d), lambda i: (0, 0), pipeline_mode=pl.Buffered(1)),
                  pl.BlockSpec((kp, kp), lambda i: (0, 0), pipeline_mode=pl.Buffered(1)),
                  pl.BlockSpec((kp, kp), lambda i: (0, 0), pipeline_mode=pl.Buffered(1))],
        out_specs=[tab_spec] * 4,
        out_shape=[tab(F32)] * 4,
        scratch_shapes=[pltpu.VMEM((P_NKEYS, P_HEADS, tm), F32), pltpu.VMEM((P_NKEYS, P_HEADS, tm), F32),
                        small(F32), small(jnp.int32), small(F32), small(jnp.int32),
                        small(F32), small(F32), small(F32)],
        compiler_params=_cparams(1, 52),
        name="peer_router",
    )(h_t, wq_t, k1_packed, k2_packed)


def _gelu_exact(x):
    return 0.5 * x * (1.0 + lax.erf(x * float(np.sqrt(0.5))))


def _peer_kernel(h_ref, u_ref, v_ref, cnt_ref, a_ref, rk_ref, b_ref, o_ref):
    e = pl.program_id(1)
    te = u_ref.shape[0]
    tm = h_ref.shape[1]

    def step(first):
        halves = [(c0, jnp.dot(u_ref[c0:c0 + te // 2, :], h_ref[...], preferred_element_type=F32))
                  for c0 in (0, te // 2)]
        rows_per_half = te // 2 // P_NKEYS
        n_slab = P_NKEYS // SUBLANES
        ga_rows = []
        for c0, pre in halves:
            i0 = e * (te // P_NKEYS) + c0 // P_NKEYS
            rows = [[(cnt_ref[h, pl.ds(i0 + k, 1), :], a_ref[h, pl.ds(i0 + k, 1), :]) for h in range(P_HEADS)]
                    for k in range(rows_per_half)]
            tiles = {}
            for t0 in range(0, tm, LANES):
                lanes = slice(t0, t0 + LANES)
                g = [[None] * n_slab for _ in range(rows_per_half)]
                for h in range(P_HEADS):
                    per_row = [[jnp.broadcast_to(x[:, lanes], (SUBLANES, LANES)) for x in rows[k][h]]
                               for k in range(rows_per_half)]
                    for s in range(n_slab):
                        slab = slice(s * SUBLANES, (s + 1) * SUBLANES)
                        rk = rk_ref[h, slab, lanes]
                        b = b_ref[h, slab, lanes]
                        for k, (cnt, a) in enumerate(per_row):
                            term = jnp.where(rk < cnt, b * a, jnp.zeros_like(b))
                            g[k][s] = term if g[k][s] is None else g[k][s] + term
                for k in range(rows_per_half):
                    for s in range(n_slab):
                        r = k * P_NKEYS + s * SUBLANES
                        tiles[(k, s, t0)] = g[k][s] * _gelu_exact(pre[r:r + SUBLANES, lanes])
            for k in range(rows_per_half):
                for s in range(n_slab):
                    ga_rows.append(jnp.concatenate([tiles[(k, s, t0)] for t0 in range(0, tm, LANES)], axis=1))
        ga = jnp.concatenate(ga_rows, axis=0).astype(BF16)
        contrib = lax.dot_general(v_ref[...], ga, (((0,), (0,)), ((), ())), preferred_element_type=F32)
        if first:
            o_ref[...] = contrib
        else:
            o_ref[...] += contrib

    pl.when(e == 0)(functools.partial(step, True))
    pl.when(e > 0)(functools.partial(step, False))


def peer_mixture(h_t, u_all, v_all, layer, tables, tm=512, te=512):
    d, n = h_t.shape
    tab_spec = pl.BlockSpec((P_HEADS, P_NKEYS, tm), lambda i, e: (0, 0, i), pipeline_mode=pl.Buffered(1))
    expert_tile = pl.BlockSpec((None, te, d), lambda i, e: (layer, e, 0))
    return pl.pallas_call(
        _peer_kernel,
        grid=(n // tm, u_all.shape[1] // te),
        in_specs=[pl.BlockSpec((d, tm), lambda i, e: (0, i), pipeline_mode=pl.Buffered(1)),
                  expert_tile, expert_tile] + [tab_spec] * 4,
        out_specs=pl.BlockSpec((d, tm), lambda i, e: (0, i)),
        out_shape=jax.ShapeDtypeStruct((d, n), F32),
        compiler_params=_cparams(2, 56),
        name="peer_mixture",
    )(h_t, u_all, v_all, *tables)


def _residual_t_ln_kernel(x_ref, y_ref, g_ref, ng_ref, sc_ref, sh_ref, o_ref, h_ref):
    x = x_ref[...] + g_ref[0] * y_ref[...].T
    o_ref[...] = x
    h_ref[...] = (_rms(x, ng_ref[...]) * (1.0 + sc_ref[0]) + sh_ref[0]).astype(h_ref.dtype)


def _residual_t_final_kernel(x_ref, y_ref, g_ref, ng_ref, o_ref):
    o_ref[...] = _rms(x_ref[...] + g_ref[0] * y_ref[...].T, ng_ref[...])


def residual_from_transposed(x, y_t, mod3, gate_chunk, n_rows, seq, n_batch, norm_g, next_mod3=None):
    d = x.shape[1]
    tm = 256
    seg = functools.partial(_segment, tile_rows=tm, seq=seq, n_batch=n_batch)
    row = pl.BlockSpec((tm, d), lambda i: (i, 0))
    in_specs = [row,
                pl.BlockSpec((d, tm), lambda i: (0, i)),
                pl.BlockSpec((1, 1, d), lambda i: (seg(i), 0, gate_chunk)),
                pl.BlockSpec((1, d), lambda i: (0, 0))]
    if next_mod3 is None:
        return pl.pallas_call(
            _residual_t_final_kernel,
            grid=(n_rows // tm,),
            in_specs=in_specs,
            out_specs=row,
            out_shape=jax.ShapeDtypeStruct((n_rows, d), F32),
            compiler_params=_cparams(1, 48),
            name="residual_final_norm",
        )(x, y_t, mod3, norm_g.reshape(1, d))
    return pl.pallas_call(
        _residual_t_ln_kernel,
        grid=(n_rows // tm,),
        in_specs=in_specs + [pl.BlockSpec((1, 1, d), lambda i: (seg(i), 0, 1)),
                             pl.BlockSpec((1, 1, d), lambda i: (seg(i), 0, 0))],
        out_specs=[row, row],
        out_shape=[jax.ShapeDtypeStruct((n_rows, d), F32), jax.ShapeDtypeStruct((n_rows, d), BF16)],
        compiler_params=_cparams(1, 48),
        name="residual_next_norm",
    )(x, y_t, mod3, norm_g.reshape(1, d), next_mod3, next_mod3)


def _rope_tables(seq, n_batch, n_ctx_rows):
    t = jnp.arange(seq)
    row = (t // GRID_W).astype(F32)
    col = (t % GRID_W).astype(F32)
    n_freq = A_ROPE // 4
    freqs = ROPE_THETA ** (-jnp.arange(n_freq, dtype=F32) / n_freq)
    ang = jnp.concatenate([row[:, None] * freqs, col[:, None] * freqs], axis=-1)
    cos, sin = jnp.cos(ang), jnp.sin(ang)
    zero = jnp.zeros_like(cos)
    pad = jnp.zeros((seq, LANES - A_ROPE), F32)
    c = jnp.concatenate([cos, cos, pad], axis=1)
    s1 = jnp.concatenate([-sin, zero, pad], axis=1)
    s2 = jnp.concatenate([zero, sin, pad], axis=1)
    ident = jnp.concatenate([jnp.ones((n_ctx_rows, A_ROPE), F32), jnp.zeros((n_ctx_rows, LANES - A_ROPE), F32)], 1)
    none = jnp.zeros((n_ctx_rows, LANES), F32)
    tile = lambda a, ctx: jnp.concatenate([jnp.tile(a, (n_batch, 1)), ctx], axis=0)
    return tile(c, ident), tile(s1, none), tile(s2, none)


def _pack_keys(k):
    h, nk, dk = k.shape
    eye = jnp.eye(h, dtype=k.dtype)
    blk = k.transpose(1, 0, 2)[:, :, None, :] * eye[None, :, :, None]
    return blk.reshape(nk * h, h * dk).astype(BF16)


def _prepare_layer(w_in, w_uq, w_ukv, w_br_a, w_br_b, w_br_c, w_out, peer_wq, peer_k1, peer_k2):
    d = w_in.shape[0]
    o_kr = A_QLORA + A_KVLORA
    o_q = o_kr + A_ROPE
    o_pool = o_q + 3 * B_WIDTH
    o_gate = o_pool + C_WIDTH
    w_small = jnp.concatenate([w_in[:, :o_q], jnp.zeros((d, LANES - A_ROPE), w_in.dtype)], axis=1)
    w_uq_pad = jnp.pad(w_uq.reshape(A_QLORA, A_HEADS, A_NOPE + A_ROPE),
                       ((0, 0), (0, 0), (0, A_HEAD_PAD - A_NOPE - A_ROPE))).reshape(A_QLORA, A_HEADS * A_HEAD_PAD)
    w_ukv_perm = w_ukv.reshape(A_KVLORA, A_HEADS, 2, A_NOPE).transpose(0, 2, 1, 3).reshape(A_KVLORA, -1)
    wq_t = peer_wq.reshape(d, P_HEADS, 2, P_DK // 2).transpose(2, 1, 3, 0).reshape(P_HEADS * P_DK, d)
    return dict(
        w_small=w_small.astype(BF16),
        w_qkv=w_in[:, o_q:o_pool].astype(BF16),
        w_pool=w_in[:, o_pool:o_gate].astype(BF16),
        w_gate=w_in[:, o_gate:].astype(BF16),
        w_uq=w_uq_pad.astype(BF16),
        w_ukv=w_ukv_perm.astype(BF16),
        w_br_a=w_br_a.astype(BF16), w_br_b=w_br_b.astype(BF16), w_br_c=w_br_c.astype(BF16),
        w_out=w_out.astype(BF16),
        wq_t=wq_t.astype(BF16),
        k1=_pack_keys(peer_k1), k2=_pack_keys(peer_k2),
    )


def kernel(x, c, ctx, c_ctx, w_ada, b_ada, norm1_g, norm2_g, w_in, q_norm_g, kv_norm_g, w_uq, w_ukv, na_rel_bias, pool_w, pool_scale, w_br_a, w_br_b, w_br_c, w_out, peer_wq, peer_k1, peer_k2, peer_u, peer_v, final_g):
    n_batch, seq, d = x.shape
    ctx_len = ctx.shape[1]
    depth = w_ada.shape[0]
    n_lat = n_batch * seq
    n_ctx = n_batch * ctx_len
    nt = n_lat + n_ctx
    ctx_blk0 = n_lat // ctx_len

    stream = jnp.concatenate([x.reshape(n_lat, d), ctx.reshape(n_ctx, d)], axis=0)
    mod_rows = 8
    cvec = jnp.concatenate([c, c_ctx[None], jnp.zeros((mod_rows - n_batch - 1, d), c.dtype)], axis=0)
    mod = ada_modulation(cvec, w_ada, b_ada)
    rope_c, rope_s1, rope_s2 = _rope_tables(seq, n_batch, n_ctx)
    na_bias = na_bias_table(na_rel_bias)
    mod3s = [mod[l].reshape(mod_rows, 1, 6 * d) for l in range(depth)]
    u_all = peer_u.astype(BF16)
    v_all = peer_v.astype(BF16)

    h = ln_modulate(stream, norm1_g[0], mod3s[0], 0, nt, seq, n_batch)
    for l in range(depth):
        ctx_out = l < depth - 1
        n_act = nt if ctx_out else n_lat
        wl = _prepare_layer(w_in[l], w_uq[l], w_ukv[l], w_br_a[l], w_br_b[l], w_br_c[l], w_out[l],
                            peer_wq[l], peer_k1[l], peer_k2[l])
        mod3 = mod3s[l]

        p_small = matmul(h, wl["w_small"], F32)
        qkv = matmul(h, wl["w_qkv"], BF16)
        p_pool = matmul(h, wl["w_pool"], F32)
        gates = matmul(h, wl["w_gate"], F32, act="sigmoid", n_rows=n_act)

        q = mla_q(p_small, q_norm_g[l], wl["w_uq"], rope_c, rope_s1, rope_s2)
        k, v = mla_kv(p_small, kv_norm_g[l], wl["w_ukv"], rope_c, rope_s1, rope_s2)
        a_br = mla_latent_attention(q, k, v, n_batch, seq, ctx_len)
        b_br = na_latent_attention(qkv, na_bias[l], n_batch, seq, ctx_len)
        pw = pool_w[l].astype(BF16)
        c_br = pool_mix(p_pool, pw, pool_scale[l], n_batch, seq, 0)
        if ctx_out:
            a_ctx = context_attention(q, k, v, n_batch, seq, ctx_len, A_HEADS, A_HEAD_PAD, A_VDIM, 0, 0, 0,
                                      float((A_NOPE + A_ROPE) ** -0.5))
            b_ctx = context_attention(qkv, qkv, qkv, n_batch, seq, ctx_len, B_HEADS, B_DH, B_DH,
                                      0, B_HEADS, 2 * B_HEADS, float(B_DH ** -0.5))
            c_ctx_br = pool_mix(p_pool, pw, pool_scale[l], n_batch, ctx_len, ctx_blk0)
            a_br = jnp.concatenate([a_br, a_ctx], axis=0)
            b_br = jnp.concatenate([b_br, b_ctx], axis=0)
            c_br = jnp.concatenate([c_br, c_ctx_br], axis=0)

        merged = merge_branches(a_br, b_br, c_br, wl["w_br_a"], wl["w_br_b"], wl["w_br_c"], gates)
        stream = matmul_gated_residual(merged, wl["w_out"], stream, mod3, 2, seq, n_batch)

        h2_t = ln_modulate(stream, norm2_g[l], mod3, 3, n_act, seq, n_batch, transpose=True)
        tables = peer_router(h2_t, wl["wq_t"], wl["k1"], wl["k2"])
        y_t = peer_mixture(h2_t, u_all, v_all, l, tables)
        if ctx_out:
            stream, h = residual_from_transposed(stream, y_t, mod3, 5, n_act, seq, n_batch,
                                                 norm1_g[l + 1], mod3s[l + 1])
        else:
            out = residual_from_transposed(stream, y_t, mod3, 5, n_act, seq, n_batch, final_g)

    return out.reshape(n_batch, seq, d)
```

```python
import functools

import numpy as np
import jax
import jax.numpy as jnp
from jax import lax
from jax.experimental import pallas as pl
from jax.experimental.pallas import tpu as pltpu

F32 = jnp.float32
BF16 = jnp.bfloat16

GRID_W = 64
EPS = 1e-6
ROPE_THETA = 10000.0

A_HEADS = 16
A_NOPE = 128
A_ROPE = 64
A_VDIM = 128
A_QLORA = 768
A_KVLORA = 256
A_HEAD_PAD = 256

B_HEADS = 8
B_DH = 128
B_WIDTH = B_HEADS * B_DH
WIN_H = 8
WIN_W = 16

C_WIDTH = 1024
POOL_SIZES = (2, 4, 8, 16)
C_GW = C_WIDTH // len(POOL_SIZES)
POOL_HALO = 8

P_HEADS = 8
P_NKEYS = 128
P_DK = 256
P_TOPK = 16

LANES = 128
SUBLANES = 8
GATE_DTYPE = BF16
GATE_SLAB = 16
SMALL_W = A_QLORA + A_KVLORA + LANES

NEG_BIG = -1e30
MIB = 1024 * 1024


def _cparams(n_axes, vmem_mib):
    return pltpu.CompilerParams(dimension_semantics=("arbitrary",) * n_axes,
                                vmem_limit_bytes=vmem_mib * MIB)


def _segment(row_tile, tile_rows, seq, n_batch):
    return jnp.minimum((row_tile * tile_rows) // seq, n_batch)


def _ada_kernel(c_ref, w_ref, b_ref, o_ref):
    c = c_ref[...]
    a = (c * jax.nn.sigmoid(c)).astype(BF16)
    o_ref[0] = jnp.dot(a, w_ref[0].astype(BF16), preferred_element_type=F32) + b_ref[0]


def ada_modulation(cvec, w_ada, b_ada):
    n_layers, d, n_out = w_ada.shape
    rows = cvec.shape[0]
    tn = 512
    return pl.pallas_call(
        _ada_kernel,
        grid=(n_layers, n_out // tn),
        in_specs=[pl.BlockSpec((rows, d), lambda l, j: (0, 0)),
                  pl.BlockSpec((1, d, tn), lambda l, j: (l, 0, j)),
                  pl.BlockSpec((1, 1, tn), lambda l, j: (l, 0, j))],
        out_specs=pl.BlockSpec((1, rows, tn), lambda l, j: (l, 0, j)),
        out_shape=jax.ShapeDtypeStruct((n_layers, rows, n_out), F32),
        compiler_params=_cparams(2, 40),
        name="ada_modulation",
    )(cvec, w_ada, b_ada.reshape(n_layers, 1, n_out))


def _rms(x, g):
    return x * lax.rsqrt(jnp.mean(x * x, axis=-1, keepdims=True) + EPS) * g


def _ln_mod_kernel(x_ref, g_ref, sc_ref, sh_ref, o_ref, *, transpose):
    h = _rms(x_ref[...], g_ref[...]) * (1.0 + sc_ref[0]) + sh_ref[0]
    if transpose:
        o_ref[...] = h.T.astype(o_ref.dtype)
    else:
        o_ref[...] = h.astype(o_ref.dtype)


def ln_modulate(x, g, mod3, shift_chunk, n_rows, seq, n_batch, transpose=False):
    d = x.shape[1]
    tm = 256
    seg = functools.partial(_segment, tile_rows=tm, seq=seq, n_batch=n_batch)
    if transpose:
        out_spec = pl.BlockSpec((d, tm), lambda i: (0, i))
        out_shape = jax.ShapeDtypeStruct((d, n_rows), BF16)
    else:
        out_spec = pl.BlockSpec((tm, d), lambda i: (i, 0))
        out_shape = jax.ShapeDtypeStruct((n_rows, d), BF16)
    return pl.pallas_call(
        functools.partial(_ln_mod_kernel, transpose=transpose),
        grid=(n_rows // tm,),
        in_specs=[pl.BlockSpec((tm, d), lambda i: (i, 0)),
                  pl.BlockSpec((1, d), lambda i: (0, 0)),
                  pl.BlockSpec((1, 1, d), lambda i: (seg(i), 0, shift_chunk + 1)),
                  pl.BlockSpec((1, 1, d), lambda i: (seg(i), 0, shift_chunk))],
        out_specs=out_spec,
        out_shape=out_shape,
        compiler_params=_cparams(1, 48),
        name="ln_modulate_t" if transpose else "ln_modulate",
    )(x, g.reshape(1, d), mod3, mod3)


def _mm_kernel(a_ref, b_ref, o_ref, *, act):
    acc = jnp.dot(a_ref[...], b_ref[...], preferred_element_type=F32)
    if act == "sigmoid":
        acc = jax.nn.sigmoid(acc)
    o_ref[...] = acc.astype(o_ref.dtype)


def matmul(a, b, out_dtype, act=None, n_rows=None, tn=1024, tm=512):
    m, k = a.shape
    m = n_rows or m
    n = b.shape[1]
    if n % tn:
        tn = n
    return pl.pallas_call(
        functools.partial(_mm_kernel, act=act),
        grid=(n // tn, m // tm),
        in_specs=[pl.BlockSpec((tm, k), lambda j, i: (i, 0)),
                  pl.BlockSpec((k, tn), lambda j, i: (0, j))],
        out_specs=pl.BlockSpec((tm, tn), lambda j, i: (i, j)),
        out_shape=jax.ShapeDtypeStruct((m, n), out_dtype),
        compiler_params=_cparams(2, 52),
        name="matmul_" + (act or "plain"),
    )(a, b)


def _mm_residual_kernel(a_ref, b_ref, x_ref, g_ref, o_ref):
    acc = jnp.dot(a_ref[...], b_ref[...], preferred_element_type=F32)
    o_ref[...] = x_ref[...] + g_ref[0] * acc


def matmul_gated_residual(a, b, x, mod3, gate_chunk, seq, n_batch, tn=1024, tm=512):
    m, k = a.shape
    n = b.shape[1]
    seg = functools.partial(_segment, tile_rows=tm, seq=seq, n_batch=n_batch)
    nj = n // tn
    return pl.pallas_call(
        _mm_residual_kernel,
        grid=(nj, m // tm),
        in_specs=[pl.BlockSpec((tm, k), lambda j, i: (i, 0)),
                  pl.BlockSpec((k, tn), lambda j, i: (0, j)),
                  pl.BlockSpec((tm, tn), lambda j, i: (i, j)),
                  pl.BlockSpec((1, 1, tn), lambda j, i: (seg(i), 0, gate_chunk * nj + j))],
        out_specs=pl.BlockSpec((tm, tn), lambda j, i: (i, j)),
        out_shape=jax.ShapeDtypeStruct((m, n), F32),
        compiler_params=_cparams(2, 52),
        name="matmul_gated_residual",
    )(a, b, x, mod3)


def _merge_kernel(a_ref, b_ref, c_ref, wa_ref, wb_ref, wc_ref, ga_ref, gb_ref, gc_ref, o_ref):
    m = ga_ref[...] * jnp.dot(a_ref[...], wa_ref[...], preferred_element_type=F32)
    m = m + gb_ref[...] * jnp.dot(b_ref[...], wb_ref[...], preferred_element_type=F32)
    m = m + gc_ref[...] * jnp.dot(c_ref[...], wc_ref[...], preferred_element_type=F32)
    o_ref[...] = m.astype(o_ref.dtype)


def merge_branches(a, b, c, wa, wb, wc, gates, tn=1024, tm=512):
    m = a.shape[0]
    n = wa.shape[1]
    nj = n // tn
    row = lambda j, i: (i, 0)
    col = lambda j, i: (0, j)
    return pl.pallas_call(
        _merge_kernel,
        grid=(nj, m // tm),
        in_specs=[pl.BlockSpec((tm, a.shape[1]), row),
                  pl.BlockSpec((tm, b.shape[1]), row),
                  pl.BlockSpec((tm, c.shape[1]), row),
                  pl.BlockSpec((wa.shape[0], tn), col),
                  pl.BlockSpec((wb.shape[0], tn), col),
                  pl.BlockSpec((wc.shape[0], tn), col),
                  pl.BlockSpec((tm, tn), lambda j, i: (i, j)),
                  pl.BlockSpec((tm, tn), lambda j, i: (i, nj + j)),
                  pl.BlockSpec((tm, tn), lambda j, i: (i, 2 * nj + j))],
        out_specs=pl.BlockSpec((tm, tn), lambda j, i: (i, j)),
        out_shape=jax.ShapeDtypeStruct((m, n), BF16),
        compiler_params=_cparams(2, 52),
        name="merge_branches",
    )(a, b, c, wa, wb, wc, gates, gates, gates)


def _rotate(v, c_ref, s1_ref, s2_ref):
    return (v * c_ref[...] + pltpu.roll(v, LANES - A_ROPE // 2, 1) * s1_ref[...]
            + pltpu.roll(v, A_ROPE // 2, 1) * s2_ref[...])


def _mla_q_kernel(cq_ref, g_ref, w_ref, c_ref, s1_ref, s2_ref, o_ref):
    y = _rms(cq_ref[...], g_ref[...]).astype(BF16)
    acc = jnp.dot(y, w_ref[...], preferred_element_type=F32)
    for h in range(A_HEADS):
        lo = h * A_HEAD_PAD
        o_ref[:, lo:lo + A_NOPE] = acc[:, lo:lo + A_NOPE].astype(o_ref.dtype)
        rot = _rotate(acc[:, lo + A_NOPE:lo + A_HEAD_PAD], c_ref, s1_ref, s2_ref)
        o_ref[:, lo + A_NOPE:lo + A_HEAD_PAD] = rot.astype(o_ref.dtype)


def mla_q(p_small, g, w_uq_pad, rope_c, rope_s1, rope_s2):
    nt = p_small.shape[0]
    tm = 256
    width = A_HEADS * A_HEAD_PAD
    tab = pl.BlockSpec((tm, LANES), lambda i: (i, 0))
    return pl.pallas_call(
        _mla_q_kernel,
        grid=(nt // tm,),
        in_specs=[pl.BlockSpec((tm, A_QLORA), lambda i: (i, 0)),
                  pl.BlockSpec((1, A_QLORA), lambda i: (0, 0)),
                  pl.BlockSpec((A_QLORA, width), lambda i: (0, 0)),
                  tab, tab, tab],
        out_specs=pl.BlockSpec((tm, width), lambda i: (i, 0)),
        out_shape=jax.ShapeDtypeStruct((nt, width), BF16),
        compiler_params=_cparams(1, 48),
        name="mla_q",
    )(p_small, g.reshape(1, A_QLORA), w_uq_pad, rope_c, rope_s1, rope_s2)


def _mla_kv_kernel(ckv_ref, kr_ref, g_ref, w_ref, c_ref, s1_ref, s2_ref, k_ref, v_ref):
    y = _rms(ckv_ref[...], g_ref[...]).astype(BF16)
    acc = jnp.dot(y, w_ref[...], preferred_element_type=F32)
    kr = _rotate(kr_ref[...], c_ref, s1_ref, s2_ref).astype(k_ref.dtype)
    for h in range(A_HEADS):
        lo = h * A_HEAD_PAD
        k_ref[:, lo:lo + A_NOPE] = acc[:, h * A_NOPE:(h + 1) * A_NOPE].astype(k_ref.dtype)
        k_ref[:, lo + A_NOPE:lo + A_HEAD_PAD] = kr
    v_ref[...] = acc[:, A_HEADS * A_NOPE:].astype(v_ref.dtype)


def mla_kv(p_small, g, w_ukv_perm, rope_c, rope_s1, rope_s2):
    nt = p_small.shape[0]
    tm = 256
    kw = A_HEADS * A_HEAD_PAD
    vw = A_HEADS * A_VDIM
    tab = pl.BlockSpec((tm, LANES), lambda i: (i, 0))
    return pl.pallas_call(
        _mla_kv_kernel,
        grid=(nt // tm,),
        in_specs=[pl.BlockSpec((tm, A_KVLORA), lambda i: (i, A_QLORA // A_KVLORA)),
                  pl.BlockSpec((tm, LANES), lambda i: (i, (A_QLORA + A_KVLORA) // LANES)),
                  pl.BlockSpec((1, A_KVLORA), lambda i: (0, 0)),
                  pl.BlockSpec((A_KVLORA, A_HEADS * (A_NOPE + A_VDIM)), lambda i: (0, 0)),
                  tab, tab, tab],
        out_specs=[pl.BlockSpec((tm, kw), lambda i: (i, 0)),
                   pl.BlockSpec((tm, vw), lambda i: (i, 0))],
        out_shape=[jax.ShapeDtypeStruct((nt, kw), BF16),
                   jax.ShapeDtypeStruct((nt, vw), BF16)],
        compiler_params=_cparams(1, 48),
        name="mla_kv",
    )(p_small, p_small, g.reshape(1, A_KVLORA), w_ukv_perm, rope_c, rope_s1, rope_s2)


def _qk(q, k):
    return lax.dot_general(q, k, (((1,), (1,)), ((), ())), preferred_element_type=F32)


LOG2_E = 1.4426950408889634


def _flash_two_sets_kernel(q_ref, kl_ref, vl_ref, kc_ref, vc_ref, o_ref, *, scale, chunk):
    q = q_ref[...]
    c = scale * LOG2_E
    n_chunks = kl_ref.shape[0] // chunk
    keys = lambda j: kl_ref[j * chunk:(j + 1) * chunk, :]
    t = _qk(q, kc_ref[...]) * c
    t_next = _qk(q, keys(0)) * c
    m = jnp.max(t, axis=-1, keepdims=True)
    p = jnp.exp2(t - m)
    den = jnp.sum(p, axis=-1, keepdims=True)
    acc = jnp.dot(p.astype(BF16), vc_ref[...], preferred_element_type=F32)
    for j in range(n_chunks):
        t = t_next
        if j + 1 < n_chunks:
            t_next = _qk(q, keys(j + 1)) * c
        m_new = jnp.maximum(m, jnp.max(t, axis=-1, keepdims=True))
        alpha = jnp.exp2(m - m_new)
        p = jnp.exp2(t - m_new)
        den = alpha * den + jnp.sum(p, axis=-1, keepdims=True)
        acc = alpha * acc + jnp.dot(p.astype(BF16), vl_ref[j * chunk:(j + 1) * chunk, :],
                                    preferred_element_type=F32)
        m = m_new
    o_ref[...] = (acc / den).astype(o_ref.dtype)


def mla_latent_attention(q, k, v, n_batch, seq, ctx_len, tq=1024, chunk=512):
    ctx_blk0 = (n_batch * seq) // ctx_len
    nq = seq // tq
    scale = float((A_NOPE + A_ROPE) ** -0.5)
    return pl.pallas_call(
        functools.partial(_flash_two_sets_kernel, scale=scale, chunk=chunk),
        grid=(n_batch, A_HEADS, nq),
        in_specs=[pl.BlockSpec((tq, A_HEAD_PAD), lambda b, h, i: (b * nq + i, h)),
                  pl.BlockSpec((seq, A_HEAD_PAD), lambda b, h, i: (b, h)),
                  pl.BlockSpec((seq, A_VDIM), lambda b, h, i: (b, h)),
                  pl.BlockSpec((ctx_len, A_HEAD_PAD), lambda b, h, i: (ctx_blk0 + b, h)),
                  pl.BlockSpec((ctx_len, A_VDIM), lambda b, h, i: (ctx_blk0 + b, h))],
        out_specs=pl.BlockSpec((tq, A_VDIM), lambda b, h, i: (b * nq + i, h)),
        out_shape=jax.ShapeDtypeStruct((n_batch * seq, A_HEADS * A_VDIM), BF16),
        compiler_params=_cparams(3, 48),
        name="mla_latent_attention",
    )(q, k, v, k, v)


def _attn_one_set_kernel(q_ref, k_ref, v_ref, o_ref, *, scale):
    s = _qk(q_ref[...], k_ref[...]) * scale
    p = jnp.exp(s - jnp.max(s, axis=-1, keepdims=True))
    den = jnp.sum(p, axis=-1, keepdims=True)
    o = jnp.dot(p.astype(BF16), v_ref[...], preferred_element_type=F32)
    o_ref[...] = (o / den).astype(o_ref.dtype)


def context_attention(q, k, v, n_batch, seq, ctx_len, n_heads, dqk, dv, q_col0, k_col0, v_col0, scale):
    ctx_blk0 = (n_batch * seq) // ctx_len
    return pl.pallas_call(
        functools.partial(_attn_one_set_kernel, scale=scale),
        grid=(n_batch, n_heads),
        in_specs=[pl.BlockSpec((ctx_len, dqk), lambda b, h: (ctx_blk0 + b, q_col0 + h)),
                  pl.BlockSpec((ctx_len, dqk), lambda b, h: (ctx_blk0 + b, k_col0 + h)),
                  pl.BlockSpec((ctx_len, dv), lambda b, h: (ctx_blk0 + b, v_col0 + h))],
        out_specs=pl.BlockSpec((ctx_len, dv), lambda b, h: (b, h)),
        out_shape=jax.ShapeDtypeStruct((n_batch * ctx_len, n_heads * dv), BF16),
        compiler_params=_cparams(2, 32),
        name="context_attention",
    )(q, k, v)


def _na_kernel(q_ref, kl_ref, vl_ref, kc_ref, vc_ref, bias_ref, o_ref, *, rows, scale):
    r = pl.program_id(1)
    r0 = jnp.clip(r - WIN_H // 2, 0, rows - WIN_H)
    start = pl.multiple_of(r0 * GRID_W, GRID_W)
    win = WIN_H * GRID_W
    heads = [slice(h * B_DH, (h + 1) * B_DH) for h in range(B_HEADS)]
    scores = [(_qk(q_ref[:, cols], kl_ref[pl.ds(start, win), cols]), _qk(q_ref[:, cols], kc_ref[:, cols]))
              for cols in heads]
    probs = []
    for h, (s_l, s_c) in enumerate(scores):
        s_l = s_l * scale + bias_ref[0, h]
        s_c = s_c * scale
        m = jnp.maximum(jnp.max(s_l, axis=-1, keepdims=True), jnp.max(s_c, axis=-1, keepdims=True))
        p_l = jnp.exp(s_l - m)
        p_c = jnp.exp(s_c - m)
        den = jnp.sum(p_l, axis=-1, keepdims=True) + jnp.sum(p_c, axis=-1, keepdims=True)
        probs.append((p_l.astype(BF16), p_c.astype(BF16), den))
    outs = []
    for cols, (p_l, p_c, den) in zip(heads, probs):
        o = jnp.dot(p_l, vl_ref[pl.ds(start, win), cols], preferred_element_type=F32)
        o = o + jnp.dot(p_c, vc_ref[:, cols], preferred_element_type=F32)
        outs.append(o / den)
    o_ref[...] = jnp.concatenate(outs, axis=1).astype(o_ref.dtype)


def na_latent_attention(qkv, bias, n_batch, seq, ctx_len):
    rows = seq // GRID_W
    ctx_blk0 = (n_batch * seq) // ctx_len
    win = WIN_H * GRID_W

    def bias_idx(b, r):
        return (r - jnp.clip(r - WIN_H // 2, 0, rows - WIN_H), 0, 0, 0)

    once = pl.Buffered(1)
    return pl.pallas_call(
        functools.partial(_na_kernel, rows=rows, scale=float(B_DH ** -0.5)),
        grid=(n_batch, rows),
        in_specs=[pl.BlockSpec((GRID_W, B_WIDTH), lambda b, r: (b * rows + r, 0)),
                  pl.BlockSpec((seq, B_WIDTH), lambda b, r: (b, 1), pipeline_mode=once),
                  pl.BlockSpec((seq, B_WIDTH), lambda b, r: (b, 2), pipeline_mode=once),
                  pl.BlockSpec((ctx_len, B_WIDTH), lambda b, r: (ctx_blk0 + b, 1)),
                  pl.BlockSpec((ctx_len, B_WIDTH), lambda b, r: (ctx_blk0 + b, 2)),
                  pl.BlockSpec((1, B_HEADS, GRID_W, win), bias_idx)],
        out_specs=pl.BlockSpec((GRID_W, B_WIDTH), lambda b, r: (b * rows + r, 0)),
        out_shape=jax.ShapeDtypeStruct((n_batch * seq, B_WIDTH), BF16),
        compiler_params=_cparams(2, 40),
        name="na_latent_attention",
    )(qkv, qkv, qkv, qkv, qkv, bias)


def na_bias_table(rel_bias):
    rel = rel_bias.astype(F32)
    n_layers = rel.shape[0]
    by_row = jnp.stack([rel[:, :, WIN_H - 1 - d:2 * WIN_H - 1 - d, :] for d in range(WIN_H)], axis=2)
    pad = GRID_W - 1
    by_row = jnp.pad(by_row, ((0, 0), (0, 0), (0, 0), (0, 0), (pad, pad)))
    off = WIN_W - 1 + pad
    tab = jnp.stack([by_row[..., off - c:off - c + GRID_W] for c in range(GRID_W)], axis=4)
    c = jnp.arange(GRID_W)[:, None]
    kc = jnp.arange(GRID_W)[None, :]
    c0 = jnp.clip(c - WIN_W // 2, 0, GRID_W - WIN_W)
    inside = (kc >= c0) & (kc < c0 + WIN_W)
    tab = jnp.where(inside[None, None, None, None], tab, NEG_BIG)
    return tab.transpose(0, 2, 1, 4, 3, 5).reshape(n_layers, WIN_H, B_HEADS, GRID_W, WIN_H * GRID_W)


def _pool_kernel(x_ref, w_ref, sc_ref, o_ref, pad_ref, *, seq_len, chunk):
    g = pl.program_id(1)
    zeros = jnp.zeros((POOL_HALO, C_GW), F32)
    pad_ref[0:POOL_HALO, :] = zeros
    pad_ref[POOL_HALO + seq_len:2 * POOL_HALO + seq_len, :] = zeros
    pad_ref[POOL_HALO:POOL_HALO + seq_len, :] = x_ref[...]
    for gi, w in enumerate(POOL_SIZES):
        @pl.when(g == gi)
        def _(w=w):
            for c0 in range(0, seq_len, chunk):
                acc = None
                for k in range(-(w // 2), w - w // 2):
                    piece = pad_ref[POOL_HALO + c0 + k:POOL_HALO + c0 + k + chunk, :]
                    acc = piece if acc is None else acc + piece
                t = c0 + lax.broadcasted_iota(jnp.int32, (chunk, C_GW), 0)
                cnt = jnp.minimum(t + (w - w // 2), seq_len) - jnp.maximum(t - w // 2, 0)
                d = (acc / cnt.astype(F32) - x_ref[c0:c0 + chunk, :]).astype(BF16)
                y = jnp.dot(d, w_ref[0], preferred_element_type=F32) * sc_ref[...]
                o_ref[c0:c0 + chunk, :] = y.astype(o_ref.dtype)


def pool_mix(p_pool, pool_w, pool_scale, n_seq, seq_len, first_block):
    chunk = min(seq_len, 512)
    return pl.pallas_call(
        functools.partial(_pool_kernel, seq_len=seq_len, chunk=chunk),
        grid=(n_seq, len(POOL_SIZES)),
        in_specs=[pl.BlockSpec((seq_len, C_GW), lambda b, g: (first_block + b, g)),
                  pl.BlockSpec((1, C_GW, C_GW), lambda b, g: (g, 0, 0)),
                  pl.BlockSpec((1, C_GW), lambda b, g: (0, g))],
        out_specs=pl.BlockSpec((seq_len, C_GW), lambda b, g: (b, g)),
        out_shape=jax.ShapeDtypeStruct((n_seq * seq_len, C_WIDTH), BF16),
        scratch_shapes=[pltpu.VMEM((seq_len + 2 * POOL_HALO, C_GW), F32)],
        compiler_params=_cparams(2, 48),
        name="pool_mix",
    )(p_pool, pool_w, pool_scale.reshape(1, C_WIDTH))


_REGION = [(r1, r2) for r1 in range(P_TOPK) for r2 in range(P_TOPK) if (r1 + 1) * (r2 + 1) <= P_TOPK]


def _dominates(a, b):
    return a != b and a[0] <= b[0] and a[1] <= b[1]


def _router_kernel(h_ref, wq_ref, k1_ref, k2_ref, cnt_ref, a_ref, rk_ref, b_ref,
                   s1_scr, s2_scr, v1_scr, i1_scr, v2_scr, i2_scr, cr_scr, e1_scr, e2_scr):
    tm = h_ref.shape[1]
    half = P_HEADS * P_DK // 2
    q_t = jnp.dot(wq_ref[...], h_ref[...], preferred_element_type=F32)
    key = lax.broadcasted_iota(jnp.int32, (P_NKEYS, P_HEADS, tm), 0)
    s1_scr[...] = jnp.dot(k1_ref[...], q_t[:half].astype(BF16),
                          preferred_element_type=F32).reshape(P_NKEYS, P_HEADS, tm)
    s2_scr[...] = jnp.dot(k2_ref[...], q_t[half:].astype(BF16),
                          preferred_element_type=F32).reshape(P_NKEYS, P_HEADS, tm)

    def first_max(s):
        level = [(s[k], k) for k in range(P_NKEYS)]
        while len(level) > 1:
            nxt = []
            for (va, ia), (vb, ib) in zip(level[0::2], level[1::2]):
                take = vb > va
                nxt.append((jnp.where(take, vb, va), jnp.where(take, ib, ia)))
            level = nxt
        return level[0]

    def extract(r, carry):
        for s_scr, v_scr, i_scr in ((s1_scr, v1_scr, i1_scr), (s2_scr, v2_scr, i2_scr)):
            s = s_scr[...]
            m, idx = first_max(s)
            s_scr[...] = jnp.where(key == idx[None], -jnp.inf, s)
            v_scr[r] = m
            i_scr[r] = idx
        return carry

    lax.fori_loop(0, P_TOPK, extract, 0)

    v1 = [v1_scr[r] for r in range(P_TOPK)]
    v2 = [v2_scr[r] for r in range(P_TOPK)]
    sums = {c: v1[c[0]] + v2[c[1]] for c in _REGION}
    beaten = {c: float(sum(_dominates(o, c) for o in _REGION)) for c in _REGION}
    for ci, c in enumerate(_REGION):
        for o in _REGION[:ci]:
            if _dominates(o, c):
                continue
            o_first = (sums[o] >= sums[c]).astype(F32)
            beaten[c] = beaten[c] + o_first
            beaten[o] = beaten[o] + (1.0 - o_first)
    e1 = [jnp.exp(v1[r] - v1[0]) for r in range(P_TOPK)]
    e2 = [jnp.exp(v2[r] - v2[0]) for r in range(P_TOPK)]
    z = jnp.zeros_like(v1[0])
    counts = [jnp.zeros_like(v1[0]) for _ in range(P_TOPK)]
    for c in _REGION:
        chosen = beaten[c] < float(P_TOPK)
        z = z + jnp.where(chosen, e1[c[0]] * e2[c[1]], 0.0)
        counts[c[0]] = counts[c[0]] + jnp.where(chosen, 1.0, 0.0)
    inv_z = 1.0 / z
    for r in range(P_TOPK):
        cr_scr[r] = counts[r]
        e1_scr[r] = e1[r]
        e2_scr[r] = e2[r] * inv_z

    key2 = lax.broadcasted_iota(jnp.int32, (P_NKEYS, LANES), 0)

    def expand(n, carry, h):
        t0 = pl.multiple_of(n * LANES, LANES)
        row = lambda ref, r: ref[r, h:h + 1, pl.ds(t0, LANES)]
        cnt = jnp.zeros((P_NKEYS, LANES), F32)
        a = cnt
        for r in range(P_TOPK):
            hit = key2 == row(i1_scr, r)
            cnt = jnp.where(hit, row(cr_scr, r), cnt)
            a = jnp.where(hit, row(e1_scr, r), a)
        cnt_ref[h, :, pl.ds(t0, LANES)] = cnt
        a_ref[h, :, pl.ds(t0, LANES)] = a
        rk = jnp.full((P_NKEYS, LANES), float(P_TOPK), F32)
        b = jnp.zeros((P_NKEYS, LANES), F32)
        for r in range(P_TOPK):
            hit = key2 == row(i2_scr, r)
            rk = jnp.where(hit, float(r), rk)
            b = jnp.where(hit, row(e2_scr, r), b)
        rk_ref[h, :, pl.ds(t0, LANES)] = rk.astype(rk_ref.dtype)
        b_ref[h, :, pl.ds(t0, LANES)] = b.astype(b_ref.dtype)
        return carry

    for h in range(P_HEADS):
        lax.fori_loop(0, tm // LANES, functools.partial(expand, h=h), 0)


def peer_router(h_t, wq_t, k1_packed, k2_packed, tm=256):
    d, n = h_t.shape
    qw = wq_t.shape[0]
    kp = k1_packed.shape[0]
    tab = lambda dt: jax.ShapeDtypeStruct((P_HEADS, P_NKEYS, n), dt)
    tab_spec = pl.BlockSpec((P_HEADS, P_NKEYS, tm), lambda i: (0, 0, i))
    small = lambda dt: pltpu.VMEM((P_TOPK, P_HEADS, tm), dt)
    return pl.pallas_call(
        _router_kernel,
        grid=(n // tm,),
        in_specs=[pl.BlockSpec((d, tm), lambda i: (0, i)),
                  pl.BlockSpec((qw, d), lambda i: (0, 0), pipeline_mode=pl.Buffered(1)),
                  pl.BlockSpec((kp, kp), lambda i: (0, 0), pipeline_mode=pl.Buffered(1)),
                  pl.BlockSpec((kp, kp), lambda i: (0, 0), pipeline_mode=pl.Buffered(1))],
        out_specs=[tab_spec] * 4,
        out_shape=[tab(F32), tab(F32), tab(GATE_DTYPE), tab(GATE_DTYPE)],
        scratch_shapes=[pltpu.VMEM((P_NKEYS, P_HEADS, tm), F32), pltpu.VMEM((P_NKEYS, P_HEADS, tm), F32),
                        small(F32), small(jnp.int32), small(F32), small(jnp.int32),
                        small(F32), small(F32), small(F32)],
        compiler_params=_cparams(1, 52),
        name="peer_router",
    )(h_t, wq_t, k1_packed, k2_packed)


def _gelu_exact(x):
    return 0.5 * x * (1.0 + lax.erf(x * float(np.sqrt(0.5))))


def _peer_kernel(h_ref, u_ref, v_ref, cnt_ref, a_ref, rk_ref, b_ref, o_ref):
    e = pl.program_id(1)
    te = u_ref.shape[0]
    tm = h_ref.shape[1]

    def step(first):
        halves = [(c0, jnp.dot(u_ref[c0:c0 + te // 2, :], h_ref[...], preferred_element_type=F32))
                  for c0 in (0, te // 2)]
        rows_per_half = te // 2 // P_NKEYS
        n_slab = P_NKEYS // GATE_SLAB
        ga_rows = []
        for c0, pre in halves:
            i0 = e * (te // P_NKEYS) + c0 // P_NKEYS
            rows = [[(cnt_ref[h, pl.ds(i0 + k, 1), :], a_ref[h, pl.ds(i0 + k, 1), :]) for h in range(P_HEADS)]
                    for k in range(rows_per_half)]
            tiles = {}
            for t0 in range(0, tm, LANES):
                lanes = slice(t0, t0 + LANES)
                g = [[None] * n_slab for _ in range(rows_per_half)]
                for h in range(P_HEADS):
                    per_row = [[jnp.broadcast_to(x[:, lanes].astype(GATE_DTYPE), (GATE_SLAB, LANES)) for x in rows[k][h]]
                               for k in range(rows_per_half)]
                    for s in range(n_slab):
                        slab = slice(s * GATE_SLAB, (s + 1) * GATE_SLAB)
                        rk = rk_ref[h, slab, lanes]
                        b = b_ref[h, slab, lanes]
                        for k, (cnt, a) in enumerate(per_row):
                            term = jnp.where(rk < cnt, b * a, jnp.zeros_like(b))
                            g[k][s] = term if g[k][s] is None else g[k][s] + term
                for k in range(rows_per_half):
                    for s in range(n_slab):
                        r = k * P_NKEYS + s * GATE_SLAB
                        tiles[(k, s, t0)] = g[k][s].astype(F32) * _gelu_exact(pre[r:r + GATE_SLAB, lanes])
            for k in range(rows_per_half):
                for s in range(n_slab):
                    ga_rows.append(jnp.concatenate([tiles[(k, s, t0)] for t0 in range(0, tm, LANES)], axis=1))
        ga = jnp.concatenate(ga_rows, axis=0).astype(BF16)
        contrib = lax.dot_general(v_ref[...], ga, (((0,), (0,)), ((), ())), preferred_element_type=F32)
        if first:
            o_ref[...] = contrib
        else:
            o_ref[...] += contrib

    pl.when(e == 0)(functools.partial(step, True))
    pl.when(e > 0)(functools.partial(step, False))


def peer_mixture(h_t, u_all, v_all, layer, tables, tm=512, te=512):
    d, n = h_t.shape
    tab_spec = pl.BlockSpec((P_HEADS, P_NKEYS, tm), lambda i, e: (0, 0, i), pipeline_mode=pl.Buffered(1))
    expert_tile = pl.BlockSpec((None, te, d), lambda i, e: (layer, e, 0))
    return pl.pallas_call(
        _peer_kernel,
        grid=(n // tm, u_all.shape[1] // te),
        in_specs=[pl.BlockSpec((d, tm), lambda i, e: (0, i), pipeline_mode=pl.Buffered(1)),
                  expert_tile, expert_tile] + [tab_spec] * 4,
        out_specs=pl.BlockSpec((d, tm), lambda i, e: (0, i)),
        out_shape=jax.ShapeDtypeStruct((d, n), F32),
        compiler_params=_cparams(2, 56),
        name="peer_mixture",
    )(h_t, u_all, v_all, *tables)


def _residual_t_ln_kernel(x_ref, y_ref, g_ref, ng_ref, sc_ref, sh_ref, o_ref, h_ref):
    x = x_ref[...] + g_ref[0] * y_ref[...].T
    o_ref[...] = x
    h_ref[...] = (_rms(x, ng_ref[...]) * (1.0 + sc_ref[0]) + sh_ref[0]).astype(h_ref.dtype)


def _residual_t_final_kernel(x_ref, y_ref, g_ref, ng_ref, o_ref):
    o_ref[...] = _rms(x_ref[...] + g_ref[0] * y_ref[...].T, ng_ref[...])


def residual_from_transposed(x, y_t, mod3, gate_chunk, n_rows, seq, n_batch, norm_g, next_mod3=None):
    d = x.shape[1]
    tm = 256
    seg = functools.partial(_segment, tile_rows=tm, seq=seq, n_batch=n_batch)
    row = pl.BlockSpec((tm, d), lambda i: (i, 0))
    in_specs = [row,
                pl.BlockSpec((d, tm), lambda i: (0, i)),
                pl.BlockSpec((1, 1, d), lambda i: (seg(i), 0, gate_chunk)),
                pl.BlockSpec((1, d), lambda i: (0, 0))]
    if next_mod3 is None:
        return pl.pallas_call(
            _residual_t_final_kernel,
            grid=(n_rows // tm,),
            in_specs=in_specs,
            out_specs=row,
            out_shape=jax.ShapeDtypeStruct((n_rows, d), F32),
            compiler_params=_cparams(1, 48),
            name="residual_final_norm",
        )(x, y_t, mod3, norm_g.reshape(1, d))
    return pl.pallas_call(
        _residual_t_ln_kernel,
        grid=(n_rows // tm,),
        in_specs=in_specs + [pl.BlockSpec((1, 1, d), lambda i: (seg(i), 0, 1)),
                             pl.BlockSpec((1, 1, d), lambda i: (seg(i), 0, 0))],
        out_specs=[row, row],
        out_shape=[jax.ShapeDtypeStruct((n_rows, d), F32), jax.ShapeDtypeStruct((n_rows, d), BF16)],
        compiler_params=_cparams(1, 48),
        name="residual_next_norm",
    )(x, y_t, mod3, norm_g.reshape(1, d), next_mod3, next_mod3)


def _rope_tables(seq, n_batch, n_ctx_rows):
    t = jnp.arange(seq)
    row = (t // GRID_W).astype(F32)
    col = (t % GRID_W).astype(F32)
    n_freq = A_ROPE // 4
    freqs = ROPE_THETA ** (-jnp.arange(n_freq, dtype=F32) / n_freq)
    ang = jnp.concatenate([row[:, None] * freqs, col[:, None] * freqs], axis=-1)
    cos, sin = jnp.cos(ang), jnp.sin(ang)
    zero = jnp.zeros_like(cos)
    pad = jnp.zeros((seq, LANES - A_ROPE), F32)
    c = jnp.concatenate([cos, cos, pad], axis=1)
    s1 = jnp.concatenate([-sin, zero, pad], axis=1)
    s2 = jnp.concatenate([zero, sin, pad], axis=1)
    ident = jnp.concatenate([jnp.ones((n_ctx_rows, A_ROPE), F32), jnp.zeros((n_ctx_rows, LANES - A_ROPE), F32)], 1)
    none = jnp.zeros((n_ctx_rows, LANES), F32)
    tile = lambda a, ctx: jnp.concatenate([jnp.tile(a, (n_batch, 1)), ctx], axis=0)
    return tile(c, ident), tile(s1, none), tile(s2, none)


def _pack_keys(k):
    h, nk, dk = k.shape
    eye = jnp.eye(h, dtype=k.dtype)
    blk = k.transpose(1, 0, 2)[:, :, None, :] * eye[None, :, :, None]
    return blk.reshape(nk * h, h * dk).astype(BF16)


def _prepare_layer(w_in, w_uq, w_ukv, w_br_a, w_br_b, w_br_c, w_out, peer_wq, peer_k1, peer_k2):
    d = w_in.shape[0]
    o_kr = A_QLORA + A_KVLORA
    o_q = o_kr + A_ROPE
    o_pool = o_q + 3 * B_WIDTH
    o_gate = o_pool + C_WIDTH
    w_small = jnp.concatenate([w_in[:, :o_q], jnp.zeros((d, LANES - A_ROPE), w_in.dtype)], axis=1)
    w_uq_pad = jnp.pad(w_uq.reshape(A_QLORA, A_HEADS, A_NOPE + A_ROPE),
                       ((0, 0), (0, 0), (0, A_HEAD_PAD - A_NOPE - A_ROPE))).reshape(A_QLORA, A_HEADS * A_HEAD_PAD)
    w_ukv_perm = w_ukv.reshape(A_KVLORA, A_HEADS, 2, A_NOPE).transpose(0, 2, 1, 3).reshape(A_KVLORA, -1)
    wq_t = peer_wq.reshape(d, P_HEADS, 2, P_DK // 2).transpose(2, 1, 3, 0).reshape(P_HEADS * P_DK, d)
    return dict(
        w_small=w_small.astype(BF16),
        w_qkv=w_in[:, o_q:o_pool].astype(BF16),
        w_pool=w_in[:, o_pool:o_gate].astype(BF16),
        w_gate=w_in[:, o_gate:].astype(BF16),
        w_uq=w_uq_pad.astype(BF16),
        w_ukv=w_ukv_perm.astype(BF16),
        w_br_a=w_br_a.astype(BF16), w_br_b=w_br_b.astype(BF16), w_br_c=w_br_c.astype(BF16),
        w_out=w_out.astype(BF16),
        wq_t=wq_t.astype(BF16),
        k1=_pack_keys(peer_k1), k2=_pack_keys(peer_k2),
    )


def kernel(x, c, ctx, c_ctx, w_ada, b_ada, norm1_g, norm2_g, w_in, q_norm_g, kv_norm_g, w_uq, w_ukv, na_rel_bias, pool_w, pool_scale, w_br_a, w_br_b, w_br_c, w_out, peer_wq, peer_k1, peer_k2, peer_u, peer_v, final_g):
    n_batch, seq, d = x.shape
    ctx_len = ctx.shape[1]
    depth = w_ada.shape[0]
    n_lat = n_batch * seq
    n_ctx = n_batch * ctx_len
    nt = n_lat + n_ctx
    ctx_blk0 = n_lat // ctx_len

    stream = jnp.concatenate([x.reshape(n_lat, d), ctx.reshape(n_ctx, d)], axis=0)
    mod_rows = 8
    cvec = jnp.concatenate([c, c_ctx[None], jnp.zeros((mod_rows - n_batch - 1, d), c.dtype)], axis=0)
    mod = ada_modulation(cvec, w_ada, b_ada)
    rope_c, rope_s1, rope_s2 = _rope_tables(seq, n_batch, n_ctx)
    na_bias = na_bias_table(na_rel_bias)
    mod3s = [mod[l].reshape(mod_rows, 1, 6 * d) for l in range(depth)]
    u_all = peer_u.astype(BF16)
    v_all = peer_v.astype(BF16)

    h = ln_modulate(stream, norm1_g[0], mod3s[0], 0, nt, seq, n_batch)
    for l in range(depth):
        ctx_out = l < depth - 1
        n_act = nt if ctx_out else n_lat
        wl = _prepare_layer(w_in[l], w_uq[l], w_ukv[l], w_br_a[l], w_br_b[l], w_br_c[l], w_out[l],
                            peer_wq[l], peer_k1[l], peer_k2[l])
        mod3 = mod3s[l]

        p_small = matmul(h, wl["w_small"], F32)
        qkv = matmul(h, wl["w_qkv"], BF16)
        p_pool = matmul(h, wl["w_pool"], F32)
        gates = matmul(h, wl["w_gate"], F32, act="sigmoid", n_rows=n_act)

        q = mla_q(p_small, q_norm_g[l], wl["w_uq"], rope_c, rope_s1, rope_s2)
        k, v = mla_kv(p_small, kv_norm_g[l], wl["w_ukv"], rope_c, rope_s1, rope_s2)
        a_br = mla_latent_attention(q, k, v, n_batch, seq, ctx_len)
        b_br = na_latent_attention(qkv, na_bias[l], n_batch, seq, ctx_len)
        pw = pool_w[l].astype(BF16)
        c_br = pool_mix(p_pool, pw, pool_scale[l], n_batch, seq, 0)
        if ctx_out:
            a_ctx = context_attention(q, k, v, n_batch, seq, ctx_len, A_HEADS, A_HEAD_PAD, A_VDIM, 0, 0, 0,
                                      float((A_NOPE + A_ROPE) ** -0.5))
            b_ctx = context_attention(qkv, qkv, qkv, n_batch, seq, ctx_len, B_HEADS, B_DH, B_DH,
                                      0, B_HEADS, 2 * B_HEADS, float(B_DH ** -0.5))
            c_ctx_br = pool_mix(p_pool, pw, pool_scale[l], n_batch, ctx_len, ctx_blk0)
            a_br = jnp.concatenate([a_br, a_ctx], axis=0)
            b_br = jnp.concatenate([b_br, b_ctx], axis=0)
            c_br = jnp.concatenate([c_br, c_ctx_br], axis=0)

        merged = merge_branches(a_br, b_br, c_br, wl["w_br_a"], wl["w_br_b"], wl["w_br_c"], gates)
        stream = matmul_gated_residual(merged, wl["w_out"], stream, mod3, 2, seq, n_batch)

        h2_t = ln_modulate(stream, norm2_g[l], mod3, 3, n_act, seq, n_batch, transpose=True)
        tables = peer_router(h2_t, wl["wq_t"], wl["k1"], wl["k2"])
        y_t = peer_mixture(h2_t, u_all, v_all, l, tables)
        if ctx_out:
            stream, h = residual_from_transposed(stream, y_t, mod3, 5, n_act, seq, n_batch,
                                                 norm1_g[l + 1], mod3s[l + 1])
        else:
            out = residual_from_transposed(stream, y_t, mod3, 5, n_act, seq, n_batch, final_g)

    return out.reshape(n_batch, seq, d)
```

```python
import functools

import numpy as np
import jax
import jax.numpy as jnp
from jax import lax
from jax.experimental import pallas as pl
from jax.experimental.pallas import tpu as pltpu

F32 = jnp.float32
BF16 = jnp.bfloat16

GRID_W = 64
EPS = 1e-6
ROPE_THETA = 10000.0

A_HEADS = 16
A_NOPE = 128
A_ROPE = 64
A_VDIM = 128
A_QLORA = 768
A_KVLORA = 256
A_HEAD_PAD = 256

B_HEADS = 8
B_DH = 128
B_WIDTH = B_HEADS * B_DH
WIN_H = 8
WIN_W = 16

C_WIDTH = 1024
POOL_SIZES = (2, 4, 8, 16)
C_GW = C_WIDTH // len(POOL_SIZES)
POOL_HALO = 8

P_HEADS = 8
P_NKEYS = 128
P_DK = 256
P_TOPK = 16

LANES = 128
SUBLANES = 8
GATE_DTYPE = BF16
GATE_SLAB = 16
SMALL_W = A_QLORA + A_KVLORA + LANES

NEG_BIG = -1e30
MIB = 1024 * 1024


def _cparams(n_axes, vmem_mib):
    return pltpu.CompilerParams(dimension_semantics=("arbitrary",) * n_axes,
                                vmem_limit_bytes=vmem_mib * MIB)


def _segment(row_tile, tile_rows, seq, n_batch):
    return jnp.minimum((row_tile * tile_rows) // seq, n_batch)


def _ada_kernel(c_ref, w_ref, b_ref, o_ref):
    c = c_ref[...]
    a = (c * jax.nn.sigmoid(c)).astype(BF16)
    o_ref[0] = jnp.dot(a, w_ref[0].astype(BF16), preferred_element_type=F32) + b_ref[0]


def ada_modulation(cvec, w_ada, b_ada):
    n_layers, d, n_out = w_ada.shape
    rows = cvec.shape[0]
    tn = 512
    return pl.pallas_call(
        _ada_kernel,
        grid=(n_layers, n_out // tn),
        in_specs=[pl.BlockSpec((rows, d), lambda l, j: (0, 0)),
                  pl.BlockSpec((1, d, tn), lambda l, j: (l, 0, j)),
                  pl.BlockSpec((1, 1, tn), lambda l, j: (l, 0, j))],
        out_specs=pl.BlockSpec((1, rows, tn), lambda l, j: (l, 0, j)),
        out_shape=jax.ShapeDtypeStruct((n_layers, rows, n_out), F32),
        compiler_params=_cparams(2, 40),
        name="ada_modulation",
    )(cvec, w_ada, b_ada.reshape(n_layers, 1, n_out))


def _rms(x, g):
    return x * lax.rsqrt(jnp.mean(x * x, axis=-1, keepdims=True) + EPS) * g


def _ln_mod_kernel(x_ref, g_ref, sc_ref, sh_ref, o_ref, *, transpose):
    h = _rms(x_ref[...], g_ref[...]) * (1.0 + sc_ref[0]) + sh_ref[0]
    if transpose:
        o_ref[...] = h.T.astype(o_ref.dtype)
    else:
        o_ref[...] = h.astype(o_ref.dtype)


def ln_modulate(x, g, mod3, shift_chunk, n_rows, seq, n_batch, transpose=False):
    d = x.shape[1]
    tm = 256
    seg = functools.partial(_segment, tile_rows=tm, seq=seq, n_batch=n_batch)
    if transpose:
        out_spec = pl.BlockSpec((d, tm), lambda i: (0, i))
        out_shape = jax.ShapeDtypeStruct((d, n_rows), BF16)
    else:
        out_spec = pl.BlockSpec((tm, d), lambda i: (i, 0))
        out_shape = jax.ShapeDtypeStruct((n_rows, d), BF16)
    return pl.pallas_call(
        functools.partial(_ln_mod_kernel, transpose=transpose),
        grid=(n_rows // tm,),
        in_specs=[pl.BlockSpec((tm, d), lambda i: (i, 0)),
                  pl.BlockSpec((1, d), lambda i: (0, 0)),
                  pl.BlockSpec((1, 1, d), lambda i: (seg(i), 0, shift_chunk + 1)),
                  pl.BlockSpec((1, 1, d), lambda i: (seg(i), 0, shift_chunk))],
        out_specs=out_spec,
        out_shape=out_shape,
        compiler_params=_cparams(1, 48),
        name="ln_modulate_t" if transpose else "ln_modulate",
    )(x, g.reshape(1, d), mod3, mod3)


def _mm_kernel(a_ref, b_ref, o_ref, *, act):
    acc = jnp.dot(a_ref[...], b_ref[...], preferred_element_type=F32)
    if act == "sigmoid":
        acc = jax.nn.sigmoid(acc)
    o_ref[...] = acc.astype(o_ref.dtype)


def matmul(a, b, out_dtype, act=None, n_rows=None, tn=1024, tm=512):
    m, k = a.shape
    m = n_rows or m
    n = b.shape[1]
    if n % tn:
        tn = n
    return pl.pallas_call(
        functools.partial(_mm_kernel, act=act),
        grid=(n // tn, m // tm),
        in_specs=[pl.BlockSpec((tm, k), lambda j, i: (i, 0)),
                  pl.BlockSpec((k, tn), lambda j, i: (0, j))],
        out_specs=pl.BlockSpec((tm, tn), lambda j, i: (i, j)),
        out_shape=jax.ShapeDtypeStruct((m, n), out_dtype),
        compiler_params=_cparams(2, 52),
        name="matmul_" + (act or "plain"),
    )(a, b)


def _mm_residual_kernel(a_ref, b_ref, x_ref, g_ref, o_ref):
    acc = jnp.dot(a_ref[...], b_ref[...], preferred_element_type=F32)
    o_ref[...] = x_ref[...] + g_ref[0] * acc


def matmul_gated_residual(a, b, x, mod3, gate_chunk, seq, n_batch, tn=1024, tm=512):
    m, k = a.shape
    n = b.shape[1]
    seg = functools.partial(_segment, tile_rows=tm, seq=seq, n_batch=n_batch)
    nj = n // tn
    return pl.pallas_call(
        _mm_residual_kernel,
        grid=(nj, m // tm),
        in_specs=[pl.BlockSpec((tm, k), lambda j, i: (i, 0)),
                  pl.BlockSpec((k, tn), lambda j, i: (0, j)),
                  pl.BlockSpec((tm, tn), lambda j, i: (i, j)),
                  pl.BlockSpec((1, 1, tn), lambda j, i: (seg(i), 0, gate_chunk * nj + j))],
        out_specs=pl.BlockSpec((tm, tn), lambda j, i: (i, j)),
        out_shape=jax.ShapeDtypeStruct((m, n), F32),
        compiler_params=_cparams(2, 52),
        name="matmul_gated_residual",
    )(a, b, x, mod3)


def _merge_kernel(a_ref, b_ref, c_ref, wa_ref, wb_ref, wc_ref, ga_ref, gb_ref, gc_ref, o_ref):
    m = ga_ref[...] * jnp.dot(a_ref[...], wa_ref[...], preferred_element_type=F32)
    m = m + gb_ref[...] * jnp.dot(b_ref[...], wb_ref[...], preferred_element_type=F32)
    m = m + gc_ref[...] * jnp.dot(c_ref[...], wc_ref[...], preferred_element_type=F32)
    o_ref[...] = m.astype(o_ref.dtype)


def merge_branches(a, b, c, wa, wb, wc, gates, tn=1024, tm=512):
    m = a.shape[0]
    n = wa.shape[1]
    nj = n // tn
    row = lambda j, i: (i, 0)
    col = lambda j, i: (0, j)
    return pl.pallas_call(
        _merge_kernel,
        grid=(nj, m // tm),
        in_specs=[pl.BlockSpec((tm, a.shape[1]), row),
                  pl.BlockSpec((tm, b.shape[1]), row),
                  pl.BlockSpec((tm, c.shape[1]), row),
                  pl.BlockSpec((wa.shape[0], tn), col),
                  pl.BlockSpec((wb.shape[0], tn), col),
                  pl.BlockSpec((wc.shape[0], tn), col),
                  pl.BlockSpec((tm, tn), lambda j, i: (i, j)),
                  pl.BlockSpec((tm, tn), lambda j, i: (i, nj + j)),
                  pl.BlockSpec((tm, tn), lambda j, i: (i, 2 * nj + j))],
        out_specs=pl.BlockSpec((tm, tn), lambda j, i: (i, j)),
        out_shape=jax.ShapeDtypeStruct((m, n), BF16),
        compiler_params=_cparams(2, 52),
        name="merge_branches",
    )(a, b, c, wa, wb, wc, gates, gates, gates)


def _rotate(v, c_ref, s1_ref, s2_ref):
    return (v * c_ref[...] + pltpu.roll(v, LANES - A_ROPE // 2, 1) * s1_ref[...]
            + pltpu.roll(v, A_ROPE // 2, 1) * s2_ref[...])


def _mla_q_kernel(cq_ref, g_ref, w_ref, c_ref, s1_ref, s2_ref, o_ref):
    y = _rms(cq_ref[...], g_ref[...]).astype(BF16)
    acc = jnp.dot(y, w_ref[...], preferred_element_type=F32)
    for h in range(A_HEADS):
        lo = h * A_HEAD_PAD
        o_ref[:, lo:lo + A_NOPE] = acc[:, lo:lo + A_NOPE].astype(o_ref.dtype)
        rot = _rotate(acc[:, lo + A_NOPE:lo + A_HEAD_PAD], c_ref, s1_ref, s2_ref)
        o_ref[:, lo + A_NOPE:lo + A_HEAD_PAD] = rot.astype(o_ref.dtype)


def mla_q(p_small, g, w_uq_pad, rope_c, rope_s1, rope_s2):
    nt = p_small.shape[0]
    tm = 256
    width = A_HEADS * A_HEAD_PAD
    tab = pl.BlockSpec((tm, LANES), lambda i: (i, 0))
    return pl.pallas_call(
        _mla_q_kernel,
        grid=(nt // tm,),
        in_specs=[pl.BlockSpec((tm, A_QLORA), lambda i: (i, 0)),
                  pl.BlockSpec((1, A_QLORA), lambda i: (0, 0)),
                  pl.BlockSpec((A_QLORA, width), lambda i: (0, 0)),
                  tab, tab, tab],
        out_specs=pl.BlockSpec((tm, width), lambda i: (i, 0)),
        out_shape=jax.ShapeDtypeStruct((nt, width), BF16),
        compiler_params=_cparams(1, 48),
        name="mla_q",
    )(p_small, g.reshape(1, A_QLORA), w_uq_pad, rope_c, rope_s1, rope_s2)


def _mla_kv_kernel(ckv_ref, kr_ref, g_ref, w_ref, c_ref, s1_ref, s2_ref, k_ref, v_ref):
    y = _rms(ckv_ref[...], g_ref[...]).astype(BF16)
    acc = jnp.dot(y, w_ref[...], preferred_element_type=F32)
    kr = _rotate(kr_ref[...], c_ref, s1_ref, s2_ref).astype(k_ref.dtype)
    for h in range(A_HEADS):
        lo = h * A_HEAD_PAD
        k_ref[:, lo:lo + A_NOPE] = acc[:, h * A_NOPE:(h + 1) * A_NOPE].astype(k_ref.dtype)
        k_ref[:, lo + A_NOPE:lo + A_HEAD_PAD] = kr
    v_ref[...] = acc[:, A_HEADS * A_NOPE:].astype(v_ref.dtype)


def mla_kv(p_small, g, w_ukv_perm, rope_c, rope_s1, rope_s2):
    nt = p_small.shape[0]
    tm = 256
    kw = A_HEADS * A_HEAD_PAD
    vw = A_HEADS * A_VDIM
    tab = pl.BlockSpec((tm, LANES), lambda i: (i, 0))
    return pl.pallas_call(
        _mla_kv_kernel,
        grid=(nt // tm,),
        in_specs=[pl.BlockSpec((tm, A_KVLORA), lambda i: (i, A_QLORA // A_KVLORA)),
                  pl.BlockSpec((tm, LANES), lambda i: (i, (A_QLORA + A_KVLORA) // LANES)),
                  pl.BlockSpec((1, A_KVLORA), lambda i: (0, 0)),
                  pl.BlockSpec((A_KVLORA, A_HEADS * (A_NOPE + A_VDIM)), lambda i: (0, 0)),
                  tab, tab, tab],
        out_specs=[pl.BlockSpec((tm, kw), lambda i: (i, 0)),
                   pl.BlockSpec((tm, vw), lambda i: (i, 0))],
        out_shape=[jax.ShapeDtypeStruct((nt, kw), BF16),
                   jax.ShapeDtypeStruct((nt, vw), BF16)],
        compiler_params=_cparams(1, 48),
        name="mla_kv",
    )(p_small, p_small, g.reshape(1, A_KVLORA), w_ukv_perm, rope_c, rope_s1, rope_s2)


def _qk(q, k):
    return lax.dot_general(q, k, (((1,), (1,)), ((), ())), preferred_element_type=F32)


LOG2_E = 1.4426950408889634


def _flash_two_sets_kernel(q_ref, kl_ref, vl_ref, kc_ref, vc_ref, o_ref, *, scale, chunk):
    q = q_ref[...]
    c = scale * LOG2_E
    n_chunks = kl_ref.shape[0] // chunk
    keys = lambda j: kl_ref[j * chunk:(j + 1) * chunk, :]
    t = _qk(q, kc_ref[...]) * c
    t_next = _qk(q, keys(0)) * c
    m = jnp.max(t, axis=-1, keepdims=True)
    p = jnp.exp2(t - m)
    den = jnp.sum(p, axis=-1, keepdims=True)
    acc = jnp.dot(p.astype(BF16), vc_ref[...], preferred_element_type=F32)
    for j in range(n_chunks):
        t = t_next
        if j + 1 < n_chunks:
            t_next = _qk(q, keys(j + 1)) * c
        m_new = jnp.maximum(m, jnp.max(t, axis=-1, keepdims=True))
        alpha = jnp.exp2(m - m_new)
        p = jnp.exp2(t - m_new)
        den = alpha * den + jnp.sum(p, axis=-1, keepdims=True)
        acc = alpha * acc + jnp.dot(p.astype(BF16), vl_ref[j * chunk:(j + 1) * chunk, :],
                                    preferred_element_type=F32)
        m = m_new
    o_ref[...] = (acc / den).astype(o_ref.dtype)


def mla_latent_attention(q, k, v, n_batch, seq, ctx_len, tq=1024, chunk=512):
    ctx_blk0 = (n_batch * seq) // ctx_len
    nq = seq // tq
    scale = float((A_NOPE + A_ROPE) ** -0.5)
    return pl.pallas_call(
        functools.partial(_flash_two_sets_kernel, scale=scale, chunk=chunk),
        grid=(n_batch, A_HEADS, nq),
        in_specs=[pl.BlockSpec((tq, A_HEAD_PAD), lambda b, h, i: (b * nq + i, h)),
                  pl.BlockSpec((seq, A_HEAD_PAD), lambda b, h, i: (b, h)),
                  pl.BlockSpec((seq, A_VDIM), lambda b, h, i: (b, h)),
                  pl.BlockSpec((ctx_len, A_HEAD_PAD), lambda b, h, i: (ctx_blk0 + b, h)),
                  pl.BlockSpec((ctx_len, A_VDIM), lambda b, h, i: (ctx_blk0 + b, h))],
        out_specs=pl.BlockSpec((tq, A_VDIM), lambda b, h, i: (b * nq + i, h)),
        out_shape=jax.ShapeDtypeStruct((n_batch * seq, A_HEADS * A_VDIM), BF16),
        compiler_params=_cparams(3, 48),
        name="mla_latent_attention",
    )(q, k, v, k, v)


def _attn_one_set_kernel(q_ref, k_ref, v_ref, o_ref, *, scale):
    s = _qk(q_ref[...], k_ref[...]) * scale
    p = jnp.exp(s - jnp.max(s, axis=-1, keepdims=True))
    den = jnp.sum(p, axis=-1, keepdims=True)
    o = jnp.dot(p.astype(BF16), v_ref[...], preferred_element_type=F32)
    o_ref[...] = (o / den).astype(o_ref.dtype)


def context_attention(q, k, v, n_batch, seq, ctx_len, n_heads, dqk, dv, q_col0, k_col0, v_col0, scale):
    ctx_blk0 = (n_batch * seq) // ctx_len
    return pl.pallas_call(
        functools.partial(_attn_one_set_kernel, scale=scale),
        grid=(n_batch, n_heads),
        in_specs=[pl.BlockSpec((ctx_len, dqk), lambda b, h: (ctx_blk0 + b, q_col0 + h)),
                  pl.BlockSpec((ctx_len, dqk), lambda b, h: (ctx_blk0 + b, k_col0 + h)),
                  pl.BlockSpec((ctx_len, dv), lambda b, h: (ctx_blk0 + b, v_col0 + h))],
        out_specs=pl.BlockSpec((ctx_len, dv), lambda b, h: (b, h)),
        out_shape=jax.ShapeDtypeStruct((n_batch * ctx_len, n_heads * dv), BF16),
        compiler_params=_cparams(2, 32),
        name="context_attention",
    )(q, k, v)


def _na_kernel(q_ref, kl_ref, vl_ref, kc_ref, vc_ref, bias_ref, o_ref, *, rows, scale):
    r = pl.program_id(1)
    r0 = jnp.clip(r - WIN_H // 2, 0, rows - WIN_H)
    start = pl.multiple_of(r0 * GRID_W, GRID_W)
    win = WIN_H * GRID_W
    heads = [slice(h * B_DH, (h + 1) * B_DH) for h in range(B_HEADS)]
    scores = [(_qk(q_ref[:, cols], kl_ref[pl.ds(start, win), cols]), _qk(q_ref[:, cols], kc_ref[:, cols]))
              for cols in heads]
    probs = []
    for h, (s_l, s_c) in enumerate(scores):
        s_l = s_l * scale + bias_ref[0, h]
        s_c = s_c * scale
        m = jnp.maximum(jnp.max(s_l, axis=-1, keepdims=True), jnp.max(s_c, axis=-1, keepdims=True))
        p_l = jnp.exp(s_l - m)
        p_c = jnp.exp(s_c - m)
        den = jnp.sum(p_l, axis=-1, keepdims=True) + jnp.sum(p_c, axis=-1, keepdims=True)
        probs.append((p_l.astype(BF16), p_c.astype(BF16), den))
    outs = []
    for cols, (p_l, p_c, den) in zip(heads, probs):
        o = jnp.dot(p_l, vl_ref[pl.ds(start, win), cols], preferred_element_type=F32)
        o = o + jnp.dot(p_c, vc_ref[:, cols], preferred_element_type=F32)
        outs.append(o / den)
    o_ref[...] = jnp.concatenate(outs, axis=1).astype(o_ref.dtype)


def na_latent_attention(qkv, bias, n_batch, seq, ctx_len):
    rows = seq // GRID_W
    ctx_blk0 = (n_batch * seq) // ctx_len
    win = WIN_H * GRID_W

    def bias_idx(b, r):
        return (r - jnp.clip(r - WIN_H // 2, 0, rows - WIN_H), 0, 0, 0)

    once = pl.Buffered(1)
    return pl.pallas_call(
        functools.partial(_na_kernel, rows=rows, scale=float(B_DH ** -0.5)),
        grid=(n_batch, rows),
        in_specs=[pl.BlockSpec((GRID_W, B_WIDTH), lambda b, r: (b * rows + r, 0)),
                  pl.BlockSpec((seq, B_WIDTH), lambda b, r: (b, 1), pipeline_mode=once),
                  pl.BlockSpec((seq, B_WIDTH), lambda b, r: (b, 2), pipeline_mode=once),
                  pl.BlockSpec((ctx_len, B_WIDTH), lambda b, r: (ctx_blk0 + b, 1)),
                  pl.BlockSpec((ctx_len, B_WIDTH), lambda b, r: (ctx_blk0 + b, 2)),
                  pl.BlockSpec((1, B_HEADS, GRID_W, win), bias_idx)],
        out_specs=pl.BlockSpec((GRID_W, B_WIDTH), lambda b, r: (b * rows + r, 0)),
        out_shape=jax.ShapeDtypeStruct((n_batch * seq, B_WIDTH), BF16),
        compiler_params=_cparams(2, 40),
        name="na_latent_attention",
    )(qkv, qkv, qkv, qkv, qkv, bias)


def na_bias_table(rel_bias):
    rel = rel_bias.astype(F32)
    n_layers = rel.shape[0]
    by_row = jnp.stack([rel[:, :, WIN_H - 1 - d:2 * WIN_H - 1 - d, :] for d in range(WIN_H)], axis=2)
    pad = GRID_W - 1
    by_row = jnp.pad(by_row, ((0, 0), (0, 0), (0, 0), (0, 0), (pad, pad)))
    off = WIN_W - 1 + pad
    tab = jnp.stack([by_row[..., off - c:off - c + GRID_W] for c in range(GRID_W)], axis=4)
    c = jnp.arange(GRID_W)[:, None]
    kc = jnp.arange(GRID_W)[None, :]
    c0 = jnp.clip(c - WIN_W // 2, 0, GRID_W - WIN_W)
    inside = (kc >= c0) & (kc < c0 + WIN_W)
    tab = jnp.where(inside[None, None, None, None], tab, NEG_BIG)
    return tab.transpose(0, 2, 1, 4, 3, 5).reshape(n_layers, WIN_H, B_HEADS, GRID_W, WIN_H * GRID_W)


def _pool_kernel(x_ref, w_ref, sc_ref, o_ref, pad_ref, *, seq_len, chunk):
    g = pl.program_id(1)
    zeros = jnp.zeros((POOL_HALO, C_GW), F32)
    pad_ref[0:POOL_HALO, :] = zeros
    pad_ref[POOL_HALO + seq_len:2 * POOL_HALO + seq_len, :] = zeros
    pad_ref[POOL_HALO:POOL_HALO + seq_len, :] = x_ref[...]
    for gi, w in enumerate(POOL_SIZES):
        @pl.when(g == gi)
        def _(w=w):
            for c0 in range(0, seq_len, chunk):
                acc = None
                for k in range(-(w // 2), w - w // 2):
                    piece = pad_ref[POOL_HALO + c0 + k:POOL_HALO + c0 + k + chunk, :]
                    acc = piece if acc is None else acc + piece
                t = c0 + lax.broadcasted_iota(jnp.int32, (chunk, C_GW), 0)
                cnt = jnp.minimum(t + (w - w // 2), seq_len) - jnp.maximum(t - w // 2, 0)
                d = (acc / cnt.astype(F32) - x_ref[c0:c0 + chunk, :]).astype(BF16)
                y = jnp.dot(d, w_ref[0], preferred_element_type=F32) * sc_ref[...]
                o_ref[c0:c0 + chunk, :] = y.astype(o_ref.dtype)


def pool_mix(p_pool, pool_w, pool_scale, n_seq, seq_len, first_block):
    chunk = min(seq_len, 512)
    return pl.pallas_call(
        functools.partial(_pool_kernel, seq_len=seq_len, chunk=chunk),
        grid=(n_seq, len(POOL_SIZES)),
        in_specs=[pl.BlockSpec((seq_len, C_GW), lambda b, g: (first_block + b, g)),
                  pl.BlockSpec((1, C_GW, C_GW), lambda b, g: (g, 0, 0)),
                  pl.BlockSpec((1, C_GW), lambda b, g: (0, g))],
        out_specs=pl.BlockSpec((seq_len, C_GW), lambda b, g: (b, g)),
        out_shape=jax.ShapeDtypeStruct((n_seq * seq_len, C_WIDTH), BF16),
        scratch_shapes=[pltpu.VMEM((seq_len + 2 * POOL_HALO, C_GW), F32)],
        compiler_params=_cparams(2, 48),
        name="pool_mix",
    )(p_pool, pool_w, pool_scale.reshape(1, C_WIDTH))


_REGION = [(r1, r2) for r1 in range(P_TOPK) for r2 in range(P_TOPK) if (r1 + 1) * (r2 + 1) <= P_TOPK]


def _dominates(a, b):
    return a != b and a[0] <= b[0] and a[1] <= b[1]


def _router_kernel(h_ref, wq_ref, k1_ref, k2_ref, cnt_ref, a_ref, rk_ref, b_ref,
                   s1_scr, s2_scr, v1_scr, i1_scr, v2_scr, i2_scr, cr_scr, e1_scr, e2_scr):
    tm = h_ref.shape[1]
    half = P_HEADS * P_DK // 2
    q_t = jnp.dot(wq_ref[...], h_ref[...], preferred_element_type=F32)
    key = lax.broadcasted_iota(jnp.int32, (P_NKEYS, P_HEADS, tm), 0)
    s1_scr[...] = jnp.dot(k1_ref[...], q_t[:half].astype(BF16),
                          preferred_element_type=F32).reshape(P_NKEYS, P_HEADS, tm)
    s2_scr[...] = jnp.dot(k2_ref[...], q_t[half:].astype(BF16),
                          preferred_element_type=F32).reshape(P_NKEYS, P_HEADS, tm)

    def first_max(s):
        level = [(s[k], k) for k in range(P_NKEYS)]
        while len(level) > 1:
            nxt = []
            for (va, ia), (vb, ib) in zip(level[0::2], level[1::2]):
                take = vb > va
                nxt.append((jnp.where(take, vb, va), jnp.where(take, ib, ia)))
            level = nxt
        return level[0]

    def extract(r, carry):
        for s_scr, v_scr, i_scr in ((s1_scr, v1_scr, i1_scr), (s2_scr, v2_scr, i2_scr)):
            s = s_scr[...]
            m, idx = first_max(s)
            s_scr[...] = jnp.where(key == idx[None], -jnp.inf, s)
            v_scr[r] = m
            i_scr[r] = idx
        return carry

    lax.fori_loop(0, P_TOPK, extract, 0)

    v1 = [v1_scr[r] for r in range(P_TOPK)]
    v2 = [v2_scr[r] for r in range(P_TOPK)]
    sums = {c: v1[c[0]] + v2[c[1]] for c in _REGION}
    beaten = {c: float(sum(_dominates(o, c) for o in _REGION)) for c in _REGION}
    for ci, c in enumerate(_REGION):
        for o in _REGION[:ci]:
            if _dominates(o, c):
                continue
            o_first = (sums[o] >= sums[c]).astype(F32)
            beaten[c] = beaten[c] + o_first
            beaten[o] = beaten[o] + (1.0 - o_first)
    e1 = [jnp.exp(v1[r] - v1[0]) for r in range(P_TOPK)]
    e2 = [jnp.exp(v2[r] - v2[0]) for r in range(P_TOPK)]
    z = jnp.zeros_like(v1[0])
    counts = [jnp.zeros_like(v1[0]) for _ in range(P_TOPK)]
    for c in _REGION:
        chosen = beaten[c] < float(P_TOPK)
        z = z + jnp.where(chosen, e1[c[0]] * e2[c[1]], 0.0)
        counts[c[0]] = counts[c[0]] + jnp.where(chosen, 1.0, 0.0)
    inv_z = 1.0 / z
    for r in range(P_TOPK):
        cr_scr[r] = counts[r]
        e1_scr[r] = e1[r]
        e2_scr[r] = e2[r] * inv_z

    key2 = lax.broadcasted_iota(jnp.int32, (P_NKEYS, LANES), 0)

    def expand(n, carry, h):
        t0 = pl.multiple_of(n * LANES, LANES)
        row = lambda ref, r: ref[r, h:h + 1, pl.ds(t0, LANES)]
        cnt = jnp.zeros((P_NKEYS, LANES), F32)
        a = cnt
        for r in range(P_TOPK):
            hit = key2 == row(i1_scr, r)
            cnt = jnp.where(hit, row(cr_scr, r), cnt)
            a = jnp.where(hit, row(e1_scr, r), a)
        cnt_ref[h, :, pl.ds(t0, LANES)] = cnt
        a_ref[h, :, pl.ds(t0, LANES)] = a
        rk = jnp.full((P_NKEYS, LANES), float(P_TOPK), F32)
        b = jnp.zeros((P_NKEYS, LANES), F32)
        for r in range(P_TOPK):
            hit = key2 == row(i2_scr, r)
            rk = jnp.where(hit, float(r), rk)
            b = jnp.where(hit, row(e2_scr, r), b)
        rk_ref[h, :, pl.ds(t0, LANES)] = rk.astype(rk_ref.dtype)
        b_ref[h, :, pl.ds(t0, LANES)] = b.astype(b_ref.dtype)
        return carry

    for h in range(P_HEADS):
        lax.fori_loop(0, tm // LANES, functools.partial(expand, h=h), 0)


def peer_router(h_t, wq_t, k1_packed, k2_packed, tm=256):
    d, n = h_t.shape
    qw = wq_t.shape[0]
    kp = k1_packed.shape[0]
    tab = lambda dt: jax.ShapeDtypeStruct((P_HEADS, P_NKEYS, n), dt)
    tab_spec = pl.BlockSpec((P_HEADS, P_NKEYS, tm), lambda i: (0, 0, i))
    small = lambda dt: pltpu.VMEM((P_TOPK, P_HEADS, tm), dt)
    return pl.pallas_call(
        _router_kernel,
        grid=(n // tm,),
        in_specs=[pl.BlockSpec((d, tm), lambda i: (0, i)),
                  pl.BlockSpec((qw, d), lambda i: (0, 0), pipeline_mode=pl.Buffered(1)),
                  pl.BlockSpec((kp, kp), lambda i: (0, 0), pipeline_mode=pl.Buffered(1)),
                  pl.BlockSpec((kp, kp), lambda i: (0, 0), pipeline_mode=pl.Buffered(1))],
        out_specs=[tab_spec] * 4,
        out_shape=[tab(F32)] * 4,
        scratch_shapes=[pltpu.VMEM((P_NKEYS, P_HEADS, tm), F32), pltpu.VMEM((P_NKEYS, P_HEADS, tm), F32),
                        small(F32), small(jnp.int32), small(F32), small(jnp.int32),
                        small(F32), small(F32), small(F32)],
        compiler_params=_cparams(1, 52),
        name="peer_router",
    )(h_t, wq_t, k1_packed, k2_packed)


def _gelu_exact(x):
    return 0.5 * x * (1.0 + lax.erf(x * float(np.sqrt(0.5))))


def _peer_kernel(h_ref, u_ref, v_ref, cnt_ref, a_ref, rk_in_ref, b_in_ref, o_ref, rk_ref, b_ref):
    e = pl.program_id(1)
    te = u_ref.shape[0]
    tm = h_ref.shape[1]

    def step(first):
        if first:
            rk_ref[...] = rk_in_ref[...].astype(GATE_DTYPE)
            b_ref[...] = b_in_ref[...].astype(GATE_DTYPE)
        halves = [(c0, jnp.dot(u_ref[c0:c0 + te // 2, :], h_ref[...], preferred_element_type=F32))
                  for c0 in (0, te // 2)]
        rows_per_half = te // 2 // P_NKEYS
        n_slab = P_NKEYS // GATE_SLAB
        ga_rows = []
        for c0, pre in halves:
            i0 = e * (te // P_NKEYS) + c0 // P_NKEYS
            rows = [[(cnt_ref[h, pl.ds(i0 + k, 1), :], a_ref[h, pl.ds(i0 + k, 1), :]) for h in range(P_HEADS)]
                    for k in range(rows_per_half)]
            tiles = {}
            for t0 in range(0, tm, LANES):
                lanes = slice(t0, t0 + LANES)
                g = [[None] * n_slab for _ in range(rows_per_half)]
                for h in range(P_HEADS):
                    per_row = [[jnp.broadcast_to(x[:, lanes].astype(GATE_DTYPE), (GATE_SLAB, LANES)) for x in rows[k][h]]
                               for k in range(rows_per_half)]
                    for s in range(n_slab):
                        slab = slice(s * GATE_SLAB, (s + 1) * GATE_SLAB)
                        rk = rk_ref[h, slab, lanes]
                        b = b_ref[h, slab, lanes]
                        for k, (cnt, a) in enumerate(per_row):
                            term = jnp.where(rk < cnt, b * a, jnp.zeros_like(b))
                            g[k][s] = term if g[k][s] is None else g[k][s] + term
                for k in range(rows_per_half):
                    for s in range(n_slab):
                        r = k * P_NKEYS + s * GATE_SLAB
                        tiles[(k, s, t0)] = g[k][s].astype(F32) * _gelu_exact(pre[r:r + GATE_SLAB, lanes])
            for k in range(rows_per_half):
                for s in range(n_slab):
                    ga_rows.append(jnp.concatenate([tiles[(k, s, t0)] for t0 in range(0, tm, LANES)], axis=1))
        ga = jnp.concatenate(ga_rows, axis=0).astype(BF16)
        contrib = lax.dot_general(v_ref[...], ga, (((0,), (0,)), ((), ())), preferred_element_type=F32)
        if first:
            o_ref[...] = contrib
        else:
            o_ref[...] += contrib

    pl.when(e == 0)(functools.partial(step, True))
    pl.when(e > 0)(functools.partial(step, False))


def peer_mixture(h_t, u_all, v_all, layer, tables, tm=512, te=512):
    d, n = h_t.shape
    tab_spec = pl.BlockSpec((P_HEADS, P_NKEYS, tm), lambda i, e: (0, 0, i), pipeline_mode=pl.Buffered(1))
    expert_tile = pl.BlockSpec((None, te, d), lambda i, e: (layer, e, 0))
    return pl.pallas_call(
        _peer_kernel,
        grid=(n // tm, u_all.shape[1] // te),
        in_specs=[pl.BlockSpec((d, tm), lambda i, e: (0, i), pipeline_mode=pl.Buffered(1)),
                  expert_tile, expert_tile] + [tab_spec] * 4,
        out_specs=pl.BlockSpec((d, tm), lambda i, e: (0, i)),
        out_shape=jax.ShapeDtypeStruct((d, n), F32),
        scratch_shapes=[pltpu.VMEM((P_HEADS, P_NKEYS, tm), GATE_DTYPE)] * 2,
        compiler_params=_cparams(2, 56),
        name="peer_mixture",
    )(h_t, u_all, v_all, *tables)


def _residual_t_ln_kernel(x_ref, y_ref, g_ref, ng_ref, sc_ref, sh_ref, o_ref, h_ref):
    x = x_ref[...] + g_ref[0] * y_ref[...].T
    o_ref[...] = x
    h_ref[...] = (_rms(x, ng_ref[...]) * (1.0 + sc_ref[0]) + sh_ref[0]).astype(h_ref.dtype)


def _residual_t_final_kernel(x_ref, y_ref, g_ref, ng_ref, o_ref):
    o_ref[...] = _rms(x_ref[...] + g_ref[0] * y_ref[...].T, ng_ref[...])


def residual_from_transposed(x, y_t, mod3, gate_chunk, n_rows, seq, n_batch, norm_g, next_mod3=None):
    d = x.shape[1]
    tm = 256
    seg = functools.partial(_segment, tile_rows=tm, seq=seq, n_batch=n_batch)
    row = pl.BlockSpec((tm, d), lambda i: (i, 0))
    in_specs = [row,
                pl.BlockSpec((d, tm), lambda i: (0, i)),
                pl.BlockSpec((1, 1, d), lambda i: (seg(i), 0, gate_chunk)),
                pl.BlockSpec((1, d), lambda i: (0, 0))]
    if next_mod3 is None:
        return pl.pallas_call(
            _residual_t_final_kernel,
            grid=(n_rows // tm,),
            in_specs=in_specs,
            out_specs=row,
            out_shape=jax.ShapeDtypeStruct((n_rows, d), F32),
            compiler_params=_cparams(1, 48),
            name="residual_final_norm",
        )(x, y_t, mod3, norm_g.reshape(1, d))
    return pl.pallas_call(
        _residual_t_ln_kernel,
        grid=(n_rows // tm,),
        in_specs=in_specs + [pl.BlockSpec((1, 1, d), lambda i: (seg(i), 0, 1)),
                             pl.BlockSpec((1, 1, d), lambda i: (seg(i), 0, 0))],
        out_specs=[row, row],
        out_shape=[jax.ShapeDtypeStruct((n_rows, d), F32), jax.ShapeDtypeStruct((n_rows, d), BF16)],
        compiler_params=_cparams(1, 48),
        name="residual_next_norm",
    )(x, y_t, mod3, norm_g.reshape(1, d), next_mod3, next_mod3)


def _rope_tables(seq, n_batch, n_ctx_rows):
    t = jnp.arange(seq)
    row = (t // GRID_W).astype(F32)
    col = (t % GRID_W).astype(F32)
    n_freq = A_ROPE // 4
    freqs = ROPE_THETA ** (-jnp.arange(n_freq, dtype=F32) / n_freq)
    ang = jnp.concatenate([row[:, None] * freqs, col[:, None] * freqs], axis=-1)
    cos, sin = jnp.cos(ang), jnp.sin(ang)
    zero = jnp.zeros_like(cos)
    pad = jnp.zeros((seq, LANES - A_ROPE), F32)
    c = jnp.concatenate([cos, cos, pad], axis=1)
    s1 = jnp.concatenate([-sin, zero, pad], axis=1)
    s2 = jnp.concatenate([zero, sin, pad], axis=1)
    ident = jnp.concatenate([jnp.ones((n_ctx_rows, A_ROPE), F32), jnp.zeros((n_ctx_rows, LANES - A_ROPE), F32)], 1)
    none = jnp.zeros((n_ctx_rows, LANES), F32)
    tile = lambda a, ctx: jnp.concatenate([jnp.tile(a, (n_batch, 1)), ctx], axis=0)
    return tile(c, ident), tile(s1, none), tile(s2, none)


def _pack_keys(k):
    h, nk, dk = k.shape
    eye = jnp.eye(h, dtype=k.dtype)
    blk = k.transpose(1, 0, 2)[:, :, None, :] * eye[None, :, :, None]
    return blk.reshape(nk * h, h * dk).astype(BF16)


def _prepare_layer(w_in, w_uq, w_ukv, w_br_a, w_br_b, w_br_c, w_out, peer_wq, peer_k1, peer_k2):
    d = w_in.shape[0]
    o_kr = A_QLORA + A_KVLORA
    o_q = o_kr + A_ROPE
    o_pool = o_q + 3 * B_WIDTH
    o_gate = o_pool + C_WIDTH
    w_small = jnp.concatenate([w_in[:, :o_q], jnp.zeros((d, LANES - A_ROPE), w_in.dtype)], axis=1)
    w_uq_pad = jnp.pad(w_uq.reshape(A_QLORA, A_HEADS, A_NOPE + A_ROPE),
                       ((0, 0), (0, 0), (0, A_HEAD_PAD - A_NOPE - A_ROPE))).reshape(A_QLORA, A_HEADS * A_HEAD_PAD)
    w_ukv_perm = w_ukv.reshape(A_KVLORA, A_HEADS, 2, A_NOPE).transpose(0, 2, 1, 3).reshape(A_KVLORA, -1)
    wq_t = peer_wq.reshape(d, P_HEADS, 2, P_DK // 2).transpose(2, 1, 3, 0).reshape(P_HEADS * P_DK, d)
    return dict(
        w_small=w_small.astype(BF16),
        w_qkv=w_in[:, o_q:o_pool].astype(BF16),
        w_pool=w_in[:, o_pool:o_gate].astype(BF16),
        w_gate=w_in[:, o_gate:].astype(BF16),
        w_uq=w_uq_pad.astype(BF16),
        w_ukv=w_ukv_perm.astype(BF16),
        w_br_a=w_br_a.astype(BF16), w_br_b=w_br_b.astype(BF16), w_br_c=w_br_c.astype(BF16),
        w_out=w_out.astype(BF16),
        wq_t=wq_t.astype(BF16),
        k1=_pack_keys(peer_k1), k2=_pack_keys(peer_k2),
    )


def kernel(x, c, ctx, c_ctx, w_ada, b_ada, norm1_g, norm2_g, w_in, q_norm_g, kv_norm_g, w_uq, w_ukv, na_rel_bias, pool_w, pool_scale, w_br_a, w_br_b, w_br_c, w_out, peer_wq, peer_k1, peer_k2, peer_u, peer_v, final_g):
    n_batch, seq, d = x.shape
    ctx_len = ctx.shape[1]
    depth = w_ada.shape[0]
    n_lat = n_batch * seq
    n_ctx = n_batch * ctx_len
    nt = n_lat + n_ctx
    ctx_blk0 = n_lat // ctx_len

    stream = jnp.concatenate([x.reshape(n_lat, d), ctx.reshape(n_ctx, d)], axis=0)
    mod_rows = 8
    cvec = jnp.concatenate([c, c_ctx[None], jnp.zeros((mod_rows - n_batch - 1, d), c.dtype)], axis=0)
    mod = ada_modulation(cvec, w_ada, b_ada)
    rope_c, rope_s1, rope_s2 = _rope_tables(seq, n_batch, n_ctx)
    na_bias = na_bias_table(na_rel_bias)
    mod3s = [mod[l].reshape(mod_rows, 1, 6 * d) for l in range(depth)]
    u_all = peer_u.astype(BF16)
    v_all = peer_v.astype(BF16)

    h = ln_modulate(stream, norm1_g[0], mod3s[0], 0, nt, seq, n_batch)
    for l in range(depth):
        ctx_out = l < depth - 1
        n_act = nt if ctx_out else n_lat
        wl = _prepare_layer(w_in[l], w_uq[l], w_ukv[l], w_br_a[l], w_br_b[l], w_br_c[l], w_out[l],
                            peer_wq[l], peer_k1[l], peer_k2[l])
        mod3 = mod3s[l]

        p_small = matmul(h, wl["w_small"], F32)
        qkv = matmul(h, wl["w_qkv"], BF16)
        p_pool = matmul(h, wl["w_pool"], F32)
        gates = matmul(h, wl["w_gate"], F32, act="sigmoid", n_rows=n_act)

        q = mla_q(p_small, q_norm_g[l], wl["w_uq"], rope_c, rope_s1, rope_s2)
        k, v = mla_kv(p_small, kv_norm_g[l], wl["w_ukv"], rope_c, rope_s1, rope_s2)
        a_br = mla_latent_attention(q, k, v, n_batch, seq, ctx_len)
        b_br = na_latent_attention(qkv, na_bias[l], n_batch, seq, ctx_len)
        pw = pool_w[l].astype(BF16)
        c_br = pool_mix(p_pool, pw, pool_scale[l], n_batch, seq, 0)
        if ctx_out:
            a_ctx = context_attention(q, k, v, n_batch, seq, ctx_len, A_HEADS, A_HEAD_PAD, A_VDIM, 0, 0, 0,
                                      float((A_NOPE + A_ROPE) ** -0.5))
            b_ctx = context_attention(qkv, qkv, qkv, n_batch, seq, ctx_len, B_HEADS, B_DH, B_DH,
                                      0, B_HEADS, 2 * B_HEADS, float(B_DH ** -0.5))
            c_ctx_br = pool_mix(p_pool, pw, pool_scale[l], n_batch, ctx_len, ctx_blk0)
            a_br = jnp.concatenate([a_br, a_ctx], axis=0)
            b_br = jnp.concatenate([b_br, b_ctx], axis=0)
            c_br = jnp.concatenate([c_br, c_ctx_br], axis=0)

        merged = merge_branches(a_br, b_br, c_br, wl["w_br_a"], wl["w_br_b"], wl["w_br_c"], gates)
        stream = matmul_gated_residual(merged, wl["w_out"], stream, mod3, 2, seq, n_batch)

        h2_t = ln_modulate(stream, norm2_g[l], mod3, 3, n_act, seq, n_batch, transpose=True)
        tables = peer_router(h2_t, wl["wq_t"], wl["k1"], wl["k2"])
        y_t = peer_mixture(h2_t, u_all, v_all, l, tables)
        if ctx_out:
            stream, h = residual_from_transposed(stream, y_t, mod3, 5, n_act, seq, n_batch,
                                                 norm1_g[l + 1], mod3s[l + 1])
        else:
            out = residual_from_transposed(stream, y_t, mod3, 5, n_act, seq, n_batch, final_g)

    return out.reshape(n_batch, seq, d)
```

```python
import functools

import numpy as np
import jax
import jax.numpy as jnp
from jax import lax
from jax.experimental import pallas as pl
from jax.experimental.pallas import tpu as pltpu

F32 = jnp.float32
BF16 = jnp.bfloat16

GRID_W = 64
EPS = 1e-6
ROPE_THETA = 10000.0

A_HEADS = 16
A_NOPE = 128
A_ROPE = 64
A_VDIM = 128
A_QLORA = 768
A_KVLORA = 256
A_HEAD_PAD = 256

B_HEADS = 8
B_DH = 128
B_WIDTH = B_HEADS * B_DH
WIN_H = 8
WIN_W = 16

C_WIDTH = 1024
POOL_SIZES = (2, 4, 8, 16)
C_GW = C_WIDTH // len(POOL_SIZES)
POOL_HALO = 8

P_HEADS = 8
P_NKEYS = 128
P_DK = 256
P_TOPK = 16

LANES = 128
SUBLANES = 8
GATE_DTYPE = BF16
GATE_SLAB = 16
SMALL_W = A_QLORA + A_KVLORA + LANES

NEG_BIG = -1e30
MIB = 1024 * 1024


def _cparams(n_axes, vmem_mib):
    return pltpu.CompilerParams(dimension_semantics=("arbitrary",) * n_axes,
                                vmem_limit_bytes=vmem_mib * MIB)


def _segment(row_tile, tile_rows, seq, n_batch):
    return jnp.minimum((row_tile * tile_rows) // seq, n_batch)


def _ada_kernel(c_ref, w_ref, b_ref, o_ref):
    c = c_ref[...]
    a = (c * jax.nn.sigmoid(c)).astype(BF16)
    o_ref[0] = jnp.dot(a, w_ref[0].astype(BF16), preferred_element_type=F32) + b_ref[0]


def ada_modulation(cvec, w_ada, b_ada):
    n_layers, d, n_out = w_ada.shape
    rows = cvec.shape[0]
    tn = 512
    return pl.pallas_call(
        _ada_kernel,
        grid=(n_layers, n_out // tn),
        in_specs=[pl.BlockSpec((rows, d), lambda l, j: (0, 0)),
                  pl.BlockSpec((1, d, tn), lambda l, j: (l, 0, j)),
                  pl.BlockSpec((1, 1, tn), lambda l, j: (l, 0, j))],
        out_specs=pl.BlockSpec((1, rows, tn), lambda l, j: (l, 0, j)),
        out_shape=jax.ShapeDtypeStruct((n_layers, rows, n_out), F32),
        compiler_params=_cparams(2, 40),
        name="ada_modulation",
    )(cvec, w_ada, b_ada.reshape(n_layers, 1, n_out))


def _rms(x, g):
    return x * lax.rsqrt(jnp.mean(x * x, axis=-1, keepdims=True) + EPS) * g


def _ln_mod_kernel(x_ref, g_ref, sc_ref, sh_ref, o_ref, *, transpose):
    h = _rms(x_ref[...], g_ref[...]) * (1.0 + sc_ref[0]) + sh_ref[0]
    if transpose:
        o_ref[...] = h.T.astype(o_ref.dtype)
    else:
        o_ref[...] = h.astype(o_ref.dtype)


def ln_modulate(x, g, mod3, shift_chunk, n_rows, seq, n_batch, transpose=False):
    d = x.shape[1]
    tm = 256
    seg = functools.partial(_segment, tile_rows=tm, seq=seq, n_batch=n_batch)
    if transpose:
        out_spec = pl.BlockSpec((d, tm), lambda i: (0, i))
        out_shape = jax.ShapeDtypeStruct((d, n_rows), BF16)
    else:
        out_spec = pl.BlockSpec((tm, d), lambda i: (i, 0))
        out_shape = jax.ShapeDtypeStruct((n_rows, d), BF16)
    return pl.pallas_call(
        functools.partial(_ln_mod_kernel, transpose=transpose),
        grid=(n_rows // tm,),
        in_specs=[pl.BlockSpec((tm, d), lambda i: (i, 0)),
                  pl.BlockSpec((1, d), lambda i: (0, 0)),
                  pl.BlockSpec((1, 1, d), lambda i: (seg(i), 0, shift_chunk + 1)),
                  pl.BlockSpec((1, 1, d), lambda i: (seg(i), 0, shift_chunk))],
        out_specs=out_spec,
        out_shape=out_shape,
        compiler_params=_cparams(1, 48),
        name="ln_modulate_t" if transpose else "ln_modulate",
    )(x, g.reshape(1, d), mod3, mod3)


def _mm_kernel(a_ref, b_ref, o_ref, *, act):
    acc = jnp.dot(a_ref[...], b_ref[...], preferred_element_type=F32)
    if act == "sigmoid":
        acc = jax.nn.sigmoid(acc)
    o_ref[...] = acc.astype(o_ref.dtype)


def matmul(a, b, out_dtype, act=None, n_rows=None, tn=1024, tm=512):
    m, k = a.shape
    m = n_rows or m
    n = b.shape[1]
    if n % tn:
        tn = n
    return pl.pallas_call(
        functools.partial(_mm_kernel, act=act),
        grid=(n // tn, m // tm),
        in_specs=[pl.BlockSpec((tm, k), lambda j, i: (i, 0)),
                  pl.BlockSpec((k, tn), lambda j, i: (0, j))],
        out_specs=pl.BlockSpec((tm, tn), lambda j, i: (i, j)),
        out_shape=jax.ShapeDtypeStruct((m, n), out_dtype),
        compiler_params=_cparams(2, 52),
        name="matmul_" + (act or "plain"),
    )(a, b)


def _mm_residual_kernel(a_ref, b_ref, x_ref, g_ref, o_ref):
    acc = jnp.dot(a_ref[...], b_ref[...], preferred_element_type=F32)
    o_ref[...] = x_ref[...] + g_ref[0] * acc


def matmul_gated_residual(a, b, x, mod3, gate_chunk, seq, n_batch, tn=1024, tm=512):
    m, k = a.shape
    n = b.shape[1]
    seg = functools.partial(_segment, tile_rows=tm, seq=seq, n_batch=n_batch)
    nj = n // tn
    return pl.pallas_call(
        _mm_residual_kernel,
        grid=(nj, m // tm),
        in_specs=[pl.BlockSpec((tm, k), lambda j, i: (i, 0)),
                  pl.BlockSpec((k, tn), lambda j, i: (0, j)),
                  pl.BlockSpec((tm, tn), lambda j, i: (i, j)),
                  pl.BlockSpec((1, 1, tn), lambda j, i: (seg(i), 0, gate_chunk * nj + j))],
        out_specs=pl.BlockSpec((tm, tn), lambda j, i: (i, j)),
        out_shape=jax.ShapeDtypeStruct((m, n), F32),
        compiler_params=_cparams(2, 52),
        name="matmul_gated_residual",
    )(a, b, x, mod3)


def _merge_kernel(a_ref, b_ref, c_ref, wa_ref, wb_ref, wc_ref, ga_ref, gb_ref, gc_ref, o_ref):
    m = ga_ref[...] * jnp.dot(a_ref[...], wa_ref[...], preferred_element_type=F32)
    m = m + gb_ref[...] * jnp.dot(b_ref[...], wb_ref[...], preferred_element_type=F32)
    m = m + gc_ref[...] * jnp.dot(c_ref[...], wc_ref[...], preferred_element_type=F32)
    o_ref[...] = m.astype(o_ref.dtype)


def merge_branches(a, b, c, wa, wb, wc, gates, tn=1024, tm=512):
    m = a.shape[0]
    n = wa.shape[1]
    nj = n // tn
    row = lambda j, i: (i, 0)
    col = lambda j, i: (0, j)
    return pl.pallas_call(
        _merge_kernel,
        grid=(nj, m // tm),
        in_specs=[pl.BlockSpec((tm, a.shape[1]), row),
                  pl.BlockSpec((tm, b.shape[1]), row),
                  pl.BlockSpec((tm, c.shape[1]), row),
                  pl.BlockSpec((wa.shape[0], tn), col),
                  pl.BlockSpec((wb.shape[0], tn), col),
                  pl.BlockSpec((wc.shape[0], tn), col),
                  pl.BlockSpec((tm, tn), lambda j, i: (i, j)),
                  pl.BlockSpec((tm, tn), lambda j, i: (i, nj + j)),
                  pl.BlockSpec((tm, tn), lambda j, i: (i, 2 * nj + j))],
        out_specs=pl.BlockSpec((tm, tn), lambda j, i: (i, j)),
        out_shape=jax.ShapeDtypeStruct((m, n), BF16),
        compiler_params=_cparams(2, 52),
        name="merge_branches",
    )(a, b, c, wa, wb, wc, gates, gates, gates)


def _rotate(v, c_ref, s1_ref, s2_ref):
    return (v * c_ref[...] + pltpu.roll(v, LANES - A_ROPE // 2, 1) * s1_ref[...]
            + pltpu.roll(v, A_ROPE // 2, 1) * s2_ref[...])


def _mla_q_kernel(cq_ref, g_ref, w_ref, c_ref, s1_ref, s2_ref, o_ref):
    y = _rms(cq_ref[...], g_ref[...]).astype(BF16)
    acc = jnp.dot(y, w_ref[...], preferred_element_type=F32)
    for h in range(A_HEADS):
        lo = h * A_HEAD_PAD
        o_ref[:, lo:lo + A_NOPE] = acc[:, lo:lo + A_NOPE].astype(o_ref.dtype)
        rot = _rotate(acc[:, lo + A_NOPE:lo + A_HEAD_PAD], c_ref, s1_ref, s2_ref)
        o_ref[:, lo + A_NOPE:lo + A_HEAD_PAD] = rot.astype(o_ref.dtype)


def mla_q(p_small, g, w_uq_pad, rope_c, rope_s1, rope_s2):
    nt = p_small.shape[0]
    tm = 256
    width = A_HEADS * A_HEAD_PAD
    tab = pl.BlockSpec((tm, LANES), lambda i: (i, 0))
    return pl.pallas_call(
        _mla_q_kernel,
        grid=(nt // tm,),
        in_specs=[pl.BlockSpec((tm, A_QLORA), lambda i: (i, 0)),
                  pl.BlockSpec((1, A_QLORA), lambda i: (0, 0)),
                  pl.BlockSpec((A_QLORA, width), lambda i: (0, 0)),
                  tab, tab, tab],
        out_specs=pl.BlockSpec((tm, width), lambda i: (i, 0)),
        out_shape=jax.ShapeDtypeStruct((nt, width), BF16),
        compiler_params=_cparams(1, 48),
        name="mla_q",
    )(p_small, g.reshape(1, A_QLORA), w_uq_pad, rope_c, rope_s1, rope_s2)


def _mla_kv_kernel(ckv_ref, kr_ref, g_ref, w_ref, c_ref, s1_ref, s2_ref, k_ref, v_ref):
    y = _rms(ckv_ref[...], g_ref[...]).astype(BF16)
    acc = jnp.dot(y, w_ref[...], preferred_element_type=F32)
    kr = _rotate(kr_ref[...], c_ref, s1_ref, s2_ref).astype(k_ref.dtype)
    for h in range(A_HEADS):
        lo = h * A_HEAD_PAD
        k_ref[:, lo:lo + A_NOPE] = acc[:, h * A_NOPE:(h + 1) * A_NOPE].astype(k_ref.dtype)
        k_ref[:, lo + A_NOPE:lo + A_HEAD_PAD] = kr
    v_ref[...] = acc[:, A_HEADS * A_NOPE:].astype(v_ref.dtype)


def mla_kv(p_small, g, w_ukv_perm, rope_c, rope_s1, rope_s2):
    nt = p_small.shape[0]
    tm = 256
    kw = A_HEADS * A_HEAD_PAD
    vw = A_HEADS * A_VDIM
    tab = pl.BlockSpec((tm, LANES), lambda i: (i, 0))
    return pl.pallas_call(
        _mla_kv_kernel,
        grid=(nt // tm,),
        in_specs=[pl.BlockSpec((tm, A_KVLORA), lambda i: (i, A_QLORA // A_KVLORA)),
                  pl.BlockSpec((tm, LANES), lambda i: (i, (A_QLORA + A_KVLORA) // LANES)),
                  pl.BlockSpec((1, A_KVLORA), lambda i: (0, 0)),
                  pl.BlockSpec((A_KVLORA, A_HEADS * (A_NOPE + A_VDIM)), lambda i: (0, 0)),
                  tab, tab, tab],
        out_specs=[pl.BlockSpec((tm, kw), lambda i: (i, 0)),
                   pl.BlockSpec((tm, vw), lambda i: (i, 0))],
        out_shape=[jax.ShapeDtypeStruct((nt, kw), BF16),
                   jax.ShapeDtypeStruct((nt, vw), BF16)],
        compiler_params=_cparams(1, 48),
        name="mla_kv",
    )(p_small, p_small, g.reshape(1, A_KVLORA), w_ukv_perm, rope_c, rope_s1, rope_s2)


def _qk(q, k):
    return lax.dot_general(q, k, (((1,), (1,)), ((), ())), preferred_element_type=F32)


LOG2_E = 1.4426950408889634


def _flash_two_sets_kernel(q_ref, kl_ref, vl_ref, kc_ref, vc_ref, o_ref, *, scale, chunk):
    q = q_ref[...]
    c = scale * LOG2_E
    n_chunks = kl_ref.shape[0] // chunk
    keys = lambda j: kl_ref[j * chunk:(j + 1) * chunk, :]
    t = _qk(q, kc_ref[...]) * c
    t_next = _qk(q, keys(0)) * c
    m = jnp.max(t, axis=-1, keepdims=True)
    p = jnp.exp2(t - m)
    den = jnp.sum(p, axis=-1, keepdims=True)
    acc = jnp.dot(p.astype(BF16), vc_ref[...], preferred_element_type=F32)
    for j in range(n_chunks):
        t = t_next
        if j + 1 < n_chunks:
            t_next = _qk(q, keys(j + 1)) * c
        m_new = jnp.maximum(m, jnp.max(t, axis=-1, keepdims=True))
        alpha = jnp.exp2(m - m_new)
        p = jnp.exp2(t - m_new)
        den = alpha * den + jnp.sum(p, axis=-1, keepdims=True)
        acc = alpha * acc + jnp.dot(p.astype(BF16), vl_ref[j * chunk:(j + 1) * chunk, :],
                                    preferred_element_type=F32)
        m = m_new
    o_ref[...] = (acc / den).astype(o_ref.dtype)


def mla_latent_attention(q, k, v, n_batch, seq, ctx_len, tq=1024, chunk=512):
    ctx_blk0 = (n_batch * seq) // ctx_len
    nq = seq // tq
    scale = float((A_NOPE + A_ROPE) ** -0.5)
    return pl.pallas_call(
        functools.partial(_flash_two_sets_kernel, scale=scale, chunk=chunk),
        grid=(n_batch, A_HEADS, nq),
        in_specs=[pl.BlockSpec((tq, A_HEAD_PAD), lambda b, h, i: (b * nq + i, h)),
                  pl.BlockSpec((seq, A_HEAD_PAD), lambda b, h, i: (b, h)),
                  pl.BlockSpec((seq, A_VDIM), lambda b, h, i: (b, h)),
                  pl.BlockSpec((ctx_len, A_HEAD_PAD), lambda b, h, i: (ctx_blk0 + b, h)),
                  pl.BlockSpec((ctx_len, A_VDIM), lambda b, h, i: (ctx_blk0 + b, h))],
        out_specs=pl.BlockSpec((tq, A_VDIM), lambda b, h, i: (b * nq + i, h)),
        out_shape=jax.ShapeDtypeStruct((n_batch * seq, A_HEADS * A_VDIM), BF16),
        compiler_params=_cparams(3, 48),
        name="mla_latent_attention",
    )(q, k, v, k, v)


def _attn_one_set_kernel(q_ref, k_ref, v_ref, o_ref, *, scale):
    s = _qk(q_ref[...], k_ref[...]) * scale
    p = jnp.exp(s - jnp.max(s, axis=-1, keepdims=True))
    den = jnp.sum(p, axis=-1, keepdims=True)
    o = jnp.dot(p.astype(BF16), v_ref[...], preferred_element_type=F32)
    o_ref[...] = (o / den).astype(o_ref.dtype)


def context_attention(q, k, v, n_batch, seq, ctx_len, n_heads, dqk, dv, q_col0, k_col0, v_col0, scale):
    ctx_blk0 = (n_batch * seq) // ctx_len
    return pl.pallas_call(
        functools.partial(_attn_one_set_kernel, scale=scale),
        grid=(n_batch, n_heads),
        in_specs=[pl.BlockSpec((ctx_len, dqk), lambda b, h: (ctx_blk0 + b, q_col0 + h)),
                  pl.BlockSpec((ctx_len, dqk), lambda b, h: (ctx_blk0 + b, k_col0 + h)),
                  pl.BlockSpec((ctx_len, dv), lambda b, h: (ctx_blk0 + b, v_col0 + h))],
        out_specs=pl.BlockSpec((ctx_len, dv), lambda b, h: (b, h)),
        out_shape=jax.ShapeDtypeStruct((n_batch * ctx_len, n_heads * dv), BF16),
        compiler_params=_cparams(2, 32),
        name="context_attention",
    )(q, k, v)


def _na_kernel(q_ref, kl_ref, vl_ref, kc_ref, vc_ref, bias_ref, o_ref, *, rows, scale):
    r = pl.program_id(1)
    r0 = jnp.clip(r - WIN_H // 2, 0, rows - WIN_H)
    start = pl.multiple_of(r0 * GRID_W, GRID_W)
    win = WIN_H * GRID_W
    heads = [slice(h * B_DH, (h + 1) * B_DH) for h in range(B_HEADS)]
    scores = [(_qk(q_ref[:, cols], kl_ref[pl.ds(start, win), cols]), _qk(q_ref[:, cols], kc_ref[:, cols]))
              for cols in heads]
    probs = []
    for h, (s_l, s_c) in enumerate(scores):
        s_l = s_l * scale + bias_ref[0, h]
        s_c = s_c * scale
        m = jnp.maximum(jnp.max(s_l, axis=-1, keepdims=True), jnp.max(s_c, axis=-1, keepdims=True))
        p_l = jnp.exp(s_l - m)
        p_c = jnp.exp(s_c - m)
        den = jnp.sum(p_l, axis=-1, keepdims=True) + jnp.sum(p_c, axis=-1, keepdims=True)
        probs.append((p_l.astype(BF16), p_c.astype(BF16), den))
    outs = []
    for cols, (p_l, p_c, den) in zip(heads, probs):
        o = jnp.dot(p_l, vl_ref[pl.ds(start, win), cols], preferred_element_type=F32)
        o = o + jnp.dot(p_c, vc_ref[:, cols], preferred_element_type=F32)
        outs.append(o / den)
    o_ref[...] = jnp.concatenate(outs, axis=1).astype(o_ref.dtype)


def na_latent_attention(qkv, bias, n_batch, seq, ctx_len):
    rows = seq // GRID_W
    ctx_blk0 = (n_batch * seq) // ctx_len
    win = WIN_H * GRID_W

    def bias_idx(b, r):
        return (r - jnp.clip(r - WIN_H // 2, 0, rows - WIN_H), 0, 0, 0)

    once = pl.Buffered(1)
    return pl.pallas_call(
        functools.partial(_na_kernel, rows=rows, scale=float(B_DH ** -0.5)),
        grid=(n_batch, rows),
        in_specs=[pl.BlockSpec((GRID_W, B_WIDTH), lambda b, r: (b * rows + r, 0)),
                  pl.BlockSpec((seq, B_WIDTH), lambda b, r: (b, 1), pipeline_mode=once),
                  pl.BlockSpec((seq, B_WIDTH), lambda b, r: (b, 2), pipeline_mode=once),
                  pl.BlockSpec((ctx_len, B_WIDTH), lambda b, r: (ctx_blk0 + b, 1)),
                  pl.BlockSpec((ctx_len, B_WIDTH), lambda b, r: (ctx_blk0 + b, 2)),
                  pl.BlockSpec((1, B_HEADS, GRID_W, win), bias_idx)],
        out_specs=pl.BlockSpec((GRID_W, B_WIDTH), lambda b, r: (b * rows + r, 0)),
        out_shape=jax.ShapeDtypeStruct((n_batch * seq, B_WIDTH), BF16),
        compiler_params=_cparams(2, 40),
        name="na_latent_attention",
    )(qkv, qkv, qkv, qkv, qkv, bias)


def na_bias_table(rel_bias):
    rel = rel_bias.astype(F32)
    n_layers = rel.shape[0]
    by_row = jnp.stack([rel[:, :, WIN_H - 1 - d:2 * WIN_H - 1 - d, :] for d in range(WIN_H)], axis=2)
    pad = GRID_W - 1
    by_row = jnp.pad(by_row, ((0, 0), (0, 0), (0, 0), (0, 0), (pad, pad)))
    off = WIN_W - 1 + pad
    tab = jnp.stack([by_row[..., off - c:off - c + GRID_W] for c in range(GRID_W)], axis=4)
    c = jnp.arange(GRID_W)[:, None]
    kc = jnp.arange(GRID_W)[None, :]
    c0 = jnp.clip(c - WIN_W // 2, 0, GRID_W - WIN_W)
    inside = (kc >= c0) & (kc < c0 + WIN_W)
    tab = jnp.where(inside[None, None, None, None], tab, NEG_BIG)
    return tab.transpose(0, 2, 1, 4, 3, 5).reshape(n_layers, WIN_H, B_HEADS, GRID_W, WIN_H * GRID_W)


def _pool_kernel(x_ref, w_ref, sc_ref, o_ref, pad_ref, *, seq_len, chunk):
    g = pl.program_id(1)
    zeros = jnp.zeros((POOL_HALO, C_GW), F32)
    pad_ref[0:POOL_HALO, :] = zeros
    pad_ref[POOL_HALO + seq_len:2 * POOL_HALO + seq_len, :] = zeros
    pad_ref[POOL_HALO:POOL_HALO + seq_len, :] = x_ref[...]
    for gi, w in enumerate(POOL_SIZES):
        @pl.when(g == gi)
        def _(w=w):
            for c0 in range(0, seq_len, chunk):
                acc = None
                for k in range(-(w // 2), w - w // 2):
                    piece = pad_ref[POOL_HALO + c0 + k:POOL_HALO + c0 + k + chunk, :]
                    acc = piece if acc is None else acc + piece
                t = c0 + lax.broadcasted_iota(jnp.int32, (chunk, C_GW), 0)
                cnt = jnp.minimum(t + (w - w // 2), seq_len) - jnp.maximum(t - w // 2, 0)
                d = (acc / cnt.astype(F32) - x_ref[c0:c0 + chunk, :]).astype(BF16)
                y = jnp.dot(d, w_ref[0], preferred_element_type=F32) * sc_ref[...]
                o_ref[c0:c0 + chunk, :] = y.astype(o_ref.dtype)


def pool_mix(p_pool, pool_w, pool_scale, n_seq, seq_len, first_block):
    chunk = min(seq_len, 512)
    return pl.pallas_call(
        functools.partial(_pool_kernel, seq_len=seq_len, chunk=chunk),
        grid=(n_seq, len(POOL_SIZES)),
        in_specs=[pl.BlockSpec((seq_len, C_GW), lambda b, g: (first_block + b, g)),
                  pl.BlockSpec((1, C_GW, C_GW), lambda b, g: (g, 0, 0)),
                  pl.BlockSpec((1, C_GW), lambda b, g: (0, g))],
        out_specs=pl.BlockSpec((seq_len, C_GW), lambda b, g: (b, g)),
        out_shape=jax.ShapeDtypeStruct((n_seq * seq_len, C_WIDTH), BF16),
        scratch_shapes=[pltpu.VMEM((seq_len + 2 * POOL_HALO, C_GW), F32)],
        compiler_params=_cparams(2, 48),
        name="pool_mix",
    )(p_pool, pool_w, pool_scale.reshape(1, C_WIDTH))


_REGION = [(r1, r2) for r1 in range(P_TOPK) for r2 in range(P_TOPK) if (r1 + 1) * (r2 + 1) <= P_TOPK]


def _dominates(a, b):
    return a != b and a[0] <= b[0] and a[1] <= b[1]


def _router_kernel(h_ref, wq_ref, k1_ref, k2_ref, cnt_ref, a_ref, rk_ref, b_ref,
                   s1_scr, s2_scr, v1_scr, i1_scr, v2_scr, i2_scr, cr_scr, e1_scr, e2_scr):
    tm = h_ref.shape[1]
    half = P_HEADS * P_DK // 2
    q_t = jnp.dot(wq_ref[...], h_ref[...], preferred_element_type=F32)
    key = lax.broadcasted_iota(jnp.int32, (P_NKEYS, P_HEADS, tm), 0)
    s1_scr[...] = jnp.dot(k1_ref[...], q_t[:half].astype(BF16),
                          preferred_element_type=F32).reshape(P_NKEYS, P_HEADS, tm)
    s2_scr[...] = jnp.dot(k2_ref[...], q_t[half:].astype(BF16),
                          preferred_element_type=F32).reshape(P_NKEYS, P_HEADS, tm)

    def first_max(s):
        level = [(s[k], k) for k in range(P_NKEYS)]
        while len(level) > 1:
            nxt = []
            for (va, ia), (vb, ib) in zip(level[0::2], level[1::2]):
                take = vb > va
                nxt.append((jnp.where(take, vb, va), jnp.where(take, ib, ia)))
            level = nxt
        return level[0]

    def extract(r, carry):
        for s_scr, v_scr, i_scr in ((s1_scr, v1_scr, i1_scr), (s2_scr, v2_scr, i2_scr)):
            s = s_scr[...]
            m, idx = first_max(s)
            s_scr[...] = jnp.where(key == idx[None], -jnp.inf, s)
            v_scr[r] = m
            i_scr[r] = idx
        return carry

    lax.fori_loop(0, P_TOPK, extract, 0)

    v1 = [v1_scr[r] for r in range(P_TOPK)]
    v2 = [v2_scr[r] for r in range(P_TOPK)]
    sums = {c: v1[c[0]] + v2[c[1]] for c in _REGION}
    beaten = {c: float(sum(_dominates(o, c) for o in _REGION)) for c in _REGION}
    for ci, c in enumerate(_REGION):
        for o in _REGION[:ci]:
            if _dominates(o, c):
                continue
            o_first = (sums[o] >= sums[c]).astype(F32)
            beaten[c] = beaten[c] + o_first
            beaten[o] = beaten[o] + (1.0 - o_first)
    e1 = [jnp.exp(v1[r] - v1[0]) for r in range(P_TOPK)]
    e2 = [jnp.exp(v2[r] - v2[0]) for r in range(P_TOPK)]
    z = jnp.zeros_like(v1[0])
    counts = [jnp.zeros_like(v1[0]) for _ in range(P_TOPK)]
    for c in _REGION:
        chosen = beaten[c] < float(P_TOPK)
        z = z + jnp.where(chosen, e1[c[0]] * e2[c[1]], 0.0)
        counts[c[0]] = counts[c[0]] + jnp.where(chosen, 1.0, 0.0)
    inv_z = 1.0 / z
    for r in range(P_TOPK):
        cr_scr[r] = counts[r]
        e1_scr[r] = e1[r]
        e2_scr[r] = e2[r] * inv_z

    key2 = lax.broadcasted_iota(jnp.int32, (P_NKEYS, LANES), 0)

    def expand(n, carry, h):
        t0 = pl.multiple_of(n * LANES, LANES)
        row = lambda ref, r: ref[r, h:h + 1, pl.ds(t0, LANES)]
        cnt = jnp.zeros((P_NKEYS, LANES), F32)
        a = cnt
        for r in range(P_TOPK):
            hit = key2 == row(i1_scr, r)
            cnt = jnp.where(hit, row(cr_scr, r), cnt)
            a = jnp.where(hit, row(e1_scr, r), a)
        cnt_ref[h, :, pl.ds(t0, LANES)] = cnt
        a_ref[h, :, pl.ds(t0, LANES)] = a
        rk = jnp.full((P_NKEYS, LANES), float(P_TOPK), F32)
        b = jnp.zeros((P_NKEYS, LANES), F32)
        for r in range(P_TOPK):
            hit = key2 == row(i2_scr, r)
            rk = jnp.where(hit, float(r), rk)
            b = jnp.where(hit, row(e2_scr, r), b)
        rk_ref[h, :, pl.ds(t0, LANES)] = rk.astype(rk_ref.dtype)
        b_ref[h, :, pl.ds(t0, LANES)] = b.astype(b_ref.dtype)
        return carry

    for h in range(P_HEADS):
        lax.fori_loop(0, tm // LANES, functools.partial(expand, h=h), 0)


def peer_router(h_t, wq_t, k1_packed, k2_packed, tm=256):
    d, n = h_t.shape
    qw = wq_t.shape[0]
    kp = k1_packed.shape[0]
    tab = lambda dt: jax.ShapeDtypeStruct((P_HEADS, P_NKEYS, n), dt)
    tab_spec = pl.BlockSpec((P_HEADS, P_NKEYS, tm), lambda i: (0, 0, i))
    small = lambda dt: pltpu.VMEM((P_TOPK, P_HEADS, tm), dt)
    return pl.pallas_call(
        _router_kernel,
        grid=(n // tm,),
        in_specs=[pl.BlockSpec((d, tm), lambda i: (0, i)),
                  pl.BlockSpec((qw, d), lambda i: (0, 0), pipeline_mode=pl.Buffered(1)),
                  pl.BlockSpec((kp, kp), lambda i: (0, 0), pipeline_mode=pl.Buffered(1)),
                  pl.BlockSpec((kp, kp), lambda i: (0, 0), pipeline_mode=pl.Buffered(1))],
        out_specs=[tab_spec] * 4,
        out_shape=[tab(F32), tab(F32), tab(GATE_DTYPE), tab(GATE_DTYPE)],
        scratch_shapes=[pltpu.VMEM((P_NKEYS, P_HEADS, tm), F32), pltpu.VMEM((P_NKEYS, P_HEADS, tm), F32),
                        small(F32), small(jnp.int32), small(F32), small(jnp.int32),
                        small(F32), small(F32), small(F32)],
        compiler_params=_cparams(1, 52),
        name="peer_router",
    )(h_t, wq_t, k1_packed, k2_packed)


def _gelu_exact(x):
    return 0.5 * x * (1.0 + lax.erf(x * float(np.sqrt(0.5))))


def _peer_kernel(h_ref, u_ref, v_ref, cnt_ref, a_ref, rk_ref, b_ref, o_ref):
    e = pl.program_id(1)
    te = u_ref.shape[0]
    tm = h_ref.shape[1]

    def step(first):
        halves = [(c0, jnp.dot(u_ref[c0:c0 + te // 2, :], h_ref[...], preferred_element_type=F32))
                  for c0 in (0, te // 2)]
        rows_per_half = te // 2 // P_NKEYS
        n_slab = P_NKEYS // GATE_SLAB
        ga_rows = []
        for c0, pre in halves:
            i0 = c0 // P_NKEYS
            rows = [[(cnt_ref[h, i0 + k:i0 + k + 1, :], a_ref[h, i0 + k:i0 + k + 1, :]) for h in range(P_HEADS)]
                    for k in range(rows_per_half)]
            tiles = {}
            for t0 in range(0, tm, LANES):
                lanes = slice(t0, t0 + LANES)
                g = [[None] * n_slab for _ in range(rows_per_half)]
                for h in range(P_HEADS):
                    per_row = [[jnp.broadcast_to(x[:, lanes].astype(GATE_DTYPE), (GATE_SLAB, LANES)) for x in rows[k][h]]
                               for k in range(rows_per_half)]
                    for s in range(n_slab):
                        slab = slice(s * GATE_SLAB, (s + 1) * GATE_SLAB)
                        rk = rk_ref[h, slab, lanes]
                        b = b_ref[h, slab, lanes]
                        for k, (cnt, a) in enumerate(per_row):
                            term = jnp.where(rk < cnt, b * a, jnp.zeros_like(b))
                            g[k][s] = term if g[k][s] is None else g[k][s] + term
                for k in range(rows_per_half):
                    for s in range(n_slab):
                        r = k * P_NKEYS + s * GATE_SLAB
                        tiles[(k, s, t0)] = g[k][s].astype(F32) * _gelu_exact(pre[r:r + GATE_SLAB, lanes])
            for k in range(rows_per_half):
                for s in range(n_slab):
                    ga_rows.append(jnp.concatenate([tiles[(k, s, t0)] for t0 in range(0, tm, LANES)], axis=1))
        ga = jnp.concatenate(ga_rows, axis=0).astype(BF16)
        contrib = lax.dot_general(v_ref[...], ga, (((0,), (0,)), ((), ())), preferred_element_type=F32)
        if first:
            o_ref[...] = contrib
        else:
            o_ref[...] += contrib

    pl.when(e == 0)(functools.partial(step, True))
    pl.when(e > 0)(functools.partial(step, False))


def peer_mixture(h_t, u_all, v_all, layer, tables, tm=512, te=1024):
    d, n = h_t.shape
    once = pl.Buffered(1)
    key_tab = pl.BlockSpec((P_HEADS, P_NKEYS, tm), lambda i, e: (0, 0, i), pipeline_mode=once)
    row_tab = pl.BlockSpec((P_HEADS, te // P_NKEYS, tm), lambda i, e: (0, e, i))
    expert_tile = pl.BlockSpec((None, te, d), lambda i, e: (layer, e, 0))
    return pl.pallas_call(
        _peer_kernel,
        grid=(n // tm, u_all.shape[1] // te),
        in_specs=[pl.BlockSpec((d, tm), lambda i, e: (0, i), pipeline_mode=once),
                  expert_tile, expert_tile, row_tab, row_tab, key_tab, key_tab],
        out_specs=pl.BlockSpec((d, tm), lambda i, e: (0, i), pipeline_mode=once),
        out_shape=jax.ShapeDtypeStruct((d, n), F32),
        compiler_params=_cparams(2, 58),
        name="peer_mixture",
    )(h_t, u_all, v_all, *tables)


def _residual_t_ln_kernel(x_ref, y_ref, g_ref, ng_ref, sc_ref, sh_ref, o_ref, h_ref):
    x = x_ref[...] + g_ref[0] * y_ref[...].T
    o_ref[...] = x
    h_ref[...] = (_rms(x, ng_ref[...]) * (1.0 + sc_ref[0]) + sh_ref[0]).astype(h_ref.dtype)


def _residual_t_final_kernel(x_ref, y_ref, g_ref, ng_ref, o_ref):
    o_ref[...] = _rms(x_ref[...] + g_ref[0] * y_ref[...].T, ng_ref[...])


def residual_from_transposed(x, y_t, mod3, gate_chunk, n_rows, seq, n_batch, norm_g, next_mod3=None):
    d = x.shape[1]
    tm = 256
    seg = functools.partial(_segment, tile_rows=tm, seq=seq, n_batch=n_batch)
    row = pl.BlockSpec((tm, d), lambda i: (i, 0))
    in_specs = [row,
                pl.BlockSpec((d, tm), lambda i: (0, i)),
                pl.BlockSpec((1, 1, d), lambda i: (seg(i), 0, gate_chunk)),
                pl.BlockSpec((1, d), lambda i: (0, 0))]
    if next_mod3 is None:
        return pl.pallas_call(
            _residual_t_final_kernel,
            grid=(n_rows // tm,),
            in_specs=in_specs,
            out_specs=row,
            out_shape=jax.ShapeDtypeStruct((n_rows, d), F32),
            compiler_params=_cparams(1, 48),
            name="residual_final_norm",
        )(x, y_t, mod3, norm_g.reshape(1, d))
    return pl.pallas_call(
        _residual_t_ln_kernel,
        grid=(n_rows // tm,),
        in_specs=in_specs + [pl.BlockSpec((1, 1, d), lambda i: (seg(i), 0, 1)),
                             pl.BlockSpec((1, 1, d), lambda i: (seg(i), 0, 0))],
        out_specs=[row, row],
        out_shape=[jax.ShapeDtypeStruct((n_rows, d), F32), jax.ShapeDtypeStruct((n_rows, d), BF16)],
        compiler_params=_cparams(1, 48),
        name="residual_next_norm",
    )(x, y_t, mod3, norm_g.reshape(1, d), next_mod3, next_mod3)


def _rope_tables(seq, n_batch, n_ctx_rows):
    t = jnp.arange(seq)
    row = (t // GRID_W).astype(F32)
    col = (t % GRID_W).astype(F32)
    n_freq = A_ROPE // 4
    freqs = ROPE_THETA ** (-jnp.arange(n_freq, dtype=F32) / n_freq)
    ang = jnp.concatenate([row[:, None] * freqs, col[:, None] * freqs], axis=-1)
    cos, sin = jnp.cos(ang), jnp.sin(ang)
    zero = jnp.zeros_like(cos)
    pad = jnp.zeros((seq, LANES - A_ROPE), F32)
    c = jnp.concatenate([cos, cos, pad], axis=1)
    s1 = jnp.concatenate([-sin, zero, pad], axis=1)
    s2 = jnp.concatenate([zero, sin, pad], axis=1)
    ident = jnp.concatenate([jnp.ones((n_ctx_rows, A_ROPE), F32), jnp.zeros((n_ctx_rows, LANES - A_ROPE), F32)], 1)
    none = jnp.zeros((n_ctx_rows, LANES), F32)
    tile = lambda a, ctx: jnp.concatenate([jnp.tile(a, (n_batch, 1)), ctx], axis=0)
    return tile(c, ident), tile(s1, none), tile(s2, none)


def _pack_keys(k):
    h, nk, dk = k.shape
    eye = jnp.eye(h, dtype=k.dtype)
    blk = k.transpose(1, 0, 2)[:, :, None, :] * eye[None, :, :, None]
    return blk.reshape(nk * h, h * dk).astype(BF16)


def _prepare_layer(w_in, w_uq, w_ukv, w_br_a, w_br_b, w_br_c, w_out, peer_wq, peer_k1, peer_k2):
    d = w_in.shape[0]
    o_kr = A_QLORA + A_KVLORA
    o_q = o_kr + A_ROPE
    o_pool = o_q + 3 * B_WIDTH
    o_gate = o_pool + C_WIDTH
    w_small = jnp.concatenate([w_in[:, :o_q], jnp.zeros((d, LANES - A_ROPE), w_in.dtype)], axis=1)
    w_uq_pad = jnp.pad(w_uq.reshape(A_QLORA, A_HEADS, A_NOPE + A_ROPE),
                       ((0, 0), (0, 0), (0, A_HEAD_PAD - A_NOPE - A_ROPE))).reshape(A_QLORA, A_HEADS * A_HEAD_PAD)
    w_ukv_perm = w_ukv.reshape(A_KVLORA, A_HEADS, 2, A_NOPE).transpose(0, 2, 1, 3).reshape(A_KVLORA, -1)
    wq_t = peer_wq.reshape(d, P_HEADS, 2, P_DK // 2).transpose(2, 1, 3, 0).reshape(P_HEADS * P_DK, d)
    return dict(
        w_small=w_small.astype(BF16),
        w_qkv=w_in[:, o_q:o_pool].astype(BF16),
        w_pool=w_in[:, o_pool:o_gate].astype(BF16),
        w_gate=w_in[:, o_gate:].astype(BF16),
        w_uq=w_uq_pad.astype(BF16),
        w_ukv=w_ukv_perm.astype(BF16),
        w_br_a=w_br_a.astype(BF16), w_br_b=w_br_b.astype(BF16), w_br_c=w_br_c.astype(BF16),
        w_out=w_out.astype(BF16),
        wq_t=wq_t.astype(BF16),
        k1=_pack_keys(peer_k1), k2=_pack_keys(peer_k2),
    )


def kernel(x, c, ctx, c_ctx, w_ada, b_ada, norm1_g, norm2_g, w_in, q_norm_g, kv_norm_g, w_uq, w_ukv, na_rel_bias, pool_w, pool_scale, w_br_a, w_br_b, w_br_c, w_out, peer_wq, peer_k1, peer_k2, peer_u, peer_v, final_g):
    n_batch, seq, d = x.shape
    ctx_len = ctx.shape[1]
    depth = w_ada.shape[0]
    n_lat = n_batch * seq
    n_ctx = n_batch * ctx_len
    nt = n_lat + n_ctx
    ctx_blk0 = n_lat // ctx_len

    stream = jnp.concatenate([x.reshape(n_lat, d), ctx.reshape(n_ctx, d)], axis=0)
    mod_rows = 8
    cvec = jnp.concatenate([c, c_ctx[None], jnp.zeros((mod_rows - n_batch - 1, d), c.dtype)], axis=0)
    mod = ada_modulation(cvec, w_ada, b_ada)
    rope_c, rope_s1, rope_s2 = _rope_tables(seq, n_batch, n_ctx)
    na_bias = na_bias_table(na_rel_bias)
    mod3s = [mod[l].reshape(mod_rows, 1, 6 * d) for l in range(depth)]
    u_all = peer_u.astype(BF16)
    v_all = peer_v.astype(BF16)

    h = ln_modulate(stream, norm1_g[0], mod3s[0], 0, nt, seq, n_batch)
    for l in range(depth):
        ctx_out = l < depth - 1
        n_act = nt if ctx_out else n_lat
        wl = _prepare_layer(w_in[l], w_uq[l], w_ukv[l], w_br_a[l], w_br_b[l], w_br_c[l], w_out[l],
                            peer_wq[l], peer_k1[l], peer_k2[l])
        mod3 = mod3s[l]

        p_small = matmul(h, wl["w_small"], F32)
        qkv = matmul(h, wl["w_qkv"], BF16)
        p_pool = matmul(h, wl["w_pool"], F32)
        gates = matmul(h, wl["w_gate"], F32, act="sigmoid", n_rows=n_act)

        q = mla_q(p_small, q_norm_g[l], wl["w_uq"], rope_c, rope_s1, rope_s2)
        k, v = mla_kv(p_small, kv_norm_g[l], wl["w_ukv"], rope_c, rope_s1, rope_s2)
        a_br = mla_latent_attention(q, k, v, n_batch, seq, ctx_len)
        b_br = na_latent_attention(qkv, na_bias[l], n_batch, seq, ctx_len)
        pw = pool_w[l].astype(BF16)
        c_br = pool_mix(p_pool, pw, pool_scale[l], n_batch, seq, 0)
        if ctx_out:
            a_ctx = context_attention(q, k, v, n_batch, seq, ctx_len, A_HEADS, A_HEAD_PAD, A_VDIM, 0, 0, 0,
                                      float((A_NOPE + A_ROPE) ** -0.5))
            b_ctx = context_attention(qkv, qkv, qkv, n_batch, seq, ctx_len, B_HEADS, B_DH, B_DH,
                                      0, B_HEADS, 2 * B_HEADS, float(B_DH ** -0.5))
            c_ctx_br = pool_mix(p_pool, pw, pool_scale[l], n_batch, ctx_len, ctx_blk0)
            a_br = jnp.concatenate([a_br, a_ctx], axis=0)
            b_br = jnp.concatenate([b_br, b_ctx], axis=0)
            c_br = jnp.concatenate([c_br, c_ctx_br], axis=0)

        merged = merge_branches(a_br, b_br, c_br, wl["w_br_a"], wl["w_br_b"], wl["w_br_c"], gates)
        stream = matmul_gated_residual(merged, wl["w_out"], stream, mod3, 2, seq, n_batch)

        h2_t = ln_modulate(stream, norm2_g[l], mod3, 3, n_act, seq, n_batch, transpose=True)
        tables = peer_router(h2_t, wl["wq_t"], wl["k1"], wl["k2"])
        y_t = peer_mixture(h2_t, u_all, v_all, l, tables)
        if ctx_out:
            stream, h = residual_from_transposed(stream, y_t, mod3, 5, n_act, seq, n_batch,
                                                 norm1_g[l + 1], mod3s[l + 1])
        else:
            out = residual_from_transposed(stream, y_t, mod3, 5, n_act, seq, n_batch, final_g)

    return out.reshape(n_batch, seq, d)
```

```python
import functools

import numpy as np
import jax
import jax.numpy as jnp
from jax import lax
from jax.experimental import pallas as pl
from jax.experimental.pallas import tpu as pltpu

F32 = jnp.float32
BF16 = jnp.bfloat16

GRID_W = 64
EPS = 1e-6
ROPE_THETA = 10000.0

A_HEADS = 16
A_NOPE = 128
A_ROPE = 64
A_VDIM = 128
A_QLORA = 768
A_KVLORA = 256
A_HEAD_PAD = 256

B_HEADS = 8
B_DH = 128
B_WIDTH = B_HEADS * B_DH
WIN_H = 8
WIN_W = 16

C_WIDTH = 1024
POOL_SIZES = (2, 4, 8, 16)
C_GW = C_WIDTH // len(POOL_SIZES)
POOL_HALO = 8

P_HEADS = 8
P_NKEYS = 128
P_DK = 256
P_TOPK = 16

LANES = 128
SUBLANES = 8
GATE_DTYPE = BF16
GATE_SLAB = 16
SMALL_W = A_QLORA + A_KVLORA + LANES

NEG_BIG = -1e30
MIB = 1024 * 1024


def _cparams(n_axes, vmem_mib):
    return pltpu.CompilerParams(dimension_semantics=("arbitrary",) * n_axes,
                                vmem_limit_bytes=vmem_mib * MIB)


def _segment(row_tile, tile_rows, seq, n_batch):
    return jnp.minimum((row_tile * tile_rows) // seq, n_batch)


def _ada_kernel(c_ref, w_ref, b_ref, o_ref):
    c = c_ref[...]
    a = (c * jax.nn.sigmoid(c)).astype(BF16)
    o_ref[0] = jnp.dot(a, w_ref[0].astype(BF16), preferred_element_type=F32) + b_ref[0]


def ada_modulation(cvec, w_ada, b_ada):
    n_layers, d, n_out = w_ada.shape
    rows = cvec.shape[0]
    tn = 512
    return pl.pallas_call(
        _ada_kernel,
        grid=(n_layers, n_out // tn),
        in_specs=[pl.BlockSpec((rows, d), lambda l, j: (0, 0)),
                  pl.BlockSpec((1, d, tn), lambda l, j: (l, 0, j)),
                  pl.BlockSpec((1, 1, tn), lambda l, j: (l, 0, j))],
        out_specs=pl.BlockSpec((1, rows, tn), lambda l, j: (l, 0, j)),
        out_shape=jax.ShapeDtypeStruct((n_layers, rows, n_out), F32),
        compiler_params=_cparams(2, 40),
        name="ada_modulation",
    )(cvec, w_ada, b_ada.reshape(n_layers, 1, n_out))


def _rms(x, g):
    return x * lax.rsqrt(jnp.mean(x * x, axis=-1, keepdims=True) + EPS) * g


def _ln_mod_kernel(x_ref, g_ref, sc_ref, sh_ref, o_ref, *, transpose):
    h = _rms(x_ref[...], g_ref[...]) * (1.0 + sc_ref[0]) + sh_ref[0]
    if transpose:
        o_ref[...] = h.T.astype(o_ref.dtype)
    else:
        o_ref[...] = h.astype(o_ref.dtype)


def ln_modulate(x, g, mod3, shift_chunk, n_rows, seq, n_batch, transpose=False):
    d = x.shape[1]
    tm = 256
    seg = functools.partial(_segment, tile_rows=tm, seq=seq, n_batch=n_batch)
    if transpose:
        out_spec = pl.BlockSpec((d, tm), lambda i: (0, i))
        out_shape = jax.ShapeDtypeStruct((d, n_rows), BF16)
    else:
        out_spec = pl.BlockSpec((tm, d), lambda i: (i, 0))
        out_shape = jax.ShapeDtypeStruct((n_rows, d), BF16)
    return pl.pallas_call(
        functools.partial(_ln_mod_kernel, transpose=transpose),
        grid=(n_rows // tm,),
        in_specs=[pl.BlockSpec((tm, d), lambda i: (i, 0)),
                  pl.BlockSpec((1, d), lambda i: (0, 0)),
                  pl.BlockSpec((1, 1, d), lambda i: (seg(i), 0, shift_chunk + 1)),
                  pl.BlockSpec((1, 1, d), lambda i: (seg(i), 0, shift_chunk))],
        out_specs=out_spec,
        out_shape=out_shape,
        compiler_params=_cparams(1, 48),
        name="ln_modulate_t" if transpose else "ln_modulate",
    )(x, g.reshape(1, d), mod3, mod3)


def _mm_kernel(a_ref, b_ref, o_ref, *, act):
    acc = jnp.dot(a_ref[...], b_ref[...], preferred_element_type=F32)
    if act == "sigmoid":
        acc = jax.nn.sigmoid(acc)
    o_ref[...] = acc.astype(o_ref.dtype)


def matmul(a, b, out_dtype, act=None, n_rows=None, tn=1024, tm=512):
    m, k = a.shape
    m = n_rows or m
    n = b.shape[1]
    if n % tn:
        tn = n
    return pl.pallas_call(
        functools.partial(_mm_kernel, act=act),
        grid=(n // tn, m // tm),
        in_specs=[pl.BlockSpec((tm, k), lambda j, i: (i, 0)),
                  pl.BlockSpec((k, tn), lambda j, i: (0, j))],
        out_specs=pl.BlockSpec((tm, tn), lambda j, i: (i, j)),
        out_shape=jax.ShapeDtypeStruct((m, n), out_dtype),
        compiler_params=_cparams(2, 52),
        name="matmul_" + (act or "plain"),
    )(a, b)


def _mm_residual_kernel(a_ref, b_ref, x_ref, g_ref, o_ref):
    acc = jnp.dot(a_ref[...], b_ref[...], preferred_element_type=F32)
    o_ref[...] = x_ref[...] + g_ref[0] * acc


def matmul_gated_residual(a, b, x, mod3, gate_chunk, seq, n_batch, tn=1024, tm=512):
    m, k = a.shape
    n = b.shape[1]
    seg = functools.partial(_segment, tile_rows=tm, seq=seq, n_batch=n_batch)
    nj = n // tn
    return pl.pallas_call(
        _mm_residual_kernel,
        grid=(nj, m // tm),
        in_specs=[pl.BlockSpec((tm, k), lambda j, i: (i, 0)),
                  pl.BlockSpec((k, tn), lambda j, i: (0, j)),
                  pl.BlockSpec((tm, tn), lambda j, i: (i, j)),
                  pl.BlockSpec((1, 1, tn), lambda j, i: (seg(i), 0, gate_chunk * nj + j))],
        out_specs=pl.BlockSpec((tm, tn), lambda j, i: (i, j)),
        out_shape=jax.ShapeDtypeStruct((m, n), F32),
        compiler_params=_cparams(2, 52),
        name="matmul_gated_residual",
    )(a, b, x, mod3)


def _merge_kernel(a_ref, b_ref, c_ref, wa_ref, wb_ref, wc_ref, ga_ref, gb_ref, gc_ref, o_ref):
    m = ga_ref[...] * jnp.dot(a_ref[...], wa_ref[...], preferred_element_type=F32)
    m = m + gb_ref[...] * jnp.dot(b_ref[...], wb_ref[...], preferred_element_type=F32)
    m = m + gc_ref[...] * jnp.dot(c_ref[...], wc_ref[...], preferred_element_type=F32)
    o_ref[...] = m.astype(o_ref.dtype)


def merge_branches(a, b, c, wa, wb, wc, gates, tn=1024, tm=512):
    m = a.shape[0]
    n = wa.shape[1]
    nj = n // tn
    row = lambda j, i: (i, 0)
    col = lambda j, i: (0, j)
    return pl.pallas_call(
        _merge_kernel,
        grid=(nj, m // tm),
        in_specs=[pl.BlockSpec((tm, a.shape[1]), row),
                  pl.BlockSpec((tm, b.shape[1]), row),
                  pl.BlockSpec((tm, c.shape[1]), row),
                  pl.BlockSpec((wa.shape[0], tn), col),
                  pl.BlockSpec((wb.shape[0], tn), col),
                  pl.BlockSpec((wc.shape[0], tn), col),
                  pl.BlockSpec((tm, tn), lambda j, i: (i, j)),
                  pl.BlockSpec((tm, tn), lambda j, i: (i, nj + j)),
                  pl.BlockSpec((tm, tn), lambda j, i: (i, 2 * nj + j))],
        out_specs=pl.BlockSpec((tm, tn), lambda j, i: (i, j)),
        out_shape=jax.ShapeDtypeStruct((m, n), BF16),
        compiler_params=_cparams(2, 52),
        name="merge_branches",
    )(a, b, c, wa, wb, wc, gates, gates, gates)


def _rotate(v, c_ref, s1_ref, s2_ref):
    return (v * c_ref[...] + pltpu.roll(v, LANES - A_ROPE // 2, 1) * s1_ref[...]
            + pltpu.roll(v, A_ROPE // 2, 1) * s2_ref[...])


def _mla_q_kernel(cq_ref, g_ref, w_ref, c_ref, s1_ref, s2_ref, o_ref):
    y = _rms(cq_ref[...], g_ref[...]).astype(BF16)
    acc = jnp.dot(y, w_ref[...], preferred_element_type=F32)
    for h in range(A_HEADS):
        lo = h * A_HEAD_PAD
        o_ref[:, lo:lo + A_NOPE] = acc[:, lo:lo + A_NOPE].astype(o_ref.dtype)
        rot = _rotate(acc[:, lo + A_NOPE:lo + A_HEAD_PAD], c_ref, s1_ref, s2_ref)
        o_ref[:, lo + A_NOPE:lo + A_HEAD_PAD] = rot.astype(o_ref.dtype)


def mla_q(p_small, g, w_uq_pad, rope_c, rope_s1, rope_s2):
    nt = p_small.shape[0]
    tm = 256
    width = A_HEADS * A_HEAD_PAD
    tab = pl.BlockSpec((tm, LANES), lambda i: (i, 0))
    return pl.pallas_call(
        _mla_q_kernel,
        grid=(nt // tm,),
        in_specs=[pl.BlockSpec((tm, A_QLORA), lambda i: (i, 0)),
                  pl.BlockSpec((1, A_QLORA), lambda i: (0, 0)),
                  pl.BlockSpec((A_QLORA, width), lambda i: (0, 0)),
                  tab, tab, tab],
        out_specs=pl.BlockSpec((tm, width), lambda i: (i, 0)),
        out_shape=jax.ShapeDtypeStruct((nt, width), BF16),
        compiler_params=_cparams(1, 48),
        name="mla_q",
    )(p_small, g.reshape(1, A_QLORA), w_uq_pad, rope_c, rope_s1, rope_s2)


def _mla_kv_kernel(ckv_ref, kr_ref, g_ref, w_ref, c_ref, s1_ref, s2_ref, k_ref, v_ref):
    y = _rms(ckv_ref[...], g_ref[...]).astype(BF16)
    acc = jnp.dot(y, w_ref[...], preferred_element_type=F32)
    kr = _rotate(kr_ref[...], c_ref, s1_ref, s2_ref).astype(k_ref.dtype)
    for h in range(A_HEADS):
        lo = h * A_HEAD_PAD
        k_ref[:, lo:lo + A_NOPE] = acc[:, h * A_NOPE:(h + 1) * A_NOPE].astype(k_ref.dtype)
        k_ref[:, lo + A_NOPE:lo + A_HEAD_PAD] = kr
    v_ref[...] = acc[:, A_HEADS * A_NOPE:].astype(v_ref.dtype)


def mla_kv(p_small, g, w_ukv_perm, rope_c, rope_s1, rope_s2):
    nt = p_small.shape[0]
    tm = 256
    kw = A_HEADS * A_HEAD_PAD
    vw = A_HEADS * A_VDIM
    tab = pl.BlockSpec((tm, LANES), lambda i: (i, 0))
    return pl.pallas_call(
        _mla_kv_kernel,
        grid=(nt // tm,),
        in_specs=[pl.BlockSpec((tm, A_KVLORA), lambda i: (i, A_QLORA // A_KVLORA)),
                  pl.BlockSpec((tm, LANES), lambda i: (i, (A_QLORA + A_KVLORA) // LANES)),
                  pl.BlockSpec((1, A_KVLORA), lambda i: (0, 0)),
                  pl.BlockSpec((A_KVLORA, A_HEADS * (A_NOPE + A_VDIM)), lambda i: (0, 0)),
                  tab, tab, tab],
        out_specs=[pl.BlockSpec((tm, kw), lambda i: (i, 0)),
                   pl.BlockSpec((tm, vw), lambda i: (i, 0))],
        out_shape=[jax.ShapeDtypeStruct((nt, kw), BF16),
                   jax.ShapeDtypeStruct((nt, vw), BF16)],
        compiler_params=_cparams(1, 48),
        name="mla_kv",
    )(p_small, p_small, g.reshape(1, A_KVLORA), w_ukv_perm, rope_c, rope_s1, rope_s2)


def _qk(q, k):
    return lax.dot_general(q, k, (((1,), (1,)), ((), ())), preferred_element_type=F32)


LOG2_E = 1.4426950408889634


def _flash_two_sets_kernel(q_ref, kl_ref, vl_ref, kc_ref, vc_ref, o_ref, *, scale, chunk):
    q = q_ref[...]
    c = scale * LOG2_E
    n_chunks = kl_ref.shape[0] // chunk
    keys = lambda j: kl_ref[j * chunk:(j + 1) * chunk, :]
    t = _qk(q, kc_ref[...]) * c
    t_next = _qk(q, keys(0)) * c
    m = jnp.max(t, axis=-1, keepdims=True)
    p = jnp.exp2(t - m)
    den = jnp.sum(p, axis=-1, keepdims=True)
    acc = jnp.dot(p.astype(BF16), vc_ref[...], preferred_element_type=F32)
    for j in range(n_chunks):
        t = t_next
        if j + 1 < n_chunks:
            t_next = _qk(q, keys(j + 1)) * c
        m_new = jnp.maximum(m, jnp.max(t, axis=-1, keepdims=True))
        alpha = jnp.exp2(m - m_new)
        p = jnp.exp2(t - m_new)
        den = alpha * den + jnp.sum(p, axis=-1, keepdims=True)
        acc = alpha * acc + jnp.dot(p.astype(BF16), vl_ref[j * chunk:(j + 1) * chunk, :],
                                    preferred_element_type=F32)
        m = m_new
    o_ref[...] = (acc / den).astype(o_ref.dtype)


def mla_latent_attention(q, k, v, n_batch, seq, ctx_len, tq=1024, chunk=512):
    ctx_blk0 = (n_batch * seq) // ctx_len
    nq = seq // tq
    scale = float((A_NOPE + A_ROPE) ** -0.5)
    return pl.pallas_call(
        functools.partial(_flash_two_sets_kernel, scale=scale, chunk=chunk),
        grid=(n_batch, A_HEADS, nq),
        in_specs=[pl.BlockSpec((tq, A_HEAD_PAD), lambda b, h, i: (b * nq + i, h)),
                  pl.BlockSpec((seq, A_HEAD_PAD), lambda b, h, i: (b, h)),
                  pl.BlockSpec((seq, A_VDIM), lambda b, h, i: (b, h)),
                  pl.BlockSpec((ctx_len, A_HEAD_PAD), lambda b, h, i: (ctx_blk0 + b, h)),
                  pl.BlockSpec((ctx_len, A_VDIM), lambda b, h, i: (ctx_blk0 + b, h))],
        out_specs=pl.BlockSpec((tq, A_VDIM), lambda b, h, i: (b * nq + i, h)),
        out_shape=jax.ShapeDtypeStruct((n_batch * seq, A_HEADS * A_VDIM), BF16),
        compiler_params=_cparams(3, 48),
        name="mla_latent_attention",
    )(q, k, v, k, v)


def _attn_one_set_kernel(q_ref, k_ref, v_ref, o_ref, *, scale):
    s = _qk(q_ref[...], k_ref[...]) * scale
    p = jnp.exp(s - jnp.max(s, axis=-1, keepdims=True))
    den = jnp.sum(p, axis=-1, keepdims=True)
    o = jnp.dot(p.astype(BF16), v_ref[...], preferred_element_type=F32)
    o_ref[...] = (o / den).astype(o_ref.dtype)


def context_attention(q, k, v, n_batch, seq, ctx_len, n_heads, dqk, dv, q_col0, k_col0, v_col0, scale):
    ctx_blk0 = (n_batch * seq) // ctx_len
    return pl.pallas_call(
        functools.partial(_attn_one_set_kernel, scale=scale),
        grid=(n_batch, n_heads),
        in_specs=[pl.BlockSpec((ctx_len, dqk), lambda b, h: (ctx_blk0 + b, q_col0 + h)),
                  pl.BlockSpec((ctx_len, dqk), lambda b, h: (ctx_blk0 + b, k_col0 + h)),
                  pl.BlockSpec((ctx_len, dv), lambda b, h: (ctx_blk0 + b, v_col0 + h))],
        out_specs=pl.BlockSpec((ctx_len, dv), lambda b, h: (b, h)),
        out_shape=jax.ShapeDtypeStruct((n_batch * ctx_len, n_heads * dv), BF16),
        compiler_params=_cparams(2, 32),
        name="context_attention",
    )(q, k, v)


def _na_kernel(q_ref, kl_ref, vl_ref, kc_ref, vc_ref, *rest, rows, scale):
    bias_refs, o_ref = rest[:-1], rest[-1]
    win = WIN_H * GRID_W
    heads = [slice(h * B_DH, (h + 1) * B_DH) for h in range(B_HEADS)]
    chains = []
    for rr, bias_ref in enumerate(bias_refs):
        r = pl.program_id(1) * len(bias_refs) + rr
        r0 = jnp.clip(r - WIN_H // 2, 0, rows - WIN_H)
        start = pl.multiple_of(r0 * GRID_W, GRID_W)
        q_rows = slice(rr * GRID_W, (rr + 1) * GRID_W)
        chains += [(q_rows, cols, start, bias_ref, h) for h, cols in enumerate(heads)]
    scores = [(_qk(q_ref[q_rows, cols], kl_ref[pl.ds(start, win), cols]), _qk(q_ref[q_rows, cols], kc_ref[:, cols]))
              for q_rows, cols, start, _, _ in chains]
    probs = []
    for (_, _, _, bias_ref, h), (s_l, s_c) in zip(chains, scores):
        s_l = s_l * scale + bias_ref[0, h]
        s_c = s_c * scale
        m = jnp.maximum(jnp.max(s_l, axis=-1, keepdims=True), jnp.max(s_c, axis=-1, keepdims=True))
        p_l = jnp.exp(s_l - m)
        p_c = jnp.exp(s_c - m)
        den = jnp.sum(p_l, axis=-1, keepdims=True) + jnp.sum(p_c, axis=-1, keepdims=True)
        probs.append((p_l.astype(BF16), p_c.astype(BF16), den))
    outs = []
    for (_, cols, start, _, _), (p_l, p_c, den) in zip(chains, probs):
        o = jnp.dot(p_l, vl_ref[pl.ds(start, win), cols], preferred_element_type=F32)
        o = o + jnp.dot(p_c, vc_ref[:, cols], preferred_element_type=F32)
        outs.append(o / den)
    per_row = [jnp.concatenate(outs[i:i + B_HEADS], axis=1) for i in range(0, len(outs), B_HEADS)]
    o_ref[...] = jnp.concatenate(per_row, axis=0).astype(o_ref.dtype)


def na_latent_attention(qkv, bias, n_batch, seq, ctx_len):
    rows = seq // GRID_W
    ctx_blk0 = (n_batch * seq) // ctx_len
    win = WIN_H * GRID_W

    rows_per_step = 2
    steps = rows // rows_per_step

    def bias_spec(rr):
        def idx(b, s):
            r = s * rows_per_step + rr
            return (r - jnp.clip(r - WIN_H // 2, 0, rows - WIN_H), 0, 0, 0)
        return pl.BlockSpec((1, B_HEADS, GRID_W, win), idx)

    once = pl.Buffered(1)
    q_rows = rows_per_step * GRID_W
    return pl.pallas_call(
        functools.partial(_na_kernel, rows=rows, scale=float(B_DH ** -0.5)),
        grid=(n_batch, steps),
        in_specs=[pl.BlockSpec((q_rows, B_WIDTH), lambda b, s: (b * steps + s, 0)),
                  pl.BlockSpec((seq, B_WIDTH), lambda b, s: (b, 1), pipeline_mode=once),
                  pl.BlockSpec((seq, B_WIDTH), lambda b, s: (b, 2), pipeline_mode=once),
                  pl.BlockSpec((ctx_len, B_WIDTH), lambda b, s: (ctx_blk0 + b, 1)),
                  pl.BlockSpec((ctx_len, B_WIDTH), lambda b, s: (ctx_blk0 + b, 2))]
                 + [bias_spec(rr) for rr in range(rows_per_step)],
        out_specs=pl.BlockSpec((q_rows, B_WIDTH), lambda b, s: (b * steps + s, 0)),
        out_shape=jax.ShapeDtypeStruct((n_batch * seq, B_WIDTH), BF16),
        compiler_params=_cparams(2, 40),
        name="na_latent_attention",
    )(qkv, qkv, qkv, qkv, qkv, *([bias] * rows_per_step))


def na_bias_table(rel_bias):
    rel = rel_bias.astype(F32)
    n_layers = rel.shape[0]
    by_row = jnp.stack([rel[:, :, WIN_H - 1 - d:2 * WIN_H - 1 - d, :] for d in range(WIN_H)], axis=2)
    pad = GRID_W - 1
    by_row = jnp.pad(by_row, ((0, 0), (0, 0), (0, 0), (0, 0), (pad, pad)))
    off = WIN_W - 1 + pad
    tab = jnp.stack([by_row[..., off - c:off - c + GRID_W] for c in range(GRID_W)], axis=4)
    c = jnp.arange(GRID_W)[:, None]
    kc = jnp.arange(GRID_W)[None, :]
    c0 = jnp.clip(c - WIN_W // 2, 0, GRID_W - WIN_W)
    inside = (kc >= c0) & (kc < c0 + WIN_W)
    tab = jnp.where(inside[None, None, None, None], tab, NEG_BIG)
    return tab.transpose(0, 2, 1, 4, 3, 5).reshape(n_layers, WIN_H, B_HEADS, GRID_W, WIN_H * GRID_W)


def _pool_kernel(x_ref, w_ref, sc_ref, o_ref, pad_ref, *, seq_len, chunk):
    g = pl.program_id(1)
    zeros = jnp.zeros((POOL_HALO, C_GW), F32)
    pad_ref[0:POOL_HALO, :] = zeros
    pad_ref[POOL_HALO + seq_len:2 * POOL_HALO + seq_len, :] = zeros
    pad_ref[POOL_HALO:POOL_HALO + seq_len, :] = x_ref[...]
    for gi, w in enumerate(POOL_SIZES):
        @pl.when(g == gi)
        def _(w=w):
            for c0 in range(0, seq_len, chunk):
                acc = None
                for k in range(-(w // 2), w - w // 2):
                    piece = pad_ref[POOL_HALO + c0 + k:POOL_HALO + c0 + k + chunk, :]
                    acc = piece if acc is None else acc + piece
                t = c0 + lax.broadcasted_iota(jnp.int32, (chunk, C_GW), 0)
                cnt = jnp.minimum(t + (w - w // 2), seq_len) - jnp.maximum(t - w // 2, 0)
                d = (acc / cnt.astype(F32) - x_ref[c0:c0 + chunk, :]).astype(BF16)
                y = jnp.dot(d, w_ref[0], preferred_element_type=F32) * sc_ref[...]
                o_ref[c0:c0 + chunk, :] = y.astype(o_ref.dtype)


def pool_mix(p_pool, pool_w, pool_scale, n_seq, seq_len, first_block):
    chunk = min(seq_len, 512)
    return pl.pallas_call(
        functools.partial(_pool_kernel, seq_len=seq_len, chunk=chunk),
        grid=(n_seq, len(POOL_SIZES)),
        in_specs=[pl.BlockSpec((seq_len, C_GW), lambda b, g: (first_block + b, g)),
                  pl.BlockSpec((1, C_GW, C_GW), lambda b, g: (g, 0, 0)),
                  pl.BlockSpec((1, C_GW), lambda b, g: (0, g))],
        out_specs=pl.BlockSpec((seq_len, C_GW), lambda b, g: (b, g)),
        out_shape=jax.ShapeDtypeStruct((n_seq * seq_len, C_WIDTH), BF16),
        scratch_shapes=[pltpu.VMEM((seq_len + 2 * POOL_HALO, C_GW), F32)],
        compiler_params=_cparams(2, 48),
        name="pool_mix",
    )(p_pool, pool_w, pool_scale.reshape(1, C_WIDTH))


_REGION = [(r1, r2) for r1 in range(P_TOPK) for r2 in range(P_TOPK) if (r1 + 1) * (r2 + 1) <= P_TOPK]


def _dominates(a, b):
    return a != b and a[0] <= b[0] and a[1] <= b[1]


def _router_kernel(h_ref, wq_ref, k1_ref, k2_ref, cnt_ref, a_ref, rk_ref, b_ref,
                   s1_scr, s2_scr, v1_scr, i1_scr, v2_scr, i2_scr, cr_scr, e1_scr, e2_scr):
    tm = h_ref.shape[1]
    half = P_HEADS * P_DK // 2
    q_t = jnp.dot(wq_ref[...], h_ref[...], preferred_element_type=F32)
    key = lax.broadcasted_iota(jnp.int32, (P_NKEYS, P_HEADS, tm), 0)
    s1_scr[...] = jnp.dot(k1_ref[...], q_t[:half].astype(BF16),
                          preferred_element_type=F32).reshape(P_NKEYS, P_HEADS, tm)
    s2_scr[...] = jnp.dot(k2_ref[...], q_t[half:].astype(BF16),
                          preferred_element_type=F32).reshape(P_NKEYS, P_HEADS, tm)

    def first_max(s):
        level = [(s[k], k) for k in range(P_NKEYS)]
        while len(level) > 1:
            nxt = []
            for (va, ia), (vb, ib) in zip(level[0::2], level[1::2]):
                take = vb > va
                nxt.append((jnp.where(take, vb, va), jnp.where(take, ib, ia)))
            level = nxt
        return level[0]

    def extract(r, carry):
        for s_scr, v_scr, i_scr in ((s1_scr, v1_scr, i1_scr), (s2_scr, v2_scr, i2_scr)):
            s = s_scr[...]
            m, idx = first_max(s)
            s_scr[...] = jnp.where(key == idx[None], -jnp.inf, s)
            v_scr[r] = m
            i_scr[r] = idx
        return carry

    lax.fori_loop(0, P_TOPK, extract, 0)

    v1 = [v1_scr[r] for r in range(P_TOPK)]
    v2 = [v2_scr[r] for r in range(P_TOPK)]
    sums = {c: v1[c[0]] + v2[c[1]] for c in _REGION}
    beaten = {c: float(sum(_dominates(o, c) for o in _REGION)) for c in _REGION}
    for ci, c in enumerate(_REGION):
        for o in _REGION[:ci]:
            if _dominates(o, c):
                continue
            o_first = (sums[o] >= sums[c]).astype(F32)
            beaten[c] = beaten[c] + o_first
            beaten[o] = beaten[o] + (1.0 - o_first)
    e1 = [jnp.exp(v1[r] - v1[0]) for r in range(P_TOPK)]
    e2 = [jnp.exp(v2[r] - v2[0]) for r in range(P_TOPK)]
    z = jnp.zeros_like(v1[0])
    counts = [jnp.zeros_like(v1[0]) for _ in range(P_TOPK)]
    for c in _REGION:
        chosen = beaten[c] < float(P_TOPK)
        z = z + jnp.where(chosen, e1[c[0]] * e2[c[1]], 0.0)
        counts[c[0]] = counts[c[0]] + jnp.where(chosen, 1.0, 0.0)
    inv_z = 1.0 / z
    for r in range(P_TOPK):
        cr_scr[r] = counts[r]
        e1_scr[r] = e1[r]
        e2_scr[r] = e2[r] * inv_z

    key2 = lax.broadcasted_iota(jnp.int32, (P_NKEYS, LANES), 0)

    def expand(n, carry, h):
        t0 = pl.multiple_of(n * LANES, LANES)
        row = lambda ref, r: ref[r, h:h + 1, pl.ds(t0, LANES)]
        cnt = jnp.zeros((P_NKEYS, LANES), F32)
        a = cnt
        for r in range(P_TOPK):
            hit = key2 == row(i1_scr, r)
            cnt = jnp.where(hit, row(cr_scr, r), cnt)
            a = jnp.where(hit, row(e1_scr, r), a)
        cnt_ref[h, :, pl.ds(t0, LANES)] = cnt
        a_ref[h, :, pl.ds(t0, LANES)] = a
        rk = jnp.full((P_NKEYS, LANES), float(P_TOPK), F32)
        b = jnp.zeros((P_NKEYS, LANES), F32)
        for r in range(P_TOPK):
            hit = key2 == row(i2_scr, r)
            rk = jnp.where(hit, float(r), rk)
            b = jnp.where(hit, row(e2_scr, r), b)
        rk_ref[h, :, pl.ds(t0, LANES)] = rk.astype(rk_ref.dtype)
        b_ref[h, :, pl.ds(t0, LANES)] = b.astype(b_ref.dtype)
        return carry

    for h in range(P_HEADS):
        lax.fori_loop(0, tm // LANES, functools.partial(expand, h=h), 0)


def peer_router(h_t, wq_t, k1_packed, k2_packed, tm=256):
    d, n = h_t.shape
    qw = wq_t.shape[0]
    kp = k1_packed.shape[0]
    tab = lambda dt: jax.ShapeDtypeStruct((P_HEADS, P_NKEYS, n), dt)
    tab_spec = pl.BlockSpec((P_HEADS, P_NKEYS, tm), lambda i: (0, 0, i))
    small = lambda dt: pltpu.VMEM((P_TOPK, P_HEADS, tm), dt)
    return pl.pallas_call(
        _router_kernel,
        grid=(n // tm,),
        in_specs=[pl.BlockSpec((d, tm), lambda i: (0, i)),
                  pl.BlockSpec((qw, d), lambda i: (0, 0), pipeline_mode=pl.Buffered(1)),
                  pl.BlockSpec((kp, kp), lambda i: (0, 0), pipeline_mode=pl.Buffered(1)),
                  pl.BlockSpec((kp, kp), lambda i: (0, 0), pipeline_mode=pl.Buffered(1))],
        out_specs=[tab_spec] * 4,
        out_shape=[tab(F32), tab(F32), tab(GATE_DTYPE), tab(GATE_DTYPE)],
        scratch_shapes=[pltpu.VMEM((P_NKEYS, P_HEADS, tm), F32), pltpu.VMEM((P_NKEYS, P_HEADS, tm), F32),
                        small(F32), small(jnp.int32), small(F32), small(jnp.int32),
                        small(F32), small(F32), small(F32)],
        compiler_params=_cparams(1, 52),
        name="peer_router",
    )(h_t, wq_t, k1_packed, k2_packed)


def _gelu_exact(x):
    return 0.5 * x * (1.0 + lax.erf(x * float(np.sqrt(0.5))))


def _peer_kernel(h_ref, u_ref, v_ref, cnt_ref, a_ref, rk_ref, b_ref, o_ref):
    e = pl.program_id(1)
    te = u_ref.shape[0]
    tm = h_ref.shape[1]

    def step(first):
        halves = [(c0, jnp.dot(u_ref[c0:c0 + te // 2, :], h_ref[...], preferred_element_type=F32))
                  for c0 in (0, te // 2)]
        rows_per_half = te // 2 // P_NKEYS
        n_slab = P_NKEYS // GATE_SLAB
        ga_rows = []
        for c0, pre in halves:
            i0 = c0 // P_NKEYS
            rows = [[(cnt_ref[h, i0 + k:i0 + k + 1, :], a_ref[h, i0 + k:i0 + k + 1, :]) for h in range(P_HEADS)]
                    for k in range(rows_per_half)]
            tiles = {}
            for t0 in range(0, tm, LANES):
                lanes = slice(t0, t0 + LANES)
                g = [[None] * n_slab for _ in range(rows_per_half)]
                for h in range(P_HEADS):
                    per_row = [[jnp.broadcast_to(x[:, lanes].astype(GATE_DTYPE), (GATE_SLAB, LANES)) for x in rows[k][h]]
                               for k in range(rows_per_half)]
                    for s in range(n_slab):
                        slab = slice(s * GATE_SLAB, (s + 1) * GATE_SLAB)
                        rk = rk_ref[h, slab, lanes]
                        b = b_ref[h, slab, lanes]
                        for k, (cnt, a) in enumerate(per_row):
                            term = jnp.where(rk < cnt, b * a, jnp.zeros_like(b))
                            g[k][s] = term if g[k][s] is None else g[k][s] + term
                for k in range(rows_per_half):
                    for s in range(n_slab):
                        r = k * P_NKEYS + s * GATE_SLAB
                        tiles[(k, s, t0)] = g[k][s].astype(F32) * _gelu_exact(pre[r:r + GATE_SLAB, lanes])
            for k in range(rows_per_half):
                for s in range(n_slab):
                    ga_rows.append(jnp.concatenate([tiles[(k, s, t0)] for t0 in range(0, tm, LANES)], axis=1))
        ga = jnp.concatenate(ga_rows, axis=0).astype(BF16)
        contrib = lax.dot_general(v_ref[...], ga, (((0,), (0,)), ((), ())), preferred_element_type=F32)
        if first:
            o_ref[...] = contrib
        else:
            o_ref[...] += contrib

    pl.when(e == 0)(functools.partial(step, True))
    pl.when(e > 0)(functools.partial(step, False))


def peer_mixture(h_t, u_all, v_all, layer, tables, tm=512, te=1024):
    d, n = h_t.shape
    once = pl.Buffered(1)
    key_tab = pl.BlockSpec((P_HEADS, P_NKEYS, tm), lambda i, e: (0, 0, i), pipeline_mode=once)
    row_tab = pl.BlockSpec((P_HEADS, te // P_NKEYS, tm), lambda i, e: (0, e, i))
    expert_tile = pl.BlockSpec((None, te, d), lambda i, e: (layer, e, 0))
    return pl.pallas_call(
        _peer_kernel,
        grid=(n // tm, u_all.shape[1] // te),
        in_specs=[pl.BlockSpec((d, tm), lambda i, e: (0, i), pipeline_mode=once),
                  expert_tile, expert_tile, row_tab, row_tab, key_tab, key_tab],
        out_specs=pl.BlockSpec((d, tm), lambda i, e: (0, i), pipeline_mode=once),
        out_shape=jax.ShapeDtypeStruct((d, n), F32),
        compiler_params=_cparams(2, 58),
        name="peer_mixture",
    )(h_t, u_all, v_all, *tables)


def _residual_t_ln_kernel(x_ref, y_ref, g_ref, ng_ref, sc_ref, sh_ref, o_ref, h_ref):
    x = x_ref[...] + g_ref[0] * y_ref[...].T
    o_ref[...] = x
    h_ref[...] = (_rms(x, ng_ref[...]) * (1.0 + sc_ref[0]) + sh_ref[0]).astype(h_ref.dtype)


def _residual_t_final_kernel(x_ref, y_ref, g_ref, ng_ref, o_ref):
    o_ref[...] = _rms(x_ref[...] + g_ref[0] * y_ref[...].T, ng_ref[...])


def residual_from_transposed(x, y_t, mod3, gate_chunk, n_rows, seq, n_batch, norm_g, next_mod3=None):
    d = x.shape[1]
    tm = 256
    seg = functools.partial(_segment, tile_rows=tm, seq=seq, n_batch=n_batch)
    row = pl.BlockSpec((tm, d), lambda i: (i, 0))
    in_specs = [row,
                pl.BlockSpec((d, tm), lambda i: (0, i)),
                pl.BlockSpec((1, 1, d), lambda i: (seg(i), 0, gate_chunk)),
                pl.BlockSpec((1, d), lambda i: (0, 0))]
    if next_mod3 is None:
        return pl.pallas_call(
            _residual_t_final_kernel,
            grid=(n_rows // tm,),
            in_specs=in_specs,
            out_specs=row,
            out_shape=jax.ShapeDtypeStruct((n_rows, d), F32),
            compiler_params=_cparams(1, 48),
            name="residual_final_norm",
        )(x, y_t, mod3, norm_g.reshape(1, d))
    return pl.pallas_call(
        _residual_t_ln_kernel,
        grid=(n_rows // tm,),
        in_specs=in_specs + [pl.BlockSpec((1, 1, d), lambda i: (seg(i), 0, 1)),
                             pl.BlockSpec((1, 1, d), lambda i: (seg(i), 0, 0))],
        out_specs=[row, row],
        out_shape=[jax.ShapeDtypeStruct((n_rows, d), F32), jax.ShapeDtypeStruct((n_rows, d), BF16)],
        compiler_params=_cparams(1, 48),
        name="residual_next_norm",
    )(x, y_t, mod3, norm_g.reshape(1, d), next_mod3, next_mod3)


def _rope_tables(seq, n_batch, n_ctx_rows):
    t = jnp.arange(seq)
    row = (t // GRID_W).astype(F32)
    col = (t % GRID_W).astype(F32)
    n_freq = A_ROPE // 4
    freqs = ROPE_THETA ** (-jnp.arange(n_freq, dtype=F32) / n_freq)
    ang = jnp.concatenate([row[:, None] * freqs, col[:, None] * freqs], axis=-1)
    cos, sin = jnp.cos(ang), jnp.sin(ang)
    zero = jnp.zeros_like(cos)
    pad = jnp.zeros((seq, LANES - A_ROPE), F32)
    c = jnp.concatenate([cos, cos, pad], axis=1)
    s1 = jnp.concatenate([-sin, zero, pad], axis=1)
    s2 = jnp.concatenate([zero, sin, pad], axis=1)
    ident = jnp.concatenate([jnp.ones((n_ctx_rows, A_ROPE), F32), jnp.zeros((n_ctx_rows, LANES - A_ROPE), F32)], 1)
    none = jnp.zeros((n_ctx_rows, LANES), F32)
    tile = lambda a, ctx: jnp.concatenate([jnp.tile(a, (n_batch, 1)), ctx], axis=0)
    return tile(c, ident), tile(s1, none), tile(s2, none)


def _pack_keys(k):
    h, nk, dk = k.shape
    eye = jnp.eye(h, dtype=k.dtype)
    blk = k.transpose(1, 0, 2)[:, :, None, :] * eye[None, :, :, None]
    return blk.reshape(nk * h, h * dk).astype(BF16)


def _prepare_layer(w_in, w_uq, w_ukv, w_br_a, w_br_b, w_br_c, w_out, peer_wq, peer_k1, peer_k2):
    d = w_in.shape[0]
    o_kr = A_QLORA + A_KVLORA
    o_q = o_kr + A_ROPE
    o_pool = o_q + 3 * B_WIDTH
    o_gate = o_pool + C_WIDTH
    w_small = jnp.concatenate([w_in[:, :o_q], jnp.zeros((d, LANES - A_ROPE), w_in.dtype)], axis=1)
    w_uq_pad = jnp.pad(w_uq.reshape(A_QLORA, A_HEADS, A_NOPE + A_ROPE),
                       ((0, 0), (0, 0), (0, A_HEAD_PAD - A_NOPE - A_ROPE))).reshape(A_QLORA, A_HEADS * A_HEAD_PAD)
    w_ukv_perm = w_ukv.reshape(A_KVLORA, A_HEADS, 2, A_NOPE).transpose(0, 2, 1, 3).reshape(A_KVLORA, -1)
    wq_t = peer_wq.reshape(d, P_HEADS, 2, P_DK // 2).transpose(2, 1, 3, 0).reshape(P_HEADS * P_DK, d)
    return dict(
        w_small=w_small.astype(BF16),
        w_qkv=w_in[:, o_q:o_pool].astype(BF16),
        w_pool=w_in[:, o_pool:o_gate].astype(BF16),
        w_gate=w_in[:, o_gate:].astype(BF16),
        w_uq=w_uq_pad.astype(BF16),
        w_ukv=w_ukv_perm.astype(BF16),
        w_br_a=w_br_a.astype(BF16), w_br_b=w_br_b.astype(BF16), w_br_c=w_br_c.astype(BF16),
        w_out=w_out.astype(BF16),
        wq_t=wq_t.astype(BF16),
        k1=_pack_keys(peer_k1), k2=_pack_keys(peer_k2),
    )


def kernel(x, c, ctx, c_ctx, w_ada, b_ada, norm1_g, norm2_g, w_in, q_norm_g, kv_norm_g, w_uq, w_ukv, na_rel_bias, pool_w, pool_scale, w_br_a, w_br_b, w_br_c, w_out, peer_wq, peer_k1, peer_k2, peer_u, peer_v, final_g):
    n_batch, seq, d = x.shape
    ctx_len = ctx.shape[1]
    depth = w_ada.shape[0]
    n_lat = n_batch * seq
    n_ctx = n_batch * ctx_len
    nt = n_lat + n_ctx
    ctx_blk0 = n_lat // ctx_len

    stream = jnp.concatenate([x.reshape(n_lat, d), ctx.reshape(n_ctx, d)], axis=0)
    mod_rows = 8
    cvec = jnp.concatenate([c, c_ctx[None], jnp.zeros((mod_rows - n_batch - 1, d), c.dtype)], axis=0)
    mod = ada_modulation(cvec, w_ada, b_ada)
    rope_c, rope_s1, rope_s2 = _rope_tables(seq, n_batch, n_ctx)
    na_bias = na_bias_table(na_rel_bias)
    mod3s = [mod[l].reshape(mod_rows, 1, 6 * d) for l in range(depth)]
    u_all = peer_u.astype(BF16)
    v_all = peer_v.astype(BF16)

    h = ln_modulate(stream, norm1_g[0], mod3s[0], 0, nt, seq, n_batch)
    for l in range(depth):
        ctx_out = l < depth - 1
        n_act = nt if ctx_out else n_lat
        wl = _prepare_layer(w_in[l], w_uq[l], w_ukv[l], w_br_a[l], w_br_b[l], w_br_c[l], w_out[l],
                            peer_wq[l], peer_k1[l], peer_k2[l])
        mod3 = mod3s[l]

        p_small = matmul(h, wl["w_small"], F32)
        qkv = matmul(h, wl["w_qkv"], BF16)
        p_pool = matmul(h, wl["w_pool"], F32)
        gates = matmul(h, wl["w_gate"], F32, act="sigmoid", n_rows=n_act)

        q = mla_q(p_small, q_norm_g[l], wl["w_uq"], rope_c, rope_s1, rope_s2)
        k, v = mla_kv(p_small, kv_norm_g[l], wl["w_ukv"], rope_c, rope_s1, rope_s2)
        a_br = mla_latent_attention(q, k, v, n_batch, seq, ctx_len)
        b_br = na_latent_attention(qkv, na_bias[l], n_batch, seq, ctx_len)
        pw = pool_w[l].astype(BF16)
        c_br = pool_mix(p_pool, pw, pool_scale[l], n_batch, seq, 0)
        if ctx_out:
            a_ctx = context_attention(q, k, v, n_batch, seq, ctx_len, A_HEADS, A_HEAD_PAD, A_VDIM, 0, 0, 0,
                                      float((A_NOPE + A_ROPE) ** -0.5))
            b_ctx = context_attention(qkv, qkv, qkv, n_batch, seq, ctx_len, B_HEADS, B_DH, B_DH,
                                      0, B_HEADS, 2 * B_HEADS, float(B_DH ** -0.5))
            c_ctx_br = pool_mix(p_pool, pw, pool_scale[l], n_batch, ctx_len, ctx_blk0)
            a_br = jnp.concatenate([a_br, a_ctx], axis=0)
            b_br = jnp.concatenate([b_br, b_ctx], axis=0)
            c_br = jnp.concatenate([c_br, c_ctx_br], axis=0)

        merged = merge_branches(a_br, b_br, c_br, wl["w_br_a"], wl["w_br_b"], wl["w_br_c"], gates)
        stream = matmul_gated_residual(merged, wl["w_out"], stream, mod3, 2, seq, n_batch)

        h2_t = ln_modulate(stream, norm2_g[l], mod3, 3, n_act, seq, n_batch, transpose=True)
        tables = peer_router(h2_t, wl["wq_t"], wl["k1"], wl["k2"])
        y_t = peer_mixture(h2_t, u_all, v_all, l, tables)
        if ctx_out:
            stream, h = residual_from_transposed(stream, y_t, mod3, 5, n_act, seq, n_batch,
                                                 norm1_g[l + 1], mod3s[l + 1])
        else:
            out = residual_from_transposed(stream, y_t, mod3, 5, n_act, seq, n_batch, final_g)

    return out.reshape(n_batch, seq, d)
```
